```python
import jax, jax.numpy as jnp
from jax import lax
import numpy as np

D_MODEL = 4096
BATCH = 16
SEQ = 256
DEPTH = 4
DEC_BATCH = 2
DEC_SEQ = 2048
PAST_LEN = 512

GRID_W = 64
Q_BLOCK = 128
EPS = 1e-6
ROPE_THETA = 10000.0

LRU_WIDTH = D_MODEL // 4
LRU_BLOCKS = 16
LRU_BLOCK = LRU_WIDTH // LRU_BLOCKS
LRU_C = 8.0
CONV_W = 4
CONV_PAD_L = 2

HEAD_DIM = 128
GQA_HEADS = (3 * D_MODEL // 8) // HEAD_DIM
GQA_KV_HEADS = 4
GQA_WIDTH = GQA_HEADS * HEAD_DIM

MLA_HEADS = (3 * D_MODEL // 8) // 128
MLA_Q_RANK = D_MODEL // 4
MLA_KV_RANK = 512
MLA_NOPE = 128
MLA_ROPE = 64
MLA_V = 128
MLA_WIDTH = MLA_HEADS * MLA_V

D_MIX = LRU_WIDTH + GQA_WIDTH + MLA_WIDTH
D_FF = -(-8 * D_MODEL // 768) * 256

SPLIT_SIZES = (LRU_WIDTH, LRU_WIDTH, GQA_WIDTH, GQA_KV_HEADS * HEAD_DIM, GQA_KV_HEADS * HEAD_DIM,
               MLA_Q_RANK, MLA_KV_RANK, MLA_ROPE)
IN_COLS = sum(SPLIT_SIZES)

kernel_name = 'hybrid_lru_gqa_mla_prefix_dit_step'


def rmsnorm(x, g):
    xf = x.astype(jnp.float32)
    y = xf * lax.rsqrt(jnp.mean(xf * xf, axis=-1, keepdims=True) + EPS)
    return (y * g.astype(jnp.float32)).astype(x.dtype)


def modulation(cond, w_mod, b_mod):
    m = jax.nn.silu(cond) @ w_mod + b_mod
    return jnp.split(m[:, None, :], 6, axis=-1)


def axial_rope(n_tokens, rot_dim):
    rows = n_tokens // GRID_W
    row = jnp.repeat(jnp.arange(rows, dtype=jnp.float32), GRID_W)
    col = jnp.tile(jnp.arange(GRID_W, dtype=jnp.float32), rows)
    quarter = rot_dim // 4
    freqs = ROPE_THETA ** (-jnp.arange(quarter, dtype=jnp.float32) / quarter)
    ang = jnp.concatenate([row[:, None] * freqs, col[:, None] * freqs], axis=-1)
    return jnp.cos(ang), jnp.sin(ang)


def apply_rope(x, cos, sin):
    xf = x.astype(jnp.float32).reshape(x.shape[:-1] + (x.shape[-1] // 2, 2))
    x0, x1 = xf[..., 0], xf[..., 1]
    cc, ss = cos[None, :, None, :], sin[None, :, None, :]
    out = jnp.stack([x0 * cc - x1 * ss, x0 * ss + x1 * cc], axis=-1)
    return out.reshape(x.shape).astype(x.dtype)


def split_proj(p):
    bounds = [int(b) for b in np.cumsum(SPLIT_SIZES)[:-1]]
    return jnp.split(p, bounds, axis=-1)


def conv_centred(x, w, b):
    t = x.shape[1]
    xp = jnp.pad(x, ((0, 0), (CONV_PAD_L, CONV_W - 1 - CONV_PAD_L), (0, 0)))
    y = b
    for k in range(CONV_W):
        y = y + xp[:, k:k + t] * w[k]
    return y


def _lin_combine(e1, e2):
    a1, b1 = e1
    a2, b2 = e2
    return a1 * a2, a2 * b1 + b2


def rglru(xc, wr, br, wi, bi, lam, h0, reverse):
    bsz, t, w = xc.shape
    xb = xc.reshape(bsz, t, LRU_BLOCKS, LRU_BLOCK)
    r = jax.nn.sigmoid(jnp.einsum('btnk,nkj->btnj', xb, wr).reshape(bsz, t, w) + br).astype(jnp.float32)
    i = jax.nn.sigmoid(jnp.einsum('btnk,nkj->btnj', xb, wi).reshape(bsz, t, w) + bi)
    log_a = -LRU_C * r * jax.nn.softplus(-lam.astype(jnp.float32))
    a = jnp.exp(log_a)
    b = jnp.sqrt(-jnp.expm1(2.0 * log_a)) * (i * xc).astype(jnp.float32)
    a_cum, h = lax.associative_scan(_lin_combine, (a, b), reverse=reverse, axis=1)
    if h0 is not None:
        h = h + a_cum * h0.astype(jnp.float32)[:, None, :]
    return h


def block_attention(q, k, v, scale):
    bsz, tq, g, rr, dk = q.shape
    nb = tq // Q_BLOCK
    qb = jnp.moveaxis(q.reshape(bsz, nb, Q_BLOCK, g, rr, dk), 1, 0)

    def one_block(qblk):
        s = jnp.einsum('bqgrd,bkgd->bgrqk', qblk, k).astype(jnp.float32) * scale
        p = jax.nn.softmax(s, axis=-1).astype(v.dtype)
        return jnp.einsum('bgrqk,bkgd->bqgrd', p, v)

    o = lax.map(one_block, qb)
    return jnp.moveaxis(o, 0, 1).reshape(bsz, tq, g * rr, v.shape[-1])


def gqa_attend(q, k, v):
    bsz, t = q.shape[0], q.shape[1]
    qg = q.reshape(bsz, t, GQA_KV_HEADS, GQA_HEADS // GQA_KV_HEADS, HEAD_DIM)
    return block_attention(qg, k, v, HEAD_DIM ** -0.5).reshape(bsz, t, GQA_WIDTH)


def mla_queries(cq, qn, w_uq):
    bsz, t = cq.shape[0], cq.shape[1]
    qf = (rmsnorm(cq, qn) @ w_uq).reshape(bsz, t, MLA_HEADS, MLA_NOPE + MLA_ROPE)
    return qf[..., :MLA_NOPE], qf[..., MLA_NOPE:]


def mla_attend(q_nope, q_rope, ckv, k_rope, w_ukv):
    bsz, tq = q_nope.shape[0], q_nope.shape[1]
    tk = ckv.shape[1]
    kv = (ckv @ w_ukv).reshape(bsz, tk, MLA_HEADS, MLA_NOPE + MLA_V)
    k_nope, v = kv[..., :MLA_NOPE], kv[..., MLA_NOPE:]
    k = jnp.concatenate([k_nope, jnp.broadcast_to(k_rope[:, :, None, :], (bsz, tk, MLA_HEADS, MLA_ROPE))], axis=-1)
    q = jnp.concatenate([q_nope, q_rope], axis=-1)[:, :, :, None, :]
    o = block_attention(q, k, v, (MLA_NOPE + MLA_ROPE) ** -0.5)
    return o.reshape(bsz, tq, MLA_WIDTH)


def swiglu(h, w_gate, w_up, w_down):
    return (jax.nn.silu(h @ w_gate) * (h @ w_up)) @ w_down


def context_mixers(h, lw):
    bsz, t, _ = h.shape
    xa, ga, q, k, v, cq, ckv, kr = split_proj(h @ lw['w_in'])
    xc = conv_centred(xa, lw['conv_w'], lw['conv_b'])
    hf = rglru(xc, lw['lru_wr'][0], lw['lru_br'][0], lw['lru_wi'][0], lw['lru_bi'][0], lw['lru_lam'][0], None, False)
    hb = rglru(xc, lw['lru_wr'][1], lw['lru_br'][1], lw['lru_wi'][1], lw['lru_bi'][1], lw['lru_lam'][1], None, True)
    out_a = (hf + hb).astype(h.dtype) * jax.nn.gelu(ga)
    lru_state = jnp.stack([hf[:, -1], hb[:, 0]], axis=1).astype(h.dtype)
    q = rmsnorm(q.reshape(bsz, t, GQA_HEADS, HEAD_DIM), lw['gqa_qn'])
    k = rmsnorm(k.reshape(bsz, t, GQA_KV_HEADS, HEAD_DIM), lw['gqa_kn'])
    v = v.reshape(bsz, t, GQA_KV_HEADS, HEAD_DIM)
    out_b = gqa_attend(q, k, v)
    ckv = rmsnorm(ckv, lw['mla_kvn'])
    q_nope, q_rope = mla_queries(cq, lw['mla_qn'], lw['w_uq'])
    out_c = mla_attend(q_nope, q_rope, ckv, kr, lw['w_ukv'])
    mix = jnp.concatenate([out_a, out_b, out_c], axis=-1)
    return mix, (lru_state, k, v, ckv, kr)


def latent_mixers(h, lw, ctx, rope_b, rope_c):
    st, ck, cv, cckv, ckr = ctx
    bsz, t, _ = h.shape
    xa, ga, q, k, v, cq, ckv, kr = split_proj(h @ lw['w_in'])
    xc = conv_centred(xa, lw['conv_w'], lw['conv_b'])
    hf = rglru(xc, lw['lru_wr'][0], lw['lru_br'][0], lw['lru_wi'][0], lw['lru_bi'][0], lw['lru_lam'][0], st[:, 0], False)
    hb = rglru(xc, lw['lru_wr'][1], lw['lru_br'][1], lw['lru_wi'][1], lw['lru_bi'][1], lw['lru_lam'][1], st[:, 1], True)
    out_a = (hf + hb).astype(h.dtype) * jax.nn.gelu(ga)
    q = apply_rope(rmsnorm(q.reshape(bsz, t, GQA_HEADS, HEAD_DIM), lw['gqa_qn']), *rope_b)
    k = apply_rope(rmsnorm(k.reshape(bsz, t, GQA_KV_HEADS, HEAD_DIM), lw['gqa_kn']), *rope_b)
    v = v.reshape(bsz, t, GQA_KV_HEADS, HEAD_DIM)
    out_b = gqa_attend(q, jnp.concatenate([ck, k], axis=1), jnp.concatenate([cv, v], axis=1))
    ckv = rmsnorm(ckv, lw['mla_kvn'])
    q_nope, q_rope = mla_queries(cq, lw['mla_qn'], lw['w_uq'])
    q_rope = apply_rope(q_rope, *rope_c)
    kr = apply_rope(kr[:, :, None, :], *rope_c)[:, :, 0]
    out_c = mla_attend(q_nope, q_rope, jnp.concatenate([cckv, ckv], axis=1),
                       jnp.concatenate([ckr, kr], axis=1), lw['w_ukv'])
    return jnp.concatenate([out_a, out_b, out_c], axis=-1)


def layer(x, cond, lw, mixer):
    sh1, sc1, g1, sh2, sc2, g2 = modulation(cond, lw['w_mod'], lw['b_mod'])
    h = rmsnorm(x, lw['norm_mix']) * (1 + sc1) + sh1
    mix, extra = mixer(h)
    x = x + g1 * (mix @ lw['w_out'])
    h = rmsnorm(x, lw['norm_ffn']) * (1 + sc2) + sh2
    x = x + g2 * swiglu(h, lw['w_gate'], lw['w_up'], lw['w_down'])
    return x, extra


def setup_inputs(seed: int = 0) -> dict:
    key = jax.random.key(seed)
    ks = iter(jax.random.split(key, 40))
    f32 = jnp.float32

    def nrm(shape, scale):
        return scale * jax.random.normal(next(ks), shape, f32)

    def gain(shape):
        return 1.0 + 0.01 * jax.random.normal(next(ks), shape, f32)

    u = jax.random.uniform(next(ks), (DEPTH, 2, LRU_WIDTH), f32, minval=0.9, maxval=0.999)
    lru_lam = jnp.log(u) - jnp.log1p(-u)
    return {
        'x_prompt': nrm((BATCH, SEQ, D_MODEL), 1.0),
        'x_sample': nrm((DEC_BATCH, DEC_SEQ, D_MODEL), 1.0),
        'state_lru': nrm((DEC_BATCH, DEPTH, 2, LRU_WIDTH), 0.5),
        'cache_gqa_k': nrm((DEC_BATCH, DEPTH, PAST_LEN, GQA_KV_HEADS, HEAD_DIM), 1.0),
        'cache_gqa_v': nrm((DEC_BATCH, DEPTH, PAST_LEN, GQA_KV_HEADS, HEAD_DIM), 1.0),
        'cache_mla_ckv': nrm((DEC_BATCH, DEPTH, PAST_LEN, MLA_KV_RANK), 1.0),
        'cache_mla_krope': nrm((DEC_BATCH, DEPTH, PAST_LEN, MLA_ROPE), 1.0),
        'c': nrm((DEC_BATCH, D_MODEL), 1.0),
        'c_ctx': nrm((D_MODEL,), 1.0),
        'w_mod': nrm((DEPTH, D_MODEL, 6 * D_MODEL), D_MODEL ** -0.5),
        'b_mod': nrm((DEPTH, 6 * D_MODEL), 0.01),
        'norm_mix_g': gain((DEPTH, D_MODEL)),
        'norm_ffn_g': gain((DEPTH, D_MODEL)),
        'w_in': nrm((DEPTH, D_MODEL, IN_COLS), D_MODEL ** -0.5),
        'conv_w': nrm((DEPTH, CONV_W, LRU_WIDTH), 0.5),
        'conv_b': nrm((DEPTH, LRU_WIDTH), 0.01),
        'lru_wr': nrm((DEPTH, 2, LRU_BLOCKS, LRU_BLOCK, LRU_BLOCK), LRU_BLOCK ** -0.5),
        'lru_br': nrm((DEPTH, 2, LRU_WIDTH), 0.01),
        'lru_wi': nrm((DEPTH, 2, LRU_BLOCKS, LRU_BLOCK, LRU_BLOCK), LRU_BLOCK ** -0.5),
        'lru_bi': nrm((DEPTH, 2, LRU_WIDTH), 0.01),
        'lru_lam': lru_lam,
        'gqa_qn': gain((DEPTH, HEAD_DIM)),
        'gqa_kn': gain((DEPTH, HEAD_DIM)),
        'mla_qn': gain((DEPTH, MLA_Q_RANK)),
        'mla_kvn': gain((DEPTH, MLA_KV_RANK)),
        'w_uq': nrm((DEPTH, MLA_Q_RANK, MLA_HEADS * (MLA_NOPE + MLA_ROPE)), MLA_Q_RANK ** -0.5),
        'w_ukv': nrm((DEPTH, MLA_KV_RANK, MLA_HEADS * (MLA_NOPE + MLA_V)), MLA_KV_RANK ** -0.5),
        'w_out': nrm((DEPTH, D_MIX, D_MODEL), D_MIX ** -0.5),
        'w_gate': nrm((DEPTH, D_MODEL, D_FF), D_MODEL ** -0.5),
        'w_up': nrm((DEPTH, D_MODEL, D_FF), D_MODEL ** -0.5),
        'w_down': nrm((DEPTH, D_FF, D_MODEL), D_FF ** -0.5),
        'norm_f': gain((D_MODEL,)),
    }


def reference(x_prompt, x_sample, state_lru, cache_gqa_k, cache_gqa_v, cache_mla_ckv, cache_mla_krope, c,
              c_ctx, w_mod, b_mod, norm_mix_g, norm_ffn_g, w_in, conv_w, conv_b, lru_wr, lru_br, lru_wi, lru_bi,
              lru_lam, gqa_qn, gqa_kn, mla_qn, mla_kvn, w_uq, w_ukv, w_out, w_gate, w_up, w_down, norm_f):
    n_lat = x_sample.shape[1]
    rope_b = axial_rope(n_lat, HEAD_DIM)
    rope_c = axial_rope(n_lat, MLA_ROPE)
    cond_ctx = c_ctx[None, :]
    xp, xs = x_prompt, x_sample
    st_lru, st_k, st_v, st_ckv, st_kr = [], [], [], [], []
    for l in range(DEPTH):
        lw = {
            'w_mod': w_mod[l], 'b_mod': b_mod[l], 'norm_mix': norm_mix_g[l], 'norm_ffn': norm_ffn_g[l],
            'w_in': w_in[l], 'conv_w': conv_w[l], 'conv_b': conv_b[l],
            'lru_wr': lru_wr[l], 'lru_br': lru_br[l], 'lru_wi': lru_wi[l], 'lru_bi': lru_bi[l], 'lru_lam': lru_lam[l],
            'gqa_qn': gqa_qn[l], 'gqa_kn': gqa_kn[l], 'mla_qn': mla_qn[l], 'mla_kvn': mla_kvn[l],
            'w_uq': w_uq[l], 'w_ukv': w_ukv[l], 'w_out': w_out[l],
            'w_gate': w_gate[l], 'w_up': w_up[l], 'w_down': w_down[l],
        }
        xp, (s_lru, s_k, s_v, s_ckv, s_kr) = layer(xp, cond_ctx, lw, lambda h: context_mixers(h, lw))
        st_lru.append(s_lru)
        st_k.append(s_k)
        st_v.append(s_v)
        st_ckv.append(s_ckv)
        st_kr.append(s_kr)
        ctx = (state_lru[:, l], cache_gqa_k[:, l], cache_gqa_v[:, l], cache_mla_ckv[:, l], cache_mla_krope[:, l])
        xs, _ = layer(xs, c, lw, lambda h: (latent_mixers(h, lw, ctx, rope_b, rope_c), None))
    y_prompt = rmsnorm(xp, norm_f)
    y_sample = rmsnorm(xs, norm_f)
    return (y_prompt, y_sample, jnp.stack(st_lru, axis=1), jnp.stack(st_k, axis=1), jnp.stack(st_v, axis=1),
            jnp.stack(st_ckv, axis=1), jnp.stack(st_kr, axis=1))
```

```python
import functools

import jax
import jax.numpy as jnp
from jax import lax
from jax.experimental import pallas as pl
from jax.experimental.pallas import tpu as pltpu

F32 = jnp.float32
BF16 = jnp.bfloat16

D_MODEL = 4096
BATCH = 16
SEQ = 256
DEPTH = 4
DEC_BATCH = 2
DEC_SEQ = 2048
PAST_LEN = 512
GRID_W = 64
EPS = 1e-6
ROPE_THETA = 10000.0

LRU_WIDTH = 1024
LRU_BLOCKS = 16
LRU_BLOCK = 64
LRU_C = 8.0
HEAD_DIM = 128
GQA_HEADS = 12
GQA_KV_HEADS = 4
GQA_REP = GQA_HEADS // GQA_KV_HEADS
GQA_WIDTH = GQA_HEADS * HEAD_DIM
KV_WIDTH = GQA_KV_HEADS * HEAD_DIM
MLA_HEADS = 12
MLA_Q_RANK = 1024
MLA_KV_RANK = 512
MLA_NOPE = 128
MLA_ROPE = 64
MLA_V = 128
MLA_WIDTH = MLA_HEADS * MLA_V
D_FF = 11008
IN_MAIN = 6144
LANE = 128
N_CTX = BATCH * SEQ
N_LAT = DEC_BATCH * DEC_SEQ
N_TOK = N_CTX + N_LAT
LAT_KEYS = PAST_LEN + DEC_SEQ
VMEM_LIMIT = 56 * 1024 * 1024
LRU_CHUNK = 256
ROW_TILE = 256


def _params(*sem):
    return pltpu.CompilerParams(dimension_semantics=sem, vmem_limit_bytes=VMEM_LIMIT)


def _cond_of_tile(i, tm):
    row = i * tm
    return jnp.where(row < N_CTX, 0, 1 + (row - N_CTX) // DEC_SEQ)


def _silu(x):
    return x * jax.nn.sigmoid(x)


def _dot(a, b):
    return jnp.dot(a, b, preferred_element_type=F32)


def _dot_nt(a, b):
    return lax.dot_general(a, b, (((1,), (1,)), ((), ())), preferred_element_type=F32)


def _mod_kernel(c_ref, w_ref, b_ref, o_ref):
    s = _silu(c_ref[...]).astype(BF16)
    o_ref[...] = _dot(s, w_ref[...].astype(BF16)) + b_ref[...]


def modulation_all(cond8, w_mod, b_mod):
    tn = 512
    n = 6 * D_MODEL
    return pl.pallas_call(
        _mod_kernel,
        grid=(DEPTH, n // tn),
        in_specs=[
            pl.BlockSpec((8, D_MODEL), lambda l, j: (0, 0)),
            pl.BlockSpec((None, D_MODEL, tn), lambda l, j: (l, 0, j)),
            pl.BlockSpec((None, 1, tn), lambda l, j: (l, 0, j)),
        ],
        out_specs=pl.BlockSpec((None, 8, tn), lambda l, j: (l, 0, j)),
        out_shape=jax.ShapeDtypeStruct((DEPTH, 8, n), F32),
        compiler_params=_params("parallel", "parallel"),
        name="modulation",
    )(cond8, w_mod, b_mod.reshape(DEPTH, 1, n))


def _norm_mod_kernel(x_ref, g_ref, sc_ref, sh_ref, o_ref):
    x = x_ref[...]
    y = x * lax.rsqrt(jnp.mean(x * x, axis=-1, keepdims=True) + EPS) * g_ref[...]
    o_ref[...] = (y * (1.0 + sc_ref[...]) + sh_ref[...]).astype(o_ref.dtype)


def norm_mod(x, g, sc, sh):
    tm = ROW_TILE
    m = x.shape[0]
    return pl.pallas_call(
        _norm_mod_kernel,
        grid=(m // tm,),
        in_specs=[
            pl.BlockSpec((tm, D_MODEL), lambda i: (i, 0)),
            pl.BlockSpec((1, D_MODEL), lambda i: (0, 0)),
            pl.BlockSpec((None, 1, D_MODEL), lambda i: (_cond_of_tile(i, tm), 0, 0)),
            pl.BlockSpec((None, 1, D_MODEL), lambda i: (_cond_of_tile(i, tm), 0, 0)),
        ],
        out_specs=pl.BlockSpec((tm, D_MODEL), lambda i: (i, 0)),
        out_shape=jax.ShapeDtypeStruct((m, D_MODEL), BF16),
        compiler_params=_params("parallel"),
        name="norm_mod",
    )(x, g.reshape(1, D_MODEL), sc, sh)


def _final_norm_kernel(x_ref, g_ref, o_ref):
    x = x_ref[...]
    o_ref[...] = x * lax.rsqrt(jnp.mean(x * x, axis=-1, keepdims=True) + EPS) * g_ref[...]


def final_norm(x, g):
    tm = ROW_TILE
    m = x.shape[0]
    return pl.pallas_call(
        _final_norm_kernel,
        grid=(m // tm,),
        in_specs=[
            pl.BlockSpec((tm, D_MODEL), lambda i: (i, 0)),
            pl.BlockSpec((1, D_MODEL), lambda i: (0, 0)),
        ],
        out_specs=pl.BlockSpec((tm, D_MODEL), lambda i: (i, 0)),
        out_shape=jax.ShapeDtypeStruct((m, D_MODEL), F32),
        compiler_params=_params("parallel"),
        name="final_norm",
    )(x, g.reshape(1, D_MODEL))


def _mm_kernel(x_ref, w_ref, o_ref):
    o_ref[...] = _dot(x_ref[...], w_ref[...]).astype(o_ref.dtype)


def matmul(x, w, layer, *, tm, tn, out_dtype):
    m, k = x.shape
    n = w.shape[-1]
    return pl.pallas_call(
        _mm_kernel,
        grid=(m // tm, n // tn),
        in_specs=[
            pl.BlockSpec((tm, k), lambda i, j: (i, 0)),
            pl.BlockSpec((None, k, tn), lambda i, j: (layer, 0, j)),
        ],
        out_specs=pl.BlockSpec((tm, tn), lambda i, j: (i, j)),
        out_shape=jax.ShapeDtypeStruct((m, n), out_dtype),
        compiler_params=_params("parallel", "arbitrary"),
        name="matmul",
    )(x, w)


def _mm_res_kernel(x_ref, w_ref, r_ref, g_ref, o_ref):
    o_ref[...] = r_ref[...] + g_ref[...] * _dot(x_ref[...], w_ref[...])


def matmul_gated_residual(x, w, layer, res, gate, *, tm, tn):
    m, k = x.shape
    n = w.shape[-1]
    return pl.pallas_call(
        _mm_res_kernel,
        grid=(m // tm, n // tn),
        in_specs=[
            pl.BlockSpec((tm, k), lambda i, j: (i, 0)),
            pl.BlockSpec((None, k, tn), lambda i, j: (layer, 0, j)),
            pl.BlockSpec((tm, tn), lambda i, j: (i, j)),
            pl.BlockSpec((None, 1, tn), lambda i, j: (_cond_of_tile(i, tm), 0, j)),
        ],
        out_specs=pl.BlockSpec((tm, tn), lambda i, j: (i, j)),
        out_shape=jax.ShapeDtypeStruct((m, n), F32),
        compiler_params=_params("parallel", "arbitrary"),
        name="matmul_gated_residual",
    )(x, w, res, gate)


def _swiglu_kernel(x_ref, wg_ref, wu_ref, o_ref):
    x = x_ref[...]
    g = _dot(x, wg_ref[...])
    u = _dot(x, wu_ref[...])
    o_ref[...] = (_silu(g) * u).astype(o_ref.dtype)


def swiglu_up(x, w_gate, w_up, layer, *, tm, tn):
    m, k = x.shape
    n = w_gate.shape[-1]
    return pl.pallas_call(
        _swiglu_kernel,
        grid=(m // tm, n // tn),
        in_specs=[
            pl.BlockSpec((tm, k), lambda i, j: (i, 0)),
            pl.BlockSpec((None, k, tn), lambda i, j: (layer, 0, j)),
            pl.BlockSpec((None, k, tn), lambda i, j: (layer, 0, j)),
        ],
        out_specs=pl.BlockSpec((tm, tn), lambda i, j: (i, j)),
        out_shape=jax.ShapeDtypeStruct((m, n), BF16),
        compiler_params=_params("parallel", "arbitrary"),
        name="swiglu_up",
    )(x, w_gate, w_up)


def _rope(x, c, se, so):
    nxt = pltpu.roll(x, LANE - 1, 1)
    prv = pltpu.roll(x, 1, 1)
    return x * c + nxt * se + prv * so


def _mm_rope_kernel(x_ref, w_ref, c_ref, se_ref, so_ref, o_ref, *, first_rope_tile):
    acc = _dot(x_ref[...], w_ref[...])
    j = pl.program_id(1)

    @pl.when(j < first_rope_tile)
    def _():
        o_ref[...] = acc.astype(o_ref.dtype)

    @pl.when(j >= first_rope_tile)
    def _():
        c, se, so = c_ref[...], se_ref[...], so_ref[...]
        for h in range(acc.shape[1] // LANE):
            sl = slice(h * LANE, (h + 1) * LANE)
            o_ref[:, sl] = _rope(acc[:, sl], c, se, so).astype(o_ref.dtype)


def matmul_rope_tail(x, w, layer, tables, *, tm, tn, first_rope_tile):
    m, k = x.shape
    n = w.shape[-1]
    nt = DEC_SEQ // tm
    tab = pl.BlockSpec((tm, LANE), lambda i, j: (i % nt, 0))
    return pl.pallas_call(
        functools.partial(_mm_rope_kernel, first_rope_tile=first_rope_tile),
        grid=(m // tm, n // tn),
        in_specs=[
            pl.BlockSpec((tm, k), lambda i, j: (i, 0)),
            pl.BlockSpec((None, k, tn), lambda i, j: (layer, 0, j)),
            tab, tab, tab,
        ],
        out_specs=pl.BlockSpec((tm, tn), lambda i, j: (i, j)),
        out_shape=jax.ShapeDtypeStruct((m, n), BF16),
        compiler_params=_params("parallel", "arbitrary"),
        name="matmul_rope_tail",
    )(x, w, *tables)


def _head_rmsnorm(x, g):
    return x * lax.rsqrt(jnp.mean(x * x, axis=-1, keepdims=True) + EPS) * g


def _prep_kernel(*refs, rope):
    if rope:
        (qk_ref, v_ref, c_ref, kr_ref, qn_ref, kn_ref, mqn_ref, mkvn_ref,
         cb_ref, seb_ref, sob_ref, cc_ref, sec_ref, soc_ref,
         q_o, k_o, v_o, cq_o, ckv_o, kr_o) = refs
    else:
        (qk_ref, v_ref, c_ref, kr_ref, qn_ref, kn_ref, mqn_ref, mkvn_ref,
         q_o, k_o, v_o, cq_o, ckv_o, kr_o, kf_o, ckvf_o) = refs
    qn, kn = qn_ref[...], kn_ref[...]
    if rope:
        cb, seb, sob = cb_ref[...], seb_ref[...], sob_ref[...]
    for h in range(GQA_HEADS + GQA_KV_HEADS):
        sl = slice(h * LANE, (h + 1) * LANE)
        is_q = h < GQA_HEADS
        y = _head_rmsnorm(qk_ref[:, sl], qn if is_q else kn)
        osl = sl if is_q else slice((h - GQA_HEADS) * LANE, (h - GQA_HEADS + 1) * LANE)
        if not rope and not is_q:
            kf_o[:, osl] = y
        if rope:
            y = _rope(y, cb, seb, sob)
        (q_o if is_q else k_o)[:, osl] = y.astype(BF16)
    v_o[...] = v_ref[...].astype(BF16)
    cq = c_ref[:, :MLA_Q_RANK]
    cq_o[...] = _head_rmsnorm(cq, mqn_ref[...]).astype(BF16)
    ckv = _head_rmsnorm(c_ref[:, MLA_Q_RANK:], mkvn_ref[...])
    ckv_o[...] = ckv.astype(BF16)
    if not rope:
        ckvf_o[...] = ckv
    kr = kr_ref[...]
    if rope:
        kr = _rope(kr, cc_ref[...], sec_ref[...], soc_ref[...])
    kr_o[...] = kr.astype(BF16)


def prep(p, kr, gqa_qn, gqa_kn, mla_qn, mla_kvn, *, latent, tables_b=None, tables_c=None):
    tm = ROW_TILE
    n_rows = N_LAT if latent else N_CTX
    off = (N_CTX // tm) if latent else 0
    nt = DEC_SEQ // tm
    row = lambda c: (lambda i: (i + off, c))
    in_specs = [
        pl.BlockSpec((tm, 2048), row(1)),
        pl.BlockSpec((tm, KV_WIDTH), row(8)),
        pl.BlockSpec((tm, 1536), row(3)),
        pl.BlockSpec((tm, LANE), row(0)),
        pl.BlockSpec((1, HEAD_DIM), lambda i: (0, 0)),
        pl.BlockSpec((1, HEAD_DIM), lambda i: (0, 0)),
        pl.BlockSpec((1, MLA_Q_RANK), lambda i: (0, 0)),
        pl.BlockSpec((1, MLA_KV_RANK), lambda i: (0, 0)),
    ]
    args = [p, p, p, kr, gqa_qn.reshape(1, -1), gqa_kn.reshape(1, -1),
            mla_qn.reshape(1, -1), mla_kvn.reshape(1, -1)]
    out = lambda w, dt: (pl.BlockSpec((tm, w), lambda i: (i, 0)), jax.ShapeDtypeStruct((n_rows, w), dt))
    outs = [out(GQA_WIDTH, BF16), out(KV_WIDTH, BF16), out(KV_WIDTH, BF16),
            out(MLA_Q_RANK, BF16), out(MLA_KV_RANK, BF16), out(LANE, BF16)]
    if latent:
        tab = pl.BlockSpec((tm, LANE), lambda i: (i % nt, 0))
        in_specs += [tab] * 6
        args += list(tables_b) + list(tables_c)
    else:
        outs += [out(KV_WIDTH, F32), out(MLA_KV_RANK, F32)]
    return pl.pallas_call(
        functools.partial(_prep_kernel, rope=latent),
        grid=(n_rows // tm,),
        in_specs=in_specs,
        out_specs=[o[0] for o in outs],
        out_shape=[o[1] for o in outs],
        compiler_params=_params("parallel"),
        name="prep_latent" if latent else "prep_context",
    )(*args)


def _gelu_tanh(x):
    return 0.5 * x * (1.0 + jnp.tanh(0.7978845608028654 * (x + 0.044715 * (x * x * x))))


def _softplus(z):
    return jnp.maximum(z, 0.0) + jnp.log1p(jnp.exp(-jnp.abs(z)))


def _lru_kernel(xa_ref, ga_ref, cw_ref, cb_ref, w_ref, p_ref, h0_ref, o_ref, st_ref,
                a_scr, b_scr, h_scr):
    t_len, c = xa_ref.shape
    x = xa_ref[...]
    row = lax.broadcasted_iota(jnp.int32, x.shape, 0)
    cw = cw_ref[...]
    xm2 = jnp.where(row >= 2, pltpu.roll(x, 2, 0), 0.0)
    xm1 = jnp.where(row >= 1, pltpu.roll(x, 1, 0), 0.0)
    xp1 = jnp.where(row < t_len - 1, pltpu.roll(x, t_len - 1, 0), 0.0)
    xc = cb_ref[...] + xm2 * cw[0:1] + xm1 * cw[1:2] + x * cw[2:3] + xp1 * cw[3:4]
    xc16 = xc.astype(BF16)
    n_tiles = t_len // 8

    for d in range(2):
        g = _dot(xc16, w_ref[d])
        prm = p_ref[d]
        r = jax.nn.sigmoid(g[:, :c] + prm[0:1])
        ig = jax.nn.sigmoid(g[:, c:] + prm[1:2])
        log_a = (-LRU_C) * r * _softplus(-prm[2:3])
        a = jnp.exp(log_a)
        b = jnp.sqrt(-jnp.tanh(log_a) * (a * a + 1.0)) * (ig * xc)
        a_scr[...] = a
        b_scr[...] = b
        reverse = d == 1

        def body(j, h, reverse=reverse, first=(d == 0)):
            jj = (n_tiles - 1 - j) if reverse else j
            base = pl.multiple_of(jj * 8, 8)
            for r8 in (range(7, -1, -1) if reverse else range(8)):
                idx = pl.ds(base + r8, 1)
                h = a_scr[idx, :] * h + b_scr[idx, :]
                if first:
                    h_scr[idx, :] = h
                else:
                    h_scr[idx, :] = h_scr[idx, :] + h
            return h

        h_last = lax.fori_loop(0, n_tiles, body, h0_ref[d:d + 1, :])
        st_ref[d:d + 1, :] = h_last

    o_ref[...] = (h_scr[...] * _gelu_tanh(ga_ref[...])).astype(o_ref.dtype)


def lru_mixer(p, conv_w, conv_b, w_gates, lru_prm, h0, layer, *, latent):
    c = LRU_CHUNK
    t_len = DEC_SEQ if latent else SEQ
    n_seq = DEC_BATCH if latent else BATCH
    off = (N_CTX // t_len) if latent else 0
    nck = LRU_WIDTH // c
    return pl.pallas_call(
        _lru_kernel,
        grid=(n_seq, nck),
        in_specs=[
            pl.BlockSpec((t_len, c), lambda s, k: (s + off, k)),
            pl.BlockSpec((t_len, c), lambda s, k: (s + off, nck + k)),
            pl.BlockSpec((None, 4, c), lambda s, k: (layer, 0, k)),
            pl.BlockSpec((None, 1, c), lambda s, k: (layer, 0, k)),
            pl.BlockSpec((None, 2, None, c, 2 * c), lambda s, k: (layer, 0, k, 0, 0)),
            pl.BlockSpec((None, 2, 3, c), lambda s, k: (layer, 0, 0, k)),
            pl.BlockSpec((None, 2, c), lambda s, k: (s, 0, k)),
        ],
        out_specs=[
            pl.BlockSpec((t_len, c), lambda s, k: (s, k)),
            pl.BlockSpec((None, 2, c), lambda s, k: (s, 0, k)),
        ],
        out_shape=[
            jax.ShapeDtypeStruct((n_seq * t_len, LRU_WIDTH), BF16),
            jax.ShapeDtypeStruct((n_seq, 2, LRU_WIDTH), F32),
        ],
        scratch_shapes=[pltpu.VMEM((t_len, c), F32)] * 3,
        compiler_params=_params("parallel", "parallel"),
        name="lru_latent" if latent else "lru_context",
    )(p, p, conv_w, conv_b.reshape(DEPTH, 1, LRU_WIDTH), w_gates, lru_prm, h0)


def _softmax_pv(s, v):
    m = jnp.max(s, axis=-1, keepdims=True)
    e = jnp.exp(s - m)
    l = jnp.sum(e, axis=-1, keepdims=True)
    return _dot(e.astype(BF16), v) / l


def _gqa_kernel(q_ref, k_ref, v_ref, o_ref, *, scale):
    tq = q_ref.shape[0]
    qs = jnp.concatenate([q_ref[:, r * LANE:(r + 1) * LANE] for r in range(GQA_REP)], axis=0)
    s = _dot_nt(qs, k_ref[...]) * scale
    o = _softmax_pv(s, v_ref[...])
    for r in range(GQA_REP):
        o_ref[:, r * LANE:(r + 1) * LANE] = o[r * tq:(r + 1) * tq].astype(o_ref.dtype)


def gqa_attention(q, k, v, *, n_batch, t_q, t_k, tq):
    nq = t_q // tq
    gw = GQA_REP * HEAD_DIM
    return pl.pallas_call(
        functools.partial(_gqa_kernel, scale=HEAD_DIM ** -0.5),
        grid=(n_batch, GQA_KV_HEADS, nq),
        in_specs=[
            pl.BlockSpec((tq, gw), lambda b, g, i: (b * nq + i, g)),
            pl.BlockSpec((None, t_k, HEAD_DIM), lambda b, g, i: (b, 0, g)),
            pl.BlockSpec((None, t_k, HEAD_DIM), lambda b, g, i: (b, 0, g)),
        ],
        out_specs=pl.BlockSpec((tq, gw), lambda b, g, i: (b * nq + i, g)),
        out_shape=jax.ShapeDtypeStruct((n_batch * t_q, GQA_WIDTH), BF16),
        compiler_params=_params("parallel", "parallel", "arbitrary"),
        name="gqa_attention",
    )(q, k, v)


def _mla_kernel(qn_ref, qr_ref, kv_ref, kr_ref, o_ref, *, scale):
    s = (_dot_nt(qn_ref[...], kv_ref[:, :MLA_NOPE]) + _dot_nt(qr_ref[...], kr_ref[...])) * scale
    o_ref[...] = _softmax_pv(s, kv_ref[:, MLA_NOPE:]).astype(o_ref.dtype)


def mla_attention(q, kv, kr, *, n_batch, t_q, t_k, tq):
    nq = t_q // tq
    return pl.pallas_call(
        functools.partial(_mla_kernel, scale=(MLA_NOPE + MLA_ROPE) ** -0.5),
        grid=(n_batch, MLA_HEADS, nq),
        in_specs=[
            pl.BlockSpec((tq, MLA_NOPE), lambda b, h, i: (b * nq + i, h)),
            pl.BlockSpec((tq, LANE), lambda b, h, i: (b * nq + i, MLA_HEADS + h)),
            pl.BlockSpec((None, t_k, MLA_NOPE + MLA_V), lambda b, h, i: (b, 0, h)),
            pl.BlockSpec((None, t_k, LANE), lambda b, h, i: (b, 0, 0)),
        ],
        out_specs=pl.BlockSpec((tq, MLA_V), lambda b, h, i: (b * nq + i, h)),
        out_shape=jax.ShapeDtypeStruct((n_batch * t_q, MLA_WIDTH), BF16),
        compiler_params=_params("parallel", "parallel", "arbitrary"),
        name="mla_attention",
    )(q, q, kv, kr)


def _rope_tables(rot_dim):
    rows = DEC_SEQ // GRID_W
    row = jnp.repeat(jnp.arange(rows, dtype=F32), GRID_W)
    col = jnp.tile(jnp.arange(GRID_W, dtype=F32), rows)
    quarter = rot_dim // 4
    freqs = ROPE_THETA ** (-jnp.arange(quarter, dtype=F32) / quarter)
    ang = jnp.concatenate([row[:, None] * freqs, col[:, None] * freqs], axis=-1)
    cos, sin = jnp.cos(ang), jnp.sin(ang)
    zero = jnp.zeros_like(sin)
    c = jnp.repeat(cos, 2, axis=-1)
    se = jnp.stack([-sin, zero], axis=-1).reshape(DEC_SEQ, rot_dim)
    so = jnp.stack([zero, sin], axis=-1).reshape(DEC_SEQ, rot_dim)
    pad = ((0, 0), (0, LANE - rot_dim))
    return tuple(jnp.pad(t, pad) for t in (c, se, so))


def _block_diag_gates(lru_wr, lru_wi):
    per = LRU_CHUNK // LRU_BLOCK
    nck = LRU_WIDTH // LRU_CHUNK
    eye = jnp.eye(per, dtype=F32)

    def bd(w):
        w = w.reshape(DEPTH, 2, nck, per, LRU_BLOCK, LRU_BLOCK)
        w = jnp.einsum("dzcakj,ab->dzcakbj", w, eye)
        return w.reshape(DEPTH, 2, nck, LRU_CHUNK, LRU_CHUNK)

    return jnp.concatenate([bd(lru_wr), bd(lru_wi)], axis=-1).astype(BF16)


def _permute_w_uq(w_uq):
    w = w_uq.reshape(DEPTH, MLA_Q_RANK, MLA_HEADS, MLA_NOPE + MLA_ROPE)
    nope = w[..., :MLA_NOPE].reshape(DEPTH, MLA_Q_RANK, MLA_HEADS * MLA_NOPE)
    rope = jnp.pad(w[..., MLA_NOPE:], ((0, 0), (0, 0), (0, 0), (0, LANE - MLA_ROPE)))
    rope = rope.reshape(DEPTH, MLA_Q_RANK, MLA_HEADS * LANE)
    return jnp.concatenate([nope, rope], axis=-1).astype(BF16)


def kernel(x_prompt, x_sample, state_lru, cache_gqa_k, cache_gqa_v, cache_mla_ckv, cache_mla_krope, c,
           c_ctx, w_mod, b_mod, norm_mix_g, norm_ffn_g, w_in, conv_w, conv_b, lru_wr, lru_br, lru_wi, lru_bi,
           lru_lam, gqa_qn, gqa_kn, mla_qn, mla_kvn, w_uq, w_ukv, w_out, w_gate, w_up, w_down, norm_f):
    x = jnp.concatenate([x_prompt.reshape(N_CTX, D_MODEL), x_sample.reshape(N_LAT, D_MODEL)], axis=0)

    cond8 = jnp.zeros((8, D_MODEL), F32).at[0].set(c_ctx).at[1:1 + DEC_BATCH].set(c)
    mod = modulation_all(cond8, w_mod, b_mod).reshape(DEPTH, 8, 6, 1, D_MODEL)

    w_in16 = w_in[:, :, :IN_MAIN].astype(BF16)
    w_kr16 = jnp.pad(w_in[:, :, IN_MAIN:], ((0, 0), (0, 0), (0, LANE - MLA_ROPE))).astype(BF16)
    w_out16 = w_out.astype(BF16)
    w_gate16 = w_gate.astype(BF16)
    w_up16 = w_up.astype(BF16)
    w_down16 = w_down.astype(BF16)
    w_uq16 = _permute_w_uq(w_uq)
    w_ukv16 = w_ukv.astype(BF16)
    w_gates = _block_diag_gates(lru_wr, lru_wi)
    lru_prm = jnp.stack([lru_br, lru_bi, lru_lam], axis=2)
    tables_b = _rope_tables(HEAD_DIM)
    tables_c = _rope_tables(MLA_ROPE)
    h0_ctx = jnp.zeros((BATCH, 2, LRU_WIDTH), F32)
    pad_kr = ((0, 0), (0, 0), (0, LANE - MLA_ROPE))

    st_lru, st_k, st_v, st_ckv, st_kr = [], [], [], [], []
    for l in range(DEPTH):
        sh1, sc1, g1, sh2, sc2, g2 = (mod[l, :, s] for s in range(6))

        h = norm_mod(x, norm_mix_g[l], sc1, sh1)
        p = matmul(h, w_in16, l, tm=1024, tn=1024, out_dtype=F32)
        kr = matmul(h, w_kr16, l, tm=1024, tn=LANE, out_dtype=F32)

        a_ctx, s_lru = lru_mixer(p, conv_w, conv_b, w_gates, lru_prm, h0_ctx, l, latent=False)
        a_lat, _ = lru_mixer(p, conv_w, conv_b, w_gates, lru_prm, state_lru[:, l], l, latent=True)

        (q_c, k_c, v_c, cq_c, ckv_c, kr_c, kf_c, ckvf_c) = prep(
            p, kr, gqa_qn[l], gqa_kn[l], mla_qn[l], mla_kvn[l], latent=False)
        (q_l, k_l, v_l, cq_l, ckv_l, kr_l) = prep(
            p, kr, gqa_qn[l], gqa_kn[l], mla_qn[l], mla_kvn[l], latent=True,
            tables_b=tables_b, tables_c=tables_c)

        b_ctx = gqa_attention(q_c, k_c.reshape(BATCH, SEQ, KV_WIDTH), v_c.reshape(BATCH, SEQ, KV_WIDTH),
                              n_batch=BATCH, t_q=SEQ, t_k=SEQ, tq=SEQ)
        k_all = jnp.concatenate([cache_gqa_k[:, l].reshape(DEC_BATCH, PAST_LEN, KV_WIDTH).astype(BF16),
                                 k_l.reshape(DEC_BATCH, DEC_SEQ, KV_WIDTH)], axis=1)
        v_all = jnp.concatenate([cache_gqa_v[:, l].reshape(DEC_BATCH, PAST_LEN, KV_WIDTH).astype(BF16),
                                 v_l.reshape(DEC_BATCH, DEC_SEQ, KV_WIDTH)], axis=1)
        b_lat = gqa_attention(q_l, k_all, v_all, n_batch=DEC_BATCH, t_q=DEC_SEQ, t_k=LAT_KEYS, tq=256)

        qm_c = matmul(cq_c, w_uq16, l, tm=1024, tn=512, out_dtype=BF16)
        qm_l = matmul_rope_tail(cq_l, w_uq16, l, tables_c, tm=1024, tn=512, first_rope_tile=3)
        kv_c = matmul(ckv_c, w_ukv16, l, tm=1024, tn=1024, out_dtype=BF16)
        ckv_all = jnp.concatenate([cache_mla_ckv[:, l].astype(BF16),
                                   ckv_l.reshape(DEC_BATCH, DEC_SEQ, MLA_KV_RANK)], axis=1)
        kv_l = matmul(ckv_all.reshape(DEC_BATCH * LAT_KEYS, MLA_KV_RANK), w_ukv16, l,
                      tm=1024, tn=1024, out_dtype=BF16)
        kr_all = jnp.concatenate([jnp.pad(cache_mla_krope[:, l], pad_kr).astype(BF16),
                                  kr_l.reshape(DEC_BATCH, DEC_SEQ, LANE)], axis=1)
        c_ctx_out = mla_attention(qm_c, kv_c.reshape(BATCH, SEQ, -1), kr_c.reshape(BATCH, SEQ, LANE),
                                  n_batch=BATCH, t_q=SEQ, t_k=SEQ, tq=SEQ)
        c_lat = mla_attention(qm_l, kv_l.reshape(DEC_BATCH, LAT_KEYS, -1), kr_all,
                              n_batch=DEC_BATCH, t_q=DEC_SEQ, t_k=LAT_KEYS, tq=512)

        mix = jnp.concatenate([jnp.concatenate([a_ctx, b_ctx, c_ctx_out], axis=1),
                               jnp.concatenate([a_lat, b_lat, c_lat], axis=1)], axis=0)
        x = matmul_gated_residual(mix, w_out16, l, x, g1, tm=1024, tn=512)

        h = norm_mod(x, norm_ffn_g[l], sc2, sh2)
        ff = swiglu_up(h, w_gate16, w_up16, l, tm=1024, tn=256)
        x = matmul_gated_residual(ff, w_down16, l, x, g2, tm=512, tn=256)

        st_lru.append(s_lru)
        st_k.append(kf_c.reshape(BATCH, SEQ, GQA_KV_HEADS, HEAD_DIM))
        st_v.append(p[:N_CTX, 4096:4096 + KV_WIDTH].reshape(BATCH, SEQ, GQA_KV_HEADS, HEAD_DIM))
        st_ckv.append(ckvf_c.reshape(BATCH, SEQ, MLA_KV_RANK))
        st_kr.append(kr[:N_CTX, :MLA_ROPE].reshape(BATCH, SEQ, MLA_ROPE))

    y = final_norm(x, norm_f)
    return (y[:N_CTX].reshape(BATCH, SEQ, D_MODEL), y[N_CTX:].reshape(DEC_BATCH, DEC_SEQ, D_MODEL),
            jnp.stack(st_lru, axis=1), jnp.stack(st_k, axis=1), jnp.stack(st_v, axis=1),
            jnp.stack(st_ckv, axis=1), jnp.stack(st_kr, axis=1))
```

```python
import functools

import jax
import jax.numpy as jnp
from jax import lax
from jax.experimental import pallas as pl
from jax.experimental.pallas import tpu as pltpu

F32 = jnp.float32
BF16 = jnp.bfloat16

D_MODEL = 4096
BATCH = 16
SEQ = 256
DEPTH = 4
DEC_BATCH = 2
DEC_SEQ = 2048
PAST_LEN = 512
GRID_W = 64
EPS = 1e-6
ROPE_THETA = 10000.0

LRU_WIDTH = 1024
LRU_BLOCKS = 16
LRU_BLOCK = 64
LRU_C = 8.0
HEAD_DIM = 128
GQA_HEADS = 12
GQA_KV_HEADS = 4
GQA_REP = GQA_HEADS // GQA_KV_HEADS
GQA_WIDTH = GQA_HEADS * HEAD_DIM
KV_WIDTH = GQA_KV_HEADS * HEAD_DIM
MLA_HEADS = 12
MLA_Q_RANK = 1024
MLA_KV_RANK = 512
MLA_NOPE = 128
MLA_ROPE = 64
MLA_V = 128
MLA_WIDTH = MLA_HEADS * MLA_V
D_FF = 11008
IN_MAIN = 6144
LANE = 128
N_CTX = BATCH * SEQ
N_LAT = DEC_BATCH * DEC_SEQ
N_TOK = N_CTX + N_LAT
LAT_KEYS = PAST_LEN + DEC_SEQ
VMEM_LIMIT = 56 * 1024 * 1024
LRU_CHUNK = 256
LRU_ROWS = 2048
ROW_TILE = 256


def _params(*sem):
    return pltpu.CompilerParams(dimension_semantics=sem, vmem_limit_bytes=VMEM_LIMIT)


def _cond_of_tile(i, tm):
    row = i * tm
    return jnp.where(row < N_CTX, 0, 1 + (row - N_CTX) // DEC_SEQ)


def _silu(x):
    return x * jax.nn.sigmoid(x)


def _dot(a, b):
    return jnp.dot(a, b, preferred_element_type=F32)


def _dot_nt(a, b):
    return lax.dot_general(a, b, (((1,), (1,)), ((), ())), preferred_element_type=F32)


def _mod_kernel(c_ref, w_ref, b_ref, o_ref):
    s = _silu(c_ref[...]).astype(BF16)
    o_ref[...] = _dot(s, w_ref[...].astype(BF16)) + b_ref[...]


def modulation_all(cond8, w_mod, b_mod):
    tn = 512
    n = 6 * D_MODEL
    return pl.pallas_call(
        _mod_kernel,
        grid=(DEPTH, n // tn),
        in_specs=[
            pl.BlockSpec((8, D_MODEL), lambda l, j: (0, 0)),
            pl.BlockSpec((None, D_MODEL, tn), lambda l, j: (l, 0, j)),
            pl.BlockSpec((None, 1, tn), lambda l, j: (l, 0, j)),
        ],
        out_specs=pl.BlockSpec((None, 8, tn), lambda l, j: (l, 0, j)),
        out_shape=jax.ShapeDtypeStruct((DEPTH, 8, n), F32),
        compiler_params=_params("parallel", "parallel"),
        name="modulation",
    )(cond8, w_mod, b_mod.reshape(DEPTH, 1, n))


def _norm_mod_kernel(x_ref, g_ref, sc_ref, sh_ref, o_ref):
    x = x_ref[...]
    y = x * lax.rsqrt(jnp.mean(x * x, axis=-1, keepdims=True) + EPS) * g_ref[...]
    o_ref[...] = (y * (1.0 + sc_ref[...]) + sh_ref[...]).astype(o_ref.dtype)


def norm_mod(x, g, sc, sh):
    tm = ROW_TILE
    m = x.shape[0]
    return pl.pallas_call(
        _norm_mod_kernel,
        grid=(m // tm,),
        in_specs=[
            pl.BlockSpec((tm, D_MODEL), lambda i: (i, 0)),
            pl.BlockSpec((1, D_MODEL), lambda i: (0, 0)),
            pl.BlockSpec((None, 1, D_MODEL), lambda i: (_cond_of_tile(i, tm), 0, 0)),
            pl.BlockSpec((None, 1, D_MODEL), lambda i: (_cond_of_tile(i, tm), 0, 0)),
        ],
        out_specs=pl.BlockSpec((tm, D_MODEL), lambda i: (i, 0)),
        out_shape=jax.ShapeDtypeStruct((m, D_MODEL), BF16),
        compiler_params=_params("parallel"),
        name="norm_mod",
    )(x, g.reshape(1, D_MODEL), sc, sh)


def _final_norm_kernel(x_ref, g_ref, o_ref):
    x = x_ref[...]
    o_ref[...] = x * lax.rsqrt(jnp.mean(x * x, axis=-1, keepdims=True) + EPS) * g_ref[...]


def final_norm(x, g):
    tm = ROW_TILE
    m = x.shape[0]
    return pl.pallas_call(
        _final_norm_kernel,
        grid=(m // tm,),
        in_specs=[
            pl.BlockSpec((tm, D_MODEL), lambda i: (i, 0)),
            pl.BlockSpec((1, D_MODEL), lambda i: (0, 0)),
        ],
        out_specs=pl.BlockSpec((tm, D_MODEL), lambda i: (i, 0)),
        out_shape=jax.ShapeDtypeStruct((m, D_MODEL), F32),
        compiler_params=_params("parallel"),
        name="final_norm",
    )(x, g.reshape(1, D_MODEL))


def _w16(w_ref):
    return w_ref[...].astype(BF16)


def _mm_kernel(x_ref, w_ref, o_ref):
    o_ref[...] = _dot(x_ref[...], _w16(w_ref)).astype(o_ref.dtype)


def matmul(x, w, layer, *, tm, tn, out_dtype, n=None):
    m, k = x.shape
    n = w.shape[-1] if n is None else n
    return pl.pallas_call(
        _mm_kernel,
        grid=(m // tm, n // tn),
        in_specs=[
            pl.BlockSpec((tm, k), lambda i, j: (i, 0)),
            pl.BlockSpec((None, k, tn), lambda i, j: (layer, 0, j)),
        ],
        out_specs=pl.BlockSpec((tm, tn), lambda i, j: (i, j)),
        out_shape=jax.ShapeDtypeStruct((m, n), out_dtype),
        compiler_params=_params("parallel", "arbitrary"),
        name="matmul",
    )(x, w)


def _mm_res_kernel(x_ref, w_ref, r_ref, g_ref, o_ref):
    o_ref[...] = r_ref[...] + g_ref[...] * _dot(x_ref[...], _w16(w_ref))


def matmul_gated_residual(x, w, layer, res, gate, *, tm, tn):
    m, k = x.shape
    n = w.shape[-1]
    return pl.pallas_call(
        _mm_res_kernel,
        grid=(m // tm, n // tn),
        in_specs=[
            pl.BlockSpec((tm, k), lambda i, j: (i, 0)),
            pl.BlockSpec((None, k, tn), lambda i, j: (layer, 0, j)),
            pl.BlockSpec((tm, tn), lambda i, j: (i, j)),
            pl.BlockSpec((None, 1, tn), lambda i, j: (_cond_of_tile(i, tm), 0, j)),
        ],
        out_specs=pl.BlockSpec((tm, tn), lambda i, j: (i, j)),
        out_shape=jax.ShapeDtypeStruct((m, n), F32),
        compiler_params=_params("parallel", "arbitrary"),
        name="matmul_gated_residual",
    )(x, w, res, gate)


def _swiglu_kernel(x_ref, wg_ref, wu_ref, o_ref):
    x = x_ref[...]
    g = _dot(x, _w16(wg_ref))
    u = _dot(x, _w16(wu_ref))
    o_ref[...] = (_silu(g) * u).astype(o_ref.dtype)


def swiglu_up(x, w_gate, w_up, layer, *, tm, tn):
    m, k = x.shape
    n = w_gate.shape[-1]
    return pl.pallas_call(
        _swiglu_kernel,
        grid=(m // tm, n // tn),
        in_specs=[
            pl.BlockSpec((tm, k), lambda i, j: (i, 0)),
            pl.BlockSpec((None, k, tn), lambda i, j: (layer, 0, j)),
            pl.BlockSpec((None, k, tn), lambda i, j: (layer, 0, j)),
        ],
        out_specs=pl.BlockSpec((tm, tn), lambda i, j: (i, j)),
        out_shape=jax.ShapeDtypeStruct((m, n), BF16),
        compiler_params=_params("parallel", "arbitrary"),
        name="swiglu_up",
    )(x, w_gate, w_up)


def _rope(x, c, se, so):
    nxt = pltpu.roll(x, LANE - 1, 1)
    prv = pltpu.roll(x, 1, 1)
    return x * c + nxt * se + prv * so


def _mm_rope_kernel(x_ref, w_ref, c_ref, se_ref, so_ref, o_ref, *, first_rope_tile):
    acc = _dot(x_ref[...], _w16(w_ref))
    j = pl.program_id(1)

    @pl.when(j < first_rope_tile)
    def _():
        o_ref[...] = acc.astype(o_ref.dtype)

    @pl.when(j >= first_rope_tile)
    def _():
        c, se, so = c_ref[...], se_ref[...], so_ref[...]
        for h in range(acc.shape[1] // LANE):
            sl = slice(h * LANE, (h + 1) * LANE)
            o_ref[:, sl] = _rope(acc[:, sl], c, se, so).astype(o_ref.dtype)


def matmul_rope_tail(x, w, layer, tables, *, tm, tn, first_rope_tile):
    m, k = x.shape
    n = w.shape[-1]
    nt = DEC_SEQ // tm
    tab = pl.BlockSpec((tm, LANE), lambda i, j: (i % nt, 0))
    return pl.pallas_call(
        functools.partial(_mm_rope_kernel, first_rope_tile=first_rope_tile),
        grid=(m // tm, n // tn),
        in_specs=[
            pl.BlockSpec((tm, k), lambda i, j: (i, 0)),
            pl.BlockSpec((None, k, tn), lambda i, j: (layer, 0, j)),
            tab, tab, tab,
        ],
        out_specs=pl.BlockSpec((tm, tn), lambda i, j: (i, j)),
        out_shape=jax.ShapeDtypeStruct((m, n), BF16),
        compiler_params=_params("parallel", "arbitrary"),
        name="matmul_rope_tail",
    )(x, w, *tables)


def _head_rmsnorm(x, g):
    return x * lax.rsqrt(jnp.mean(x * x, axis=-1, keepdims=True) + EPS) * g


def _prep_kernel(*refs, rope):
    if rope:
        (qk_ref, v_ref, c_ref, kr_ref, qn_ref, kn_ref, mqn_ref, mkvn_ref,
         cb_ref, seb_ref, sob_ref, cc_ref, sec_ref, soc_ref,
         q_o, k_o, v_o, cq_o, ckv_o, kr_o) = refs
    else:
        (qk_ref, v_ref, c_ref, kr_ref, qn_ref, kn_ref, mqn_ref, mkvn_ref,
         q_o, k_o, v_o, cq_o, ckv_o, kr_o, kf_o, ckvf_o) = refs
    qn, kn = qn_ref[...], kn_ref[...]
    if rope:
        cb, seb, sob = cb_ref[...], seb_ref[...], sob_ref[...]
    for h in range(GQA_HEADS + GQA_KV_HEADS):
        sl = slice(h * LANE, (h + 1) * LANE)
        is_q = h < GQA_HEADS
        y = _head_rmsnorm(qk_ref[:, sl], qn if is_q else kn)
        osl = sl if is_q else slice((h - GQA_HEADS) * LANE, (h - GQA_HEADS + 1) * LANE)
        if not rope and not is_q:
            kf_o[:, osl] = y
        if rope:
            y = _rope(y, cb, seb, sob)
        (q_o if is_q else k_o)[:, osl] = y.astype(BF16)
    v_o[...] = v_ref[...].astype(BF16)
    cq = c_ref[:, :MLA_Q_RANK]
    cq_o[...] = _head_rmsnorm(cq, mqn_ref[...]).astype(BF16)
    ckv = _head_rmsnorm(c_ref[:, MLA_Q_RANK:], mkvn_ref[...])
    ckv_o[...] = ckv.astype(BF16)
    if not rope:
        ckvf_o[...] = ckv
    kr = kr_ref[...]
    if rope:
        kr = _rope(kr, cc_ref[...], sec_ref[...], soc_ref[...])
    kr_o[...] = kr.astype(BF16)


def prep(p, kr, gqa_qn, gqa_kn, mla_qn, mla_kvn, *, latent, tables_b=None, tables_c=None):
    tm = ROW_TILE
    n_rows = N_LAT if latent else N_CTX
    off = (N_CTX // tm) if latent else 0
    nt = DEC_SEQ // tm
    row = lambda c: (lambda i: (i + off, c))
    in_specs = [
        pl.BlockSpec((tm, 2048), row(1)),
        pl.BlockSpec((tm, KV_WIDTH), row(8)),
        pl.BlockSpec((tm, 1536), row(3)),
        pl.BlockSpec((tm, LANE), row(0)),
        pl.BlockSpec((1, HEAD_DIM), lambda i: (0, 0)),
        pl.BlockSpec((1, HEAD_DIM), lambda i: (0, 0)),
        pl.BlockSpec((1, MLA_Q_RANK), lambda i: (0, 0)),
        pl.BlockSpec((1, MLA_KV_RANK), lambda i: (0, 0)),
    ]
    args = [p, p, p, kr, gqa_qn.reshape(1, -1), gqa_kn.reshape(1, -1),
            mla_qn.reshape(1, -1), mla_kvn.reshape(1, -1)]
    out = lambda w, dt: (pl.BlockSpec((tm, w), lambda i: (i, 0)), jax.ShapeDtypeStruct((n_rows, w), dt))
    outs = [out(GQA_WIDTH, BF16), out(KV_WIDTH, BF16), out(KV_WIDTH, BF16),
            out(MLA_Q_RANK, BF16), out(MLA_KV_RANK, BF16), out(LANE, BF16)]
    if latent:
        tab = pl.BlockSpec((tm, LANE), lambda i: (i % nt, 0))
        in_specs += [tab] * 6
        args += list(tables_b) + list(tables_c)
    else:
        outs += [out(KV_WIDTH, F32), out(MLA_KV_RANK, F32)]
    return pl.pallas_call(
        functools.partial(_prep_kernel, rope=latent),
        grid=(n_rows // tm,),
        in_specs=in_specs,
        out_specs=[o[0] for o in outs],
        out_shape=[o[1] for o in outs],
        compiler_params=_params("parallel"),
        name="prep_latent" if latent else "prep_context",
    )(*args)


def _gelu_tanh(x):
    return 0.5 * x * (1.0 + jnp.tanh(0.7978845608028654 * (x + 0.044715 * (x * x * x))))


def _softplus(z):
    return jnp.maximum(z, 0.0) + jnp.log1p(jnp.exp(-jnp.abs(z)))


def _lru_kernel(xa_ref, ga_ref, cw_ref, cb_ref, w_ref, p_ref, h0_ref, o_ref, st_ref,
                a_scr, b_scr, hf_scr, hb_scr, *, t_sub):
    rows, c = xa_ref.shape
    n_sub = rows // t_sub
    x = xa_ref[...]
    tpos = lax.broadcasted_iota(jnp.int32, x.shape, 0) % t_sub
    cw = cw_ref[...]
    xm2 = jnp.where(tpos >= 2, pltpu.roll(x, 2, 0), 0.0)
    xm1 = jnp.where(tpos >= 1, pltpu.roll(x, 1, 0), 0.0)
    xp1 = jnp.where(tpos < t_sub - 1, pltpu.roll(x, rows - 1, 0), 0.0)
    xc = cb_ref[...] + xm2 * cw[0:1] + xm1 * cw[1:2] + x * cw[2:3] + xp1 * cw[3:4]
    xc16 = xc.astype(BF16)

    for d in range(2):
        g = _dot(xc16, w_ref[d])
        prm = p_ref[d]
        r = jax.nn.sigmoid(g[:, :c] + prm[0:1])
        ig = jax.nn.sigmoid(g[:, c:] + prm[1:2])
        log_a = (-LRU_C) * r * _softplus(-prm[2:3])
        a = jnp.exp(log_a)
        a_scr[d] = a
        b_scr[d] = jnp.sqrt(-jnp.tanh(log_a) * (a * a + 1.0)) * (ig * xc)

    def body(j, hs):
        hs = list(hs)
        base = pl.multiple_of(j * 8, 8)
        for r8 in range(8):
            for q in range(n_sub):
                idx_f = pl.ds(q * t_sub + base + r8, 1)
                idx_b = pl.ds(q * t_sub + (t_sub - 1 - r8) - base, 1)
                hf = a_scr[0, idx_f, :] * hs[2 * q] + b_scr[0, idx_f, :]
                hb = a_scr[1, idx_b, :] * hs[2 * q + 1] + b_scr[1, idx_b, :]
                hf_scr[idx_f, :] = hf
                hb_scr[idx_b, :] = hb
                hs[2 * q], hs[2 * q + 1] = hf, hb
        return tuple(hs)

    init = tuple(h0_ref[q, d:d + 1, :] for q in range(n_sub) for d in range(2))
    hs = lax.fori_loop(0, t_sub // 8, body, init)
    for q in range(n_sub):
        st_ref[q, 0:1, :] = hs[2 * q]
        st_ref[q, 1:2, :] = hs[2 * q + 1]
    o_ref[...] = ((hf_scr[...] + hb_scr[...]) * _gelu_tanh(ga_ref[...])).astype(o_ref.dtype)


def lru_mixer(p, conv_w, conv_b, w_gates, lru_prm, h0, layer, *, latent):
    c = LRU_CHUNK
    rows = LRU_ROWS
    t_sub = DEC_SEQ if latent else SEQ
    n_sub = rows // t_sub
    n_rows = N_LAT if latent else N_CTX
    off = (N_CTX // rows) if latent else 0
    nck = LRU_WIDTH // c
    return pl.pallas_call(
        functools.partial(_lru_kernel, t_sub=t_sub),
        grid=(n_rows // rows, nck),
        in_specs=[
            pl.BlockSpec((rows, c), lambda s, k: (s + off, k)),
            pl.BlockSpec((rows, c), lambda s, k: (s + off, nck + k)),
            pl.BlockSpec((None, 4, c), lambda s, k: (layer, 0, k)),
            pl.BlockSpec((None, 1, c), lambda s, k: (layer, 0, k)),
            pl.BlockSpec((None, 2, None, c, 2 * c), lambda s, k: (layer, 0, k, 0, 0)),
            pl.BlockSpec((None, 2, 3, c), lambda s, k: (layer, 0, 0, k)),
            pl.BlockSpec((n_sub, 2, c), lambda s, k: (s, 0, k)),
        ],
        out_specs=[
            pl.BlockSpec((rows, c), lambda s, k: (s, k)),
            pl.BlockSpec((n_sub, 2, c), lambda s, k: (s, 0, k)),
        ],
        out_shape=[
            jax.ShapeDtypeStruct((n_rows, LRU_WIDTH), BF16),
            jax.ShapeDtypeStruct((n_rows // t_sub, 2, LRU_WIDTH), F32),
        ],
        scratch_shapes=[pltpu.VMEM((2, rows, c), F32)] * 2 + [pltpu.VMEM((rows, c), F32)] * 2,
        compiler_params=_params("parallel", "parallel"),
        name="lru_latent" if latent else "lru_context",
    )(p, p, conv_w, conv_b.reshape(DEPTH, 1, LRU_WIDTH), w_gates, lru_prm, h0)


def _softmax_pv(s, v):
    m = jnp.max(s, axis=-1, keepdims=True)
    p = jnp.exp(s - m).astype(BF16)
    oa = _dot(p, jnp.concatenate([v, jnp.ones_like(v)], axis=1))
    return oa[:, :LANE] / oa[:, LANE:]


def _gqa_kernel(q_ref, k_ref, v_ref, o_ref, *, scale, groups, sub):
    tq = q_ref.shape[0]
    for g in range(groups):
        kv_sl = slice(g * LANE, (g + 1) * LANE)
        k, v = k_ref[:, kv_sl], v_ref[:, kv_sl]
        for h in range(g * GQA_REP, (g + 1) * GQA_REP):
            sl = slice(h * LANE, (h + 1) * LANE)
            for r0 in range(0, tq, sub):
                s = _dot_nt(q_ref[r0:r0 + sub, sl], k) * scale
                o_ref[r0:r0 + sub, sl] = _softmax_pv(s, v).astype(o_ref.dtype)


def gqa_attention(q, k, v, *, n_batch, t_q, t_k, tq, groups, sub):
    nq = t_q // tq
    qw = groups * GQA_REP * HEAD_DIM
    kw = groups * HEAD_DIM
    return pl.pallas_call(
        functools.partial(_gqa_kernel, scale=HEAD_DIM ** -0.5, groups=groups, sub=sub),
        grid=(n_batch, GQA_KV_HEADS // groups, nq),
        in_specs=[
            pl.BlockSpec((tq, qw), lambda b, g, i: (b * nq + i, g)),
            pl.BlockSpec((None, t_k, kw), lambda b, g, i: (b, 0, g)),
            pl.BlockSpec((None, t_k, kw), lambda b, g, i: (b, 0, g)),
        ],
        out_specs=pl.BlockSpec((tq, qw), lambda b, g, i: (b * nq + i, g)),
        out_shape=jax.ShapeDtypeStruct((n_batch * t_q, GQA_WIDTH), BF16),
        compiler_params=_params("parallel", "parallel", "arbitrary"),
        name="gqa_attention",
    )(q, k, v)


def _mla_kernel(qn_ref, qr_ref, kv_ref, kr_ref, o_ref, *, scale, heads, sub):
    tq = qn_ref.shape[0]
    kr = kr_ref[...]
    for h in range(heads):
        sl = slice(h * LANE, (h + 1) * LANE)
        k0 = h * (MLA_NOPE + MLA_V)
        k = jnp.concatenate([kv_ref[:, k0:k0 + MLA_NOPE], kr], axis=1)
        v = kv_ref[:, k0 + MLA_NOPE:k0 + MLA_NOPE + MLA_V]
        for r0 in range(0, tq, sub):
            q = jnp.concatenate([qn_ref[r0:r0 + sub, sl], qr_ref[r0:r0 + sub, sl]], axis=1)
            s = _dot_nt(q, k) * scale
            o_ref[r0:r0 + sub, sl] = _softmax_pv(s, v).astype(o_ref.dtype)


def mla_attention(q, kv, kr, *, n_batch, t_q, t_k, tq, heads, sub):
    nq = t_q // tq
    nh = MLA_HEADS // heads
    return pl.pallas_call(
        functools.partial(_mla_kernel, scale=(MLA_NOPE + MLA_ROPE) ** -0.5, heads=heads, sub=sub),
        grid=(n_batch, nh, nq),
        in_specs=[
            pl.BlockSpec((tq, heads * MLA_NOPE), lambda b, h, i: (b * nq + i, h)),
            pl.BlockSpec((tq, heads * LANE), lambda b, h, i: (b * nq + i, nh + h)),
            pl.BlockSpec((None, t_k, heads * (MLA_NOPE + MLA_V)), lambda b, h, i: (b, 0, h)),
            pl.BlockSpec((None, t_k, LANE), lambda b, h, i: (b, 0, 0)),
        ],
        out_specs=pl.BlockSpec((tq, heads * MLA_V), lambda b, h, i: (b * nq + i, h)),
        out_shape=jax.ShapeDtypeStruct((n_batch * t_q, MLA_WIDTH), BF16),
        compiler_params=_params("parallel", "parallel", "arbitrary"),
        name="mla_attention",
    )(q, q, kv, kr)


def _rope_tables(rot_dim):
    rows = DEC_SEQ // GRID_W
    row = jnp.repeat(jnp.arange(rows, dtype=F32), GRID_W)
    col = jnp.tile(jnp.arange(GRID_W, dtype=F32), rows)
    quarter = rot_dim // 4
    freqs = ROPE_THETA ** (-jnp.arange(quarter, dtype=F32) / quarter)
    ang = jnp.concatenate([row[:, None] * freqs, col[:, None] * freqs], axis=-1)
    cos, sin = jnp.cos(ang), jnp.sin(ang)
    zero = jnp.zeros_like(sin)
    c = jnp.repeat(cos, 2, axis=-1)
    se = jnp.stack([-sin, zero], axis=-1).reshape(DEC_SEQ, rot_dim)
    so = jnp.stack([zero, sin], axis=-1).reshape(DEC_SEQ, rot_dim)
    pad = ((0, 0), (0, LANE - rot_dim))
    return tuple(jnp.pad(t, pad) for t in (c, se, so))


def _block_diag_gates(lru_wr, lru_wi):
    per = LRU_CHUNK // LRU_BLOCK
    nck = LRU_WIDTH // LRU_CHUNK
    eye = jnp.eye(per, dtype=F32)

    def bd(w):
        w = w.reshape(DEPTH, 2, nck, per, LRU_BLOCK, LRU_BLOCK)
        w = jnp.einsum("dzcakj,ab->dzcakbj", w, eye)
        return w.reshape(DEPTH, 2, nck, LRU_CHUNK, LRU_CHUNK)

    return jnp.concatenate([bd(lru_wr), bd(lru_wi)], axis=-1).astype(BF16)


def _permute_w_uq(w_uq):
    w = w_uq.reshape(DEPTH, MLA_Q_RANK, MLA_HEADS, MLA_NOPE + MLA_ROPE)
    nope = w[..., :MLA_NOPE].reshape(DEPTH, MLA_Q_RANK, MLA_HEADS * MLA_NOPE)
    rope = jnp.pad(w[..., MLA_NOPE:], ((0, 0), (0, 0), (0, 0), (0, LANE - MLA_ROPE)))
    rope = rope.reshape(DEPTH, MLA_Q_RANK, MLA_HEADS * LANE)
    return jnp.concatenate([nope, rope], axis=-1).astype(BF16)


def kernel(x_prompt, x_sample, state_lru, cache_gqa_k, cache_gqa_v, cache_mla_ckv, cache_mla_krope, c,
           c_ctx, w_mod, b_mod, norm_mix_g, norm_ffn_g, w_in, conv_w, conv_b, lru_wr, lru_br, lru_wi, lru_bi,
           lru_lam, gqa_qn, gqa_kn, mla_qn, mla_kvn, w_uq, w_ukv, w_out, w_gate, w_up, w_down, norm_f):
    x = jnp.concatenate([x_prompt.reshape(N_CTX, D_MODEL), x_sample.reshape(N_LAT, D_MODEL)], axis=0)

    cond8 = jnp.zeros((8, D_MODEL), F32).at[0].set(c_ctx).at[1:1 + DEC_BATCH].set(c)
    mod = modulation_all(cond8, w_mod, b_mod).reshape(DEPTH, 8, 6, 1, D_MODEL)

    w_kr16 = jnp.pad(w_in[:, :, IN_MAIN:], ((0, 0), (0, 0), (0, LANE - MLA_ROPE))).astype(BF16)
    w_down16 = w_down.astype(BF16)
    w_uq16 = _permute_w_uq(w_uq)
    w_gates = _block_diag_gates(lru_wr, lru_wi)
    lru_prm = jnp.stack([lru_br, lru_bi, lru_lam], axis=2)
    tables_b = _rope_tables(HEAD_DIM)
    tables_c = _rope_tables(MLA_ROPE)
    h0_ctx = jnp.zeros((BATCH, 2, LRU_WIDTH), F32)
    pad_kr = ((0, 0), (0, 0), (0, LANE - MLA_ROPE))

    st_lru, st_k, st_v, st_ckv, st_kr = [], [], [], [], []
    for l in range(DEPTH):
        sh1, sc1, g1, sh2, sc2, g2 = (mod[l, :, s] for s in range(6))

        h = norm_mod(x, norm_mix_g[l], sc1, sh1)
        p = matmul(h, w_in, l, tm=1024, tn=512, out_dtype=F32, n=IN_MAIN)
        kr = matmul(h, w_kr16, l, tm=1024, tn=LANE, out_dtype=F32)

        a_ctx, s_lru = lru_mixer(p, conv_w, conv_b, w_gates, lru_prm, h0_ctx, l, latent=False)
        a_lat, _ = lru_mixer(p, conv_w, conv_b, w_gates, lru_prm, state_lru[:, l], l, latent=True)

        (q_c, k_c, v_c, cq_c, ckv_c, kr_c, kf_c, ckvf_c) = prep(
            p, kr, gqa_qn[l], gqa_kn[l], mla_qn[l], mla_kvn[l], latent=False)
        (q_l, k_l, v_l, cq_l, ckv_l, kr_l) = prep(
            p, kr, gqa_qn[l], gqa_kn[l], mla_qn[l], mla_kvn[l], latent=True,
            tables_b=tables_b, tables_c=tables_c)

        b_ctx = gqa_attention(q_c, k_c.reshape(BATCH, SEQ, KV_WIDTH), v_c.reshape(BATCH, SEQ, KV_WIDTH),
                              n_batch=BATCH, t_q=SEQ, t_k=SEQ, tq=SEQ, groups=GQA_KV_HEADS, sub=SEQ)
        k_all = jnp.concatenate([cache_gqa_k[:, l].reshape(DEC_BATCH, PAST_LEN, KV_WIDTH).astype(BF16),
                                 k_l.reshape(DEC_BATCH, DEC_SEQ, KV_WIDTH)], axis=1)
        v_all = jnp.concatenate([cache_gqa_v[:, l].reshape(DEC_BATCH, PAST_LEN, KV_WIDTH).astype(BF16),
                                 v_l.reshape(DEC_BATCH, DEC_SEQ, KV_WIDTH)], axis=1)
        b_lat = gqa_attention(q_l, k_all, v_all, n_batch=DEC_BATCH, t_q=DEC_SEQ, t_k=LAT_KEYS, tq=512,
                              groups=1, sub=256)

        qm_c = matmul(cq_c, w_uq16, l, tm=1024, tn=512, out_dtype=BF16)
        qm_l = matmul_rope_tail(cq_l, w_uq16, l, tables_c, tm=1024, tn=512, first_rope_tile=3)
        kv_c = matmul(ckv_c, w_ukv, l, tm=1024, tn=1024, out_dtype=BF16)
        ckv_all = jnp.concatenate([cache_mla_ckv[:, l].astype(BF16),
                                   ckv_l.reshape(DEC_BATCH, DEC_SEQ, MLA_KV_RANK)], axis=1)
        kv_l = matmul(ckv_all.reshape(DEC_BATCH * LAT_KEYS, MLA_KV_RANK), w_ukv, l,
                      tm=1024, tn=1024, out_dtype=BF16)
        kr_all = jnp.concatenate([jnp.pad(cache_mla_krope[:, l], pad_kr).astype(BF16),
                                  kr_l.reshape(DEC_BATCH, DEC_SEQ, LANE)], axis=1)
        c_ctx_out = mla_attention(qm_c, kv_c.reshape(BATCH, SEQ, -1), kr_c.reshape(BATCH, SEQ, LANE),
                                  n_batch=BATCH, t_q=SEQ, t_k=SEQ, tq=SEQ, heads=MLA_HEADS, sub=SEQ)
        c_lat = mla_attention(qm_l, kv_l.reshape(DEC_BATCH, LAT_KEYS, -1), kr_all,
                              n_batch=DEC_BATCH, t_q=DEC_SEQ, t_k=LAT_KEYS, tq=1024, heads=1, sub=256)

        mix = jnp.concatenate([jnp.concatenate([a_ctx, b_ctx, c_ctx_out], axis=1),
                               jnp.concatenate([a_lat, b_lat, c_lat], axis=1)], axis=0)
        x = matmul_gated_residual(mix, w_out, l, x, g1, tm=1024, tn=512)

        h = norm_mod(x, norm_ffn_g[l], sc2, sh2)
        ff = swiglu_up(h, w_gate, w_up, l, tm=1024, tn=256)
        x = matmul_gated_residual(ff, w_down16, l, x, g2, tm=512, tn=256)

        st_lru.append(s_lru)
        st_k.append(kf_c.reshape(BATCH, SEQ, GQA_KV_HEADS, HEAD_DIM))
        st_v.append(p[:N_CTX, 4096:4096 + KV_WIDTH].reshape(BATCH, SEQ, GQA_KV_HEADS, HEAD_DIM))
        st_ckv.append(ckvf_c.reshape(BATCH, SEQ, MLA_KV_RANK))
        st_kr.append(kr[:N_CTX, :MLA_ROPE].reshape(BATCH, SEQ, MLA_ROPE))

    y = final_norm(x, norm_f)
    return (y[:N_CTX].reshape(BATCH, SEQ, D_MODEL), y[N_CTX:].reshape(DEC_BATCH, DEC_SEQ, D_MODEL),
            jnp.stack(st_lru, axis=1), jnp.stack(st_k, axis=1), jnp.stack(st_v, axis=1),
            jnp.stack(st_ckv, axis=1), jnp.stack(st_kr, axis=1))
```

```python
import functools

import jax
import jax.numpy as jnp
from jax import lax
from jax.experimental import pallas as pl
from jax.experimental.pallas import tpu as pltpu

F32 = jnp.float32
BF16 = jnp.bfloat16

D_MODEL = 4096
BATCH = 16
SEQ = 256
DEPTH = 4
DEC_BATCH = 2
DEC_SEQ = 2048
PAST_LEN = 512
GRID_W = 64
EPS = 1e-6
ROPE_THETA = 10000.0

LRU_WIDTH = 1024
LRU_BLOCKS = 16
LRU_BLOCK = 64
LRU_C = 8.0
HEAD_DIM = 128
GQA_HEADS = 12
GQA_KV_HEADS = 4
GQA_REP = GQA_HEADS // GQA_KV_HEADS
GQA_WIDTH = GQA_HEADS * HEAD_DIM
KV_WIDTH = GQA_KV_HEADS * HEAD_DIM
MLA_HEADS = 12
MLA_Q_RANK = 1024
MLA_KV_RANK = 512
MLA_NOPE = 128
MLA_ROPE = 64
MLA_V = 128
MLA_WIDTH = MLA_HEADS * MLA_V
D_MIX = LRU_WIDTH + GQA_WIDTH + MLA_WIDTH
MIX_B_COL = 0
MIX_C_COL = GQA_WIDTH
MIX_A_COL = GQA_WIDTH + MLA_WIDTH
D_FF = 11008
IN_MAIN = 6144
LANE = 128
N_CTX = BATCH * SEQ
N_LAT = DEC_BATCH * DEC_SEQ
N_TOK = N_CTX + N_LAT
LAT_KEYS = PAST_LEN + DEC_SEQ
VMEM_LIMIT = 56 * 1024 * 1024
LRU_CHUNK = 256
LRU_ROWS = 2048
ROW_TILE = 256


def _params(*sem):
    return pltpu.CompilerParams(dimension_semantics=sem, vmem_limit_bytes=VMEM_LIMIT)


def _cond_of_tile(i, tm):
    row = i * tm
    return jnp.where(row < N_CTX, 0, 1 + (row - N_CTX) // DEC_SEQ)


def _silu(x):
    return x * jax.nn.sigmoid(x)


def _dot(a, b):
    return jnp.dot(a, b, preferred_element_type=F32)


def _dot_nt(a, b):
    return lax.dot_general(a, b, (((1,), (1,)), ((), ())), preferred_element_type=F32)


def _mod_kernel(c_ref, w_ref, b_ref, o_ref):
    s = _silu(c_ref[...]).astype(BF16)
    o_ref[...] = _dot(s, w_ref[...].astype(BF16)) + b_ref[...]


def modulation_all(cond8, w_mod, b_mod):
    tn = 512
    n = 6 * D_MODEL
    return pl.pallas_call(
        _mod_kernel,
        grid=(DEPTH, n // tn),
        in_specs=[
            pl.BlockSpec((8, D_MODEL), lambda l, j: (0, 0)),
            pl.BlockSpec((None, D_MODEL, tn), lambda l, j: (l, 0, j)),
            pl.BlockSpec((None, 1, tn), lambda l, j: (l, 0, j)),
        ],
        out_specs=pl.BlockSpec((None, 8, tn), lambda l, j: (l, 0, j)),
        out_shape=jax.ShapeDtypeStruct((DEPTH, 8, n), F32),
        compiler_params=_params("parallel", "parallel"),
        name="modulation",
    )(cond8, w_mod, b_mod.reshape(DEPTH, 1, n))


def _norm_mod_kernel(x_ref, g_ref, sc_ref, sh_ref, o_ref):
    x = x_ref[...]
    y = x * lax.rsqrt(jnp.mean(x * x, axis=-1, keepdims=True) + EPS) * g_ref[...]
    o_ref[...] = (y * (1.0 + sc_ref[...]) + sh_ref[...]).astype(o_ref.dtype)


def norm_mod(x, g, sc, sh):
    tm = ROW_TILE
    m = x.shape[0]
    return pl.pallas_call(
        _norm_mod_kernel,
        grid=(m // tm,),
        in_specs=[
            pl.BlockSpec((tm, D_MODEL), lambda i: (i, 0)),
            pl.BlockSpec((1, D_MODEL), lambda i: (0, 0)),
            pl.BlockSpec((None, 1, D_MODEL), lambda i: (_cond_of_tile(i, tm), 0, 0)),
            pl.BlockSpec((None, 1, D_MODEL), lambda i: (_cond_of_tile(i, tm), 0, 0)),
        ],
        out_specs=pl.BlockSpec((tm, D_MODEL), lambda i: (i, 0)),
        out_shape=jax.ShapeDtypeStruct((m, D_MODEL), BF16),
        compiler_params=_params("parallel"),
        name="norm_mod",
    )(x, g.reshape(1, D_MODEL), sc, sh)


def _final_norm_kernel(x_ref, g_ref, o_ref):
    x = x_ref[...]
    o_ref[...] = x * lax.rsqrt(jnp.mean(x * x, axis=-1, keepdims=True) + EPS) * g_ref[...]


def final_norm(x, g, row0, n_rows):
    tm = ROW_TILE
    off = row0 // tm
    return pl.pallas_call(
        _final_norm_kernel,
        grid=(n_rows // tm,),
        in_specs=[
            pl.BlockSpec((tm, D_MODEL), lambda i: (i + off, 0)),
            pl.BlockSpec((1, D_MODEL), lambda i: (0, 0)),
        ],
        out_specs=pl.BlockSpec((tm, D_MODEL), lambda i: (i, 0)),
        out_shape=jax.ShapeDtypeStruct((n_rows, D_MODEL), F32),
        compiler_params=_params("parallel"),
        name="final_norm",
    )(x, g.reshape(1, D_MODEL))


def _w16(w_ref):
    return w_ref[...].astype(BF16)


def _mm_kernel(x_ref, w_ref, o_ref):
    o_ref[...] = _dot(x_ref[...], _w16(w_ref)).astype(o_ref.dtype)


def matmul(x, w, layer, *, tm, tn, out_dtype, n=None):
    m, k = x.shape
    n = w.shape[-1] if n is None else n
    return pl.pallas_call(
        _mm_kernel,
        grid=(m // tm, n // tn),
        in_specs=[
            pl.BlockSpec((tm, k), lambda i, j: (i, 0)),
            pl.BlockSpec((None, k, tn), lambda i, j: (layer, 0, j)),
        ],
        out_specs=pl.BlockSpec((tm, tn), lambda i, j: (i, j)),
        out_shape=jax.ShapeDtypeStruct((m, n), out_dtype),
        compiler_params=_params("parallel", "arbitrary"),
        name="matmul",
    )(x, w)


def _mm_nt_kernel(x_ref, w_ref, o_ref):
    o_ref[...] = _dot_nt(x_ref[...], _w16(w_ref)).astype(o_ref.dtype)


def matmul_nt(x, w_t, layer, *, tm, tn, out_dtype, n):
    m, k = x.shape
    return pl.pallas_call(
        _mm_nt_kernel,
        grid=(m // tm, n // tn),
        in_specs=[
            pl.BlockSpec((tm, k), lambda i, j: (i, 0)),
            pl.BlockSpec((None, tn, k), lambda i, j: (layer, j, 0)),
        ],
        out_specs=pl.BlockSpec((tm, tn), lambda i, j: (i, j)),
        out_shape=jax.ShapeDtypeStruct((m, n), out_dtype),
        compiler_params=_params("parallel", "arbitrary"),
        name="matmul_nt",
    )(x, w_t)


def _mm_res_kernel(x_ref, w_ref, r_ref, g_ref, o_ref, *, mix_cols):
    w = _w16(w_ref)
    if mix_cols:
        acc = (_dot(x_ref[:, :MIX_A_COL], w[LRU_WIDTH:]) + _dot(x_ref[:, MIX_A_COL:], w[:LRU_WIDTH]))
    else:
        acc = _dot(x_ref[...], w)
    o_ref[...] = r_ref[...] + g_ref[...] * acc


def matmul_gated_residual(x, w, layer, res, gate, *, tm, tn, mix_cols=False):
    m, k = x.shape
    n = w.shape[-1]
    return pl.pallas_call(
        functools.partial(_mm_res_kernel, mix_cols=mix_cols),
        grid=(m // tm, n // tn),
        in_specs=[
            pl.BlockSpec((tm, k), lambda i, j: (i, 0)),
            pl.BlockSpec((None, k, tn), lambda i, j: (layer, 0, j)),
            pl.BlockSpec((tm, tn), lambda i, j: (i, j)),
            pl.BlockSpec((None, 1, tn), lambda i, j: (_cond_of_tile(i, tm), 0, j)),
        ],
        out_specs=pl.BlockSpec((tm, tn), lambda i, j: (i, j)),
        out_shape=jax.ShapeDtypeStruct((m, n), F32),
        compiler_params=_params("parallel", "arbitrary"),
        name="matmul_gated_residual",
    )(x, w, res, gate)


def _swiglu_kernel(x_ref, wg_ref, wu_ref, wd_ref, o_ref, wd16_ref):
    x = x_ref[...]
    g = _dot(x, _w16(wg_ref))
    u = _dot(x, _w16(wu_ref))
    o_ref[...] = (_silu(g) * u).astype(o_ref.dtype)
    wd16_ref[...] = wd_ref[...].astype(BF16)


def swiglu_up(x, w_gate, w_up, w_down, layer, *, tm, tn):
    m, k = x.shape
    n = w_gate.shape[-1]
    nj = n // tn
    steps = (m // tm) * nj
    kd, nd = w_down.shape[1:]
    slab = kd // steps
    assert slab * steps == kd and slab % 16 == 0
    return pl.pallas_call(
        _swiglu_kernel,
        grid=(m // tm, nj),
        in_specs=[
            pl.BlockSpec((tm, k), lambda i, j: (i, 0)),
            pl.BlockSpec((None, k, tn), lambda i, j: (layer, 0, j)),
            pl.BlockSpec((None, k, tn), lambda i, j: (layer, 0, j)),
            pl.BlockSpec((None, slab, nd), lambda i, j: (layer, i * nj + j, 0)),
        ],
        out_specs=[
            pl.BlockSpec((tm, tn), lambda i, j: (i, j)),
            pl.BlockSpec((None, slab, nd), lambda i, j: (0, i * nj + j, 0)),
        ],
        out_shape=[
            jax.ShapeDtypeStruct((m, n), BF16),
            jax.ShapeDtypeStruct((1, kd, nd), BF16),
        ],
        compiler_params=_params("arbitrary", "arbitrary"),
        name="swiglu_up",
    )(x, w_gate, w_up, w_down)


def _rope(x, c, se, so):
    nxt = pltpu.roll(x, LANE - 1, 1)
    prv = pltpu.roll(x, 1, 1)
    return x * c + nxt * se + prv * so


def _mm_rope_kernel(x_ref, w_ref, c_ref, se_ref, so_ref, o_ref, *, first_rope_tile):
    acc = _dot(x_ref[...], _w16(w_ref))
    j = pl.program_id(1)

    @pl.when(j < first_rope_tile)
    def _():
        o_ref[...] = acc.astype(o_ref.dtype)

    @pl.when(j >= first_rope_tile)
    def _():
        c, se, so = c_ref[...], se_ref[...], so_ref[...]
        for h in range(acc.shape[1] // LANE):
            sl = slice(h * LANE, (h + 1) * LANE)
            o_ref[:, sl] = _rope(acc[:, sl], c, se, so).astype(o_ref.dtype)


def matmul_rope_tail(x, w, layer, tables, *, tm, tn, first_rope_tile):
    m, k = x.shape
    n = w.shape[-1]
    nt = DEC_SEQ // tm
    tab = pl.BlockSpec((tm, LANE), lambda i, j: (i % nt, 0))
    return pl.pallas_call(
        functools.partial(_mm_rope_kernel, first_rope_tile=first_rope_tile),
        grid=(m // tm, n // tn),
        in_specs=[
            pl.BlockSpec((tm, k), lambda i, j: (i, 0)),
            pl.BlockSpec((None, k, tn), lambda i, j: (layer, 0, j)),
            tab, tab, tab,
        ],
        out_specs=pl.BlockSpec((tm, tn), lambda i, j: (i, j)),
        out_shape=jax.ShapeDtypeStruct((m, n), BF16),
        compiler_params=_params("parallel", "arbitrary"),
        name="matmul_rope_tail",
    )(x, w, *tables)


def _head_rmsnorm(x, g):
    return x * lax.rsqrt(jnp.mean(x * x, axis=-1, keepdims=True) + EPS) * g


def _prep_kernel(*refs, rope):
    if rope:
        (qk_ref, v_ref, c_ref, kr_ref, qn_ref, kn_ref, mqn_ref, mkvn_ref,
         cb_ref, seb_ref, sob_ref, cc_ref, sec_ref, soc_ref,
         q_o, k_o, v_o, cq_o, ckv_o, kr_o) = refs
    else:
        (qk_ref, v_ref, c_ref, kr_ref, qn_ref, kn_ref, mqn_ref, mkvn_ref,
         q_o, k_o, v_o, cq_o, ckv_o, kr_o, kf_o, ckvf_o) = refs
    qn, kn = qn_ref[...], kn_ref[...]
    if rope:
        cb, seb, sob = cb_ref[...], seb_ref[...], sob_ref[...]
    for h in range(GQA_HEADS + GQA_KV_HEADS):
        sl = slice(h * LANE, (h + 1) * LANE)
        is_q = h < GQA_HEADS
        y = _head_rmsnorm(qk_ref[:, sl], qn if is_q else kn)
        osl = sl if is_q else slice((h - GQA_HEADS) * LANE, (h - GQA_HEADS + 1) * LANE)
        if not rope and not is_q:
            kf_o[:, osl] = y
        if rope:
            y = _rope(y, cb, seb, sob)
        (q_o if is_q else k_o)[:, osl] = y.astype(BF16)
    v_o[...] = v_ref[...].astype(BF16)
    cq = c_ref[:, :MLA_Q_RANK]
    cq_o[...] = _head_rmsnorm(cq, mqn_ref[...]).astype(BF16)
    ckv = _head_rmsnorm(c_ref[:, MLA_Q_RANK:], mkvn_ref[...])
    ckv_o[...] = ckv.astype(BF16)
    if not rope:
        ckvf_o[...] = ckv
    kr = kr_ref[...]
    if rope:
        kr = _rope(kr, cc_ref[...], sec_ref[...], soc_ref[...])
    kr_o[...] = kr.astype(BF16)


def prep(p, kr, gqa_qn, gqa_kn, mla_qn, mla_kvn, *, latent, tables_b=None, tables_c=None):
    tm = ROW_TILE
    n_rows = N_LAT if latent else N_CTX
    off = (N_CTX // tm) if latent else 0
    nt = DEC_SEQ // tm
    row = lambda c: (lambda i: (i + off, c))
    in_specs = [
        pl.BlockSpec((tm, 2048), row(1)),
        pl.BlockSpec((tm, KV_WIDTH), row(8)),
        pl.BlockSpec((tm, 1536), row(3)),
        pl.BlockSpec((tm, LANE), row(0)),
        pl.BlockSpec((1, HEAD_DIM), lambda i: (0, 0)),
        pl.BlockSpec((1, HEAD_DIM), lambda i: (0, 0)),
        pl.BlockSpec((1, MLA_Q_RANK), lambda i: (0, 0)),
        pl.BlockSpec((1, MLA_KV_RANK), lambda i: (0, 0)),
    ]
    args = [p, p, p, kr, gqa_qn.reshape(1, -1), gqa_kn.reshape(1, -1),
            mla_qn.reshape(1, -1), mla_kvn.reshape(1, -1)]
    out = lambda w, dt: (pl.BlockSpec((tm, w), lambda i: (i, 0)), jax.ShapeDtypeStruct((n_rows, w), dt))
    outs = [out(GQA_WIDTH, BF16), out(KV_WIDTH, BF16), out(KV_WIDTH, BF16),
            out(MLA_Q_RANK, BF16), out(MLA_KV_RANK, BF16), out(LANE, BF16)]
    if latent:
        tab = pl.BlockSpec((tm, LANE), lambda i: (i % nt, 0))
        in_specs += [tab] * 6
        args += list(tables_b) + list(tables_c)
    else:
        outs += [out(KV_WIDTH, F32), out(MLA_KV_RANK, F32)]
    return pl.pallas_call(
        functools.partial(_prep_kernel, rope=latent),
        grid=(n_rows // tm,),
        in_specs=in_specs,
        out_specs=[o[0] for o in outs],
        out_shape=[o[1] for o in outs],
        compiler_params=_params("parallel"),
        name="prep_latent" if latent else "prep_context",
    )(*args)


def _gelu_tanh(x):
    return 0.5 * x * (1.0 + jnp.tanh(0.7978845608028654 * (x + 0.044715 * (x * x * x))))


def _softplus(z):
    return jnp.maximum(z, 0.0) + jnp.log1p(jnp.exp(-jnp.abs(z)))


def _lru_kernel(xa_ref, ga_ref, cw_ref, cb_ref, w_ref, p_ref, h0_ref, mix_ref, o_ref, st_ref,
                a_scr, b_scr, hf_scr, hb_scr, *, t_sub, tiles_per_iter):
    del mix_ref
    rows, c = xa_ref.shape
    n_sub = rows // t_sub
    x = xa_ref[...]
    tpos = lax.broadcasted_iota(jnp.int32, x.shape, 0) % t_sub
    cw = cw_ref[...]
    xm2 = jnp.where(tpos >= 2, pltpu.roll(x, 2, 0), 0.0)
    xm1 = jnp.where(tpos >= 1, pltpu.roll(x, 1, 0), 0.0)
    xp1 = jnp.where(tpos < t_sub - 1, pltpu.roll(x, rows - 1, 0), 0.0)
    xc = cb_ref[...] + xm2 * cw[0:1] + xm1 * cw[1:2] + x * cw[2:3] + xp1 * cw[3:4]
    xc16 = xc.astype(BF16)

    for d in range(2):
        g = _dot(xc16, w_ref[d])
        prm = p_ref[d]
        r = jax.nn.sigmoid(g[:, :c] + prm[0:1])
        ig = jax.nn.sigmoid(g[:, c:] + prm[1:2])
        log_a = (-LRU_C) * r * _softplus(-prm[2:3])
        a = jnp.exp(log_a)
        a_scr[d] = a
        b_scr[d] = jnp.sqrt(-jnp.tanh(log_a) * (a * a + 1.0)) * (ig * xc)

    sub_row = lax.broadcasted_iota(jnp.int32, (8, c), 0)
    keep_f = [sub_row >= s for s in (1, 2, 4)]
    keep_b = [sub_row < 8 - s for s in (1, 2, 4)]

    def tile_scan(a, b, carry, reverse):
        for n, s in enumerate((1, 2, 4)):
            keep, shift = (keep_b[n], 8 - s) if reverse else (keep_f[n], s)
            a_prev = jnp.where(keep, pltpu.roll(a, shift, 0), 1.0)
            b_prev = jnp.where(keep, pltpu.roll(b, shift, 0), 0.0)
            b = a * b_prev + b
            a = a * a_prev
        h = a * carry + b
        last = h[0:1] if reverse else h[7:8]
        return h, jnp.broadcast_to(last, h.shape)

    def body(j, hs):
        hs = list(hs)
        for u in range(tiles_per_iter):
            off = pl.multiple_of((j * tiles_per_iter + u) * 8, 8)
            for q in range(n_sub):
                idx_f = pl.ds(q * t_sub + off, 8)
                idx_b = pl.ds(q * t_sub + (t_sub - 8) - off, 8)
                hf, hs[2 * q] = tile_scan(a_scr[0, idx_f, :], b_scr[0, idx_f, :], hs[2 * q], False)
                hb, hs[2 * q + 1] = tile_scan(a_scr[1, idx_b, :], b_scr[1, idx_b, :], hs[2 * q + 1], True)
                hf_scr[idx_f, :] = hf
                hb_scr[idx_b, :] = hb
        return tuple(hs)

    init = tuple(jnp.broadcast_to(h0_ref[q, d:d + 1, :], (8, c)) for q in range(n_sub) for d in range(2))
    hs = lax.fori_loop(0, t_sub // (8 * tiles_per_iter), body, init)
    for q in range(n_sub):
        st_ref[q, 0:1, :] = hs[2 * q][0:1]
        st_ref[q, 1:2, :] = hs[2 * q + 1][0:1]
    o_ref[...] = ((hf_scr[...] + hb_scr[...]) * _gelu_tanh(ga_ref[...])).astype(o_ref.dtype)


def lru_mixer(p, conv_w, conv_b, w_gates, lru_prm, h0, mix, layer, *, latent):
    c = LRU_CHUNK
    rows = LRU_ROWS
    t_sub = DEC_SEQ if latent else SEQ
    n_sub = rows // t_sub
    n_rows = N_LAT if latent else N_CTX
    off = (N_CTX // rows) if latent else 0
    nck = LRU_WIDTH // c
    return pl.pallas_call(
        functools.partial(_lru_kernel, t_sub=t_sub, tiles_per_iter=4 if latent else 1),
        grid=(n_rows // rows, nck),
        in_specs=[
            pl.BlockSpec((rows, c), lambda s, k: (s + off, k)),
            pl.BlockSpec((rows, c), lambda s, k: (s + off, nck + k)),
            pl.BlockSpec((None, 4, c), lambda s, k: (layer, 0, k)),
            pl.BlockSpec((None, 1, c), lambda s, k: (layer, 0, k)),
            pl.BlockSpec((None, 2, None, c, 2 * c), lambda s, k: (layer, 0, k, 0, 0)),
            pl.BlockSpec((None, 2, 3, c), lambda s, k: (layer, 0, 0, k)),
            pl.BlockSpec((n_sub, 2, c), lambda s, k: (s, 0, k)),
            pl.BlockSpec(memory_space=pl.ANY),
        ],
        out_specs=[
            pl.BlockSpec((rows, c), lambda s, k: (s + off, MIX_A_COL // c + k)),
            pl.BlockSpec((n_sub, 2, c), lambda s, k: (s, 0, k)),
        ],
        out_shape=[
            jax.ShapeDtypeStruct(mix.shape, mix.dtype),
            jax.ShapeDtypeStruct((n_rows // t_sub, 2, LRU_WIDTH), F32),
        ],
        input_output_aliases={7: 0},
        scratch_shapes=[pltpu.VMEM((2, rows, c), F32)] * 2 + [pltpu.VMEM((rows, c), F32)] * 2,
        compiler_params=_params("parallel", "parallel"),
        name="lru_latent" if latent else "lru_context",
    )(p, p, conv_w, conv_b.reshape(DEPTH, 1, LRU_WIDTH), w_gates, lru_prm, h0, mix)


LOG2E = 1.4426950408889634


def _softmax_pv(s, v, scale):
    t = s * (scale * LOG2E)
    m = jnp.max(t, axis=-1, keepdims=True)
    p = jnp.exp2(t - m).astype(BF16)
    oa = _dot(p, jnp.concatenate([v, jnp.ones_like(v)], axis=1))
    return oa[:, :LANE] / oa[:, LANE:]


def _gqa_kernel(q_ref, k_ref, v_ref, mix_ref, o_ref, *, scale, groups, sub):
    del mix_ref
    tq = q_ref.shape[0]
    for g in range(groups):
        kv_sl = slice(g * LANE, (g + 1) * LANE)
        k, v = k_ref[:, kv_sl], v_ref[:, kv_sl]
        for h in range(g * GQA_REP, (g + 1) * GQA_REP):
            sl = slice(h * LANE, (h + 1) * LANE)
            for r0 in range(0, tq, sub):
                s = _dot_nt(q_ref[r0:r0 + sub, sl], k)
                o_ref[r0:r0 + sub, sl] = _softmax_pv(s, v, scale).astype(o_ref.dtype)


def gqa_attention(q, k, v, mix, row0, *, n_batch, t_q, t_k, tq, groups, sub):
    nq = t_q // tq
    qw = groups * GQA_REP * HEAD_DIM
    kw = groups * HEAD_DIM
    r_off, c_off = row0 // tq, MIX_B_COL // qw
    return pl.pallas_call(
        functools.partial(_gqa_kernel, scale=HEAD_DIM ** -0.5, groups=groups, sub=sub),
        grid=(n_batch, GQA_KV_HEADS // groups, nq),
        in_specs=[
            pl.BlockSpec((tq, qw), lambda b, g, i: (b * nq + i, g)),
            pl.BlockSpec((None, t_k, kw), lambda b, g, i: (b, 0, g)),
            pl.BlockSpec((None, t_k, kw), lambda b, g, i: (b, 0, g)),
            pl.BlockSpec(memory_space=pl.ANY),
        ],
        out_specs=pl.BlockSpec((tq, qw), lambda b, g, i: (r_off + b * nq + i, c_off + g)),
        out_shape=jax.ShapeDtypeStruct(mix.shape, mix.dtype),
        input_output_aliases={3: 0},
        compiler_params=_params("parallel", "parallel", "arbitrary"),
        name="gqa_attention",
    )(q, k, v, mix)


def _mla_kernel(qn_ref, qr_ref, kv_ref, kr_ref, mix_ref, o_ref, *, scale, heads, sub):
    del mix_ref
    tq = qn_ref.shape[0]
    kr = kr_ref[...]
    for h in range(heads):
        sl = slice(h * LANE, (h + 1) * LANE)
        k0 = h * (MLA_NOPE + MLA_V)
        k = jnp.concatenate([kv_ref[:, k0:k0 + MLA_NOPE], kr], axis=1)
        v = kv_ref[:, k0 + MLA_NOPE:k0 + MLA_NOPE + MLA_V]
        for r0 in range(0, tq, sub):
            q = jnp.concatenate([qn_ref[r0:r0 + sub, sl], qr_ref[r0:r0 + sub, sl]], axis=1)
            o_ref[r0:r0 + sub, sl] = _softmax_pv(_dot_nt(q, k), v, scale).astype(o_ref.dtype)


def mla_attention(q, kv, kr, mix, row0, *, n_batch, t_q, t_k, tq, heads, sub):
    nq = t_q // tq
    nh = MLA_HEADS // heads
    ow = heads * MLA_V
    r_off, c_off = row0 // tq, MIX_C_COL // ow
    return pl.pallas_call(
        functools.partial(_mla_kernel, scale=(MLA_NOPE + MLA_ROPE) ** -0.5, heads=heads, sub=sub),
        grid=(n_batch, nh, nq),
        in_specs=[
            pl.BlockSpec((tq, heads * MLA_NOPE), lambda b, h, i: (b * nq + i, h)),
            pl.BlockSpec((tq, heads * LANE), lambda b, h, i: (b * nq + i, nh + h)),
            pl.BlockSpec((None, t_k, heads * (MLA_NOPE + MLA_V)), lambda b, h, i: (b, 0, h)),
            pl.BlockSpec((None, t_k, LANE), lambda b, h, i: (b, 0, 0)),
            pl.BlockSpec(memory_space=pl.ANY),
        ],
        out_specs=pl.BlockSpec((tq, ow), lambda b, h, i: (r_off + b * nq + i, c_off + h)),
        out_shape=jax.ShapeDtypeStruct(mix.shape, mix.dtype),
        input_output_aliases={4: 0},
        compiler_params=_params("parallel", "parallel", "arbitrary"),
        name="mla_attention",
    )(q, q, kv, kr, mix)


def _rope_tables(rot_dim):
    rows = DEC_SEQ // GRID_W
    row = jnp.repeat(jnp.arange(rows, dtype=F32), GRID_W)
    col = jnp.tile(jnp.arange(GRID_W, dtype=F32), rows)
    quarter = rot_dim // 4
    freqs = ROPE_THETA ** (-jnp.arange(quarter, dtype=F32) / quarter)
    ang = jnp.concatenate([row[:, None] * freqs, col[:, None] * freqs], axis=-1)
    cos, sin = jnp.cos(ang), jnp.sin(ang)
    zero = jnp.zeros_like(sin)
    c = jnp.repeat(cos, 2, axis=-1)
    se = jnp.stack([-sin, zero], axis=-1).reshape(DEC_SEQ, rot_dim)
    so = jnp.stack([zero, sin], axis=-1).reshape(DEC_SEQ, rot_dim)
    pad = ((0, 0), (0, LANE - rot_dim))
    return tuple(jnp.pad(t, pad) for t in (c, se, so))


def _block_diag_gates(lru_wr, lru_wi):
    per = LRU_CHUNK // LRU_BLOCK
    nck = LRU_WIDTH // LRU_CHUNK
    eye = jnp.eye(per, dtype=F32)

    def bd(w):
        w = w.reshape(DEPTH, 2, nck, per, LRU_BLOCK, LRU_BLOCK)
        w = jnp.einsum("dzcakj,ab->dzcakbj", w, eye)
        return w.reshape(DEPTH, 2, nck, LRU_CHUNK, LRU_CHUNK)

    return jnp.concatenate([bd(lru_wr), bd(lru_wi)], axis=-1).astype(BF16)


def _permute_w_uq(w_uq):
    w = w_uq.reshape(DEPTH, MLA_Q_RANK, MLA_HEADS, MLA_NOPE + MLA_ROPE)
    nope = w[..., :MLA_NOPE].reshape(DEPTH, MLA_Q_RANK, MLA_HEADS * MLA_NOPE)
    rope = jnp.pad(w[..., MLA_NOPE:], ((0, 0), (0, 0), (0, 0), (0, LANE - MLA_ROPE)))
    rope = rope.reshape(DEPTH, MLA_Q_RANK, MLA_HEADS * LANE)
    return jnp.concatenate([nope, rope], axis=-1).astype(BF16)


def kernel(x_prompt, x_sample, state_lru, cache_gqa_k, cache_gqa_v, cache_mla_ckv, cache_mla_krope, c,
           c_ctx, w_mod, b_mod, norm_mix_g, norm_ffn_g, w_in, conv_w, conv_b, lru_wr, lru_br, lru_wi, lru_bi,
           lru_lam, gqa_qn, gqa_kn, mla_qn, mla_kvn, w_uq, w_ukv, w_out, w_gate, w_up, w_down, norm_f):
    x = jnp.concatenate([x_prompt.reshape(N_CTX, D_MODEL), x_sample.reshape(N_LAT, D_MODEL)], axis=0)

    cond8 = jnp.zeros((8, D_MODEL), F32).at[0].set(c_ctx).at[1:1 + DEC_BATCH].set(c)
    mod = modulation_all(cond8, w_mod, b_mod).reshape(DEPTH, 8, 6, 1, D_MODEL)

    w_kr16 = jnp.pad(w_in[:, :, IN_MAIN:], ((0, 0), (0, 0), (0, LANE - MLA_ROPE))).astype(BF16)
    w_in_t = jnp.swapaxes(w_in, 1, 2)
    w_uq16 = _permute_w_uq(w_uq)
    w_gates = _block_diag_gates(lru_wr, lru_wi)
    lru_prm = jnp.stack([lru_br, lru_bi, lru_lam], axis=2)
    tables_b = _rope_tables(HEAD_DIM)
    tables_c = _rope_tables(MLA_ROPE)
    h0_ctx = jnp.zeros((BATCH, 2, LRU_WIDTH), F32)
    pad_kr = ((0, 0), (0, 0), (0, LANE - MLA_ROPE))

    mix = jnp.zeros((N_TOK, D_MIX), BF16)

    st_lru, st_k, st_v, st_ckv, st_kr = [], [], [], [], []
    for l in range(DEPTH):
        sh1, sc1, g1, sh2, sc2, g2 = (mod[l, :, s] for s in range(6))

        h = norm_mod(x, norm_mix_g[l], sc1, sh1)
        p = matmul_nt(h, w_in_t, l, tm=1024, tn=512, out_dtype=F32, n=IN_MAIN)
        kr = matmul(h, w_kr16, l, tm=1024, tn=LANE, out_dtype=F32)

        mix, s_lru = lru_mixer(p, conv_w, conv_b, w_gates, lru_prm, h0_ctx, mix, l, latent=False)
        mix, _ = lru_mixer(p, conv_w, conv_b, w_gates, lru_prm, state_lru[:, l], mix, l, latent=True)

        (q_c, k_c, v_c, cq_c, ckv_c, kr_c, kf_c, ckvf_c) = prep(
            p, kr, gqa_qn[l], gqa_kn[l], mla_qn[l], mla_kvn[l], latent=False)
        (q_l, k_l, v_l, cq_l, ckv_l, kr_l) = prep(
            p, kr, gqa_qn[l], gqa_kn[l], mla_qn[l], mla_kvn[l], latent=True,
            tables_b=tables_b, tables_c=tables_c)

        mix = gqa_attention(q_c, k_c.reshape(BATCH, SEQ, KV_WIDTH), v_c.reshape(BATCH, SEQ, KV_WIDTH), mix, 0,
                            n_batch=BATCH, t_q=SEQ, t_k=SEQ, tq=SEQ, groups=GQA_KV_HEADS, sub=SEQ)
        k_all = jnp.concatenate([cache_gqa_k[:, l].reshape(DEC_BATCH, PAST_LEN, KV_WIDTH).astype(BF16),
                                 k_l.reshape(DEC_BATCH, DEC_SEQ, KV_WIDTH)], axis=1)
        v_all = jnp.concatenate([cache_gqa_v[:, l].reshape(DEC_BATCH, PAST_LEN, KV_WIDTH).astype(BF16),
                                 v_l.reshape(DEC_BATCH, DEC_SEQ, KV_WIDTH)], axis=1)
        mix = gqa_attention(q_l, k_all, v_all, mix, N_CTX, n_batch=DEC_BATCH, t_q=DEC_SEQ, t_k=LAT_KEYS,
                            tq=512, groups=1, sub=256)

        qm_c = matmul(cq_c, w_uq16, l, tm=1024, tn=512, out_dtype=BF16)
        qm_l = matmul_rope_tail(cq_l, w_uq16, l, tables_c, tm=1024, tn=512, first_rope_tile=3)
        kv_c = matmul(ckv_c, w_ukv, l, tm=1024, tn=1024, out_dtype=BF16)
        ckv_all = jnp.concatenate([cache_mla_ckv[:, l].astype(BF16),
                                   ckv_l.reshape(DEC_BATCH, DEC_SEQ, MLA_KV_RANK)], axis=1)
        kv_l = matmul(ckv_all.reshape(DEC_BATCH * LAT_KEYS, MLA_KV_RANK), w_ukv, l,
                      tm=1024, tn=1024, out_dtype=BF16)
        kr_all = jnp.concatenate([jnp.pad(cache_mla_krope[:, l], pad_kr).astype(BF16),
                                  kr_l.reshape(DEC_BATCH, DEC_SEQ, LANE)], axis=1)
        mix = mla_attention(qm_c, kv_c.reshape(BATCH, SEQ, -1), kr_c.reshape(BATCH, SEQ, LANE), mix, 0,
                            n_batch=BATCH, t_q=SEQ, t_k=SEQ, tq=SEQ, heads=MLA_HEADS, sub=SEQ)
        mix = mla_attention(qm_l, kv_l.reshape(DEC_BATCH, LAT_KEYS, -1), kr_all, mix, N_CTX,
                            n_batch=DEC_BATCH, t_q=DEC_SEQ, t_k=LAT_KEYS, tq=1024, heads=1, sub=256)

        x = matmul_gated_residual(mix, w_out, l, x, g1, tm=1024, tn=512, mix_cols=True)

        h = norm_mod(x, norm_ffn_g[l], sc2, sh2)
        ff, w_down16 = swiglu_up(h, w_gate, w_up, w_down, l, tm=1024, tn=256)
        x = matmul_gated_residual(ff, w_down16, 0, x, g2, tm=512, tn=512)

        st_lru.append(s_lru)
        st_k.append(kf_c.reshape(BATCH, SEQ, GQA_KV_HEADS, HEAD_DIM))
        st_v.append(p[:N_CTX, 4096:4096 + KV_WIDTH].reshape(BATCH, SEQ, GQA_KV_HEADS, HEAD_DIM))
        st_ckv.append(ckvf_c.reshape(BATCH, SEQ, MLA_KV_RANK))
        st_kr.append(kr[:N_CTX, :MLA_ROPE].reshape(BATCH, SEQ, MLA_ROPE))

    y_ctx = final_norm(x, norm_f, 0, N_CTX)
    y_lat = final_norm(x, norm_f, N_CTX, N_LAT)
    return (y_ctx.reshape(BATCH, SEQ, D_MODEL), y_lat.reshape(DEC_BATCH, DEC_SEQ, D_MODEL),
            jnp.stack(st_lru, axis=1), jnp.stack(st_k, axis=1), jnp.stack(st_v, axis=1),
            jnp.stack(st_ckv, axis=1), jnp.stack(st_kr, axis=1))
```

```python
import functools

import jax
import jax.numpy as jnp
from jax import lax
from jax.experimental import pallas as pl
from jax.experimental.pallas import tpu as pltpu

F32 = jnp.float32
BF16 = jnp.bfloat16

D_MODEL = 4096
BATCH = 16
SEQ = 256
DEPTH = 4
DEC_BATCH = 2
DEC_SEQ = 2048
PAST_LEN = 512
GRID_W = 64
EPS = 1e-6
ROPE_THETA = 10000.0

LRU_WIDTH = 1024
LRU_BLOCKS = 16
LRU_BLOCK = 64
LRU_C = 8.0
HEAD_DIM = 128
GQA_HEADS = 12
GQA_KV_HEADS = 4
GQA_REP = GQA_HEADS // GQA_KV_HEADS
GQA_WIDTH = GQA_HEADS * HEAD_DIM
KV_WIDTH = GQA_KV_HEADS * HEAD_DIM
MLA_HEADS = 12
MLA_Q_RANK = 1024
MLA_KV_RANK = 512
MLA_NOPE = 128
MLA_ROPE = 64
MLA_V = 128
MLA_WIDTH = MLA_HEADS * MLA_V
D_MIX = LRU_WIDTH + GQA_WIDTH + MLA_WIDTH
MIX_B_COL = 0
MIX_C_COL = GQA_WIDTH
MIX_A_COL = GQA_WIDTH + MLA_WIDTH
D_FF = 11008
IN_MAIN = 6144
LANE = 128
N_CTX = BATCH * SEQ
N_LAT = DEC_BATCH * DEC_SEQ
N_TOK = N_CTX + N_LAT
LAT_KEYS = PAST_LEN + DEC_SEQ
VMEM_LIMIT = 56 * 1024 * 1024
LRU_CHUNK = 256
LRU_ROWS = 2048
ROW_TILE = 256
NORM_TILE = 512


def _params(*sem):
    return pltpu.CompilerParams(dimension_semantics=sem, vmem_limit_bytes=VMEM_LIMIT)


def _cond_of_tile(i, tm):
    row = i * tm
    return jnp.where(row < N_CTX, 0, 1 + (row - N_CTX) // DEC_SEQ)


def _silu(x):
    return x * jax.nn.sigmoid(x)


def _dot(a, b):
    return jnp.dot(a, b, preferred_element_type=F32)


def _dot_nt(a, b):
    return lax.dot_general(a, b, (((1,), (1,)), ((), ())), preferred_element_type=F32)


def _mod_kernel(c_ref, w_ref, b_ref, o_ref):
    s = _silu(c_ref[...]).astype(BF16)
    o_ref[...] = _dot(s, w_ref[...].astype(BF16)) + b_ref[...]


def modulation_all(cond8, w_mod, b_mod):
    tn = 512
    n = 6 * D_MODEL
    return pl.pallas_call(
        _mod_kernel,
        grid=(DEPTH, n // tn),
        in_specs=[
            pl.BlockSpec((8, D_MODEL), lambda l, j: (0, 0)),
            pl.BlockSpec((None, D_MODEL, tn), lambda l, j: (l, 0, j)),
            pl.BlockSpec((None, 1, tn), lambda l, j: (l, 0, j)),
        ],
        out_specs=pl.BlockSpec((None, 8, tn), lambda l, j: (l, 0, j)),
        out_shape=jax.ShapeDtypeStruct((DEPTH, 8, n), F32),
        compiler_params=_params("parallel", "parallel"),
        name="modulation",
    )(cond8, w_mod, b_mod.reshape(DEPTH, 1, n))


def _row_parts(x, tm, tn=None):
    def spec(row_fn, col_fn):
        if tn is None:
            return pl.BlockSpec((tm, D_MODEL), lambda i: (row_fn(i), 0))
        return pl.BlockSpec((tm, tn), lambda i, j: (row_fn(i), col_fn(i, j)))

    if not isinstance(x, tuple):
        return [x], [spec(lambda i: i, lambda i, j: j)]
    n_a = x[0].shape[0] // tm
    return list(x), [
        spec(lambda i: jnp.minimum(i, n_a - 1), lambda i, j: jnp.where(i < n_a, j, 0)),
        spec(lambda i: jnp.maximum(i - n_a, 0), lambda i, j: jnp.where(i < n_a, 0, j)),
    ]


def _read_rows(refs, tm):
    if len(refs) == 1:
        return refs[0][...]
    n_a = N_CTX // tm
    return jnp.where(pl.program_id(0) < n_a, refs[0][...], refs[1][...])


def _norm_mod_kernel(*refs):
    *x_refs, g_ref, sc_ref, sh_ref, o_ref = refs
    x = _read_rows(x_refs, o_ref.shape[0])
    y = x * lax.rsqrt(jnp.mean(x * x, axis=-1, keepdims=True) + EPS) * g_ref[...]
    o_ref[...] = (y * (1.0 + sc_ref[...]) + sh_ref[...]).astype(o_ref.dtype)


def norm_mod(x, g, sc, sh):
    tm = NORM_TILE
    x_args, x_specs = _row_parts(x, tm)
    return pl.pallas_call(
        _norm_mod_kernel,
        grid=(N_TOK // tm,),
        in_specs=x_specs + [
            pl.BlockSpec((1, D_MODEL), lambda i: (0, 0)),
            pl.BlockSpec((None, 1, D_MODEL), lambda i: (_cond_of_tile(i, tm), 0, 0)),
            pl.BlockSpec((None, 1, D_MODEL), lambda i: (_cond_of_tile(i, tm), 0, 0)),
        ],
        out_specs=pl.BlockSpec((tm, D_MODEL), lambda i: (i, 0)),
        out_shape=jax.ShapeDtypeStruct((N_TOK, D_MODEL), BF16),
        compiler_params=_params("parallel"),
        name="norm_mod",
    )(*x_args, g.reshape(1, D_MODEL), sc, sh)


def _final_norm_kernel(x_ref, g_ref, o_ref):
    x = x_ref[...]
    o_ref[...] = x * lax.rsqrt(jnp.mean(x * x, axis=-1, keepdims=True) + EPS) * g_ref[...]


def final_norm(x, g, row0, n_rows):
    tm = NORM_TILE
    off = row0 // tm
    return pl.pallas_call(
        _final_norm_kernel,
        grid=(n_rows // tm,),
        in_specs=[
            pl.BlockSpec((tm, D_MODEL), lambda i: (i + off, 0)),
            pl.BlockSpec((1, D_MODEL), lambda i: (0, 0)),
        ],
        out_specs=pl.BlockSpec((tm, D_MODEL), lambda i: (i, 0)),
        out_shape=jax.ShapeDtypeStruct((n_rows, D_MODEL), F32),
        compiler_params=_params("parallel"),
        name="final_norm",
    )(x, g.reshape(1, D_MODEL))


def _w16(w_ref):
    return w_ref[...].astype(BF16)


def _mm_kernel(x_ref, w_ref, o_ref):
    o_ref[...] = _dot(x_ref[...], _w16(w_ref)).astype(o_ref.dtype)


def matmul(x, w, layer, *, tm, tn, out_dtype, n=None):
    m, k = x.shape
    n = w.shape[-1] if n is None else n
    return pl.pallas_call(
        _mm_kernel,
        grid=(m // tm, n // tn),
        in_specs=[
            pl.BlockSpec((tm, k), lambda i, j: (i, 0)),
            pl.BlockSpec((None, k, tn), lambda i, j: (layer, 0, j)),
        ],
        out_specs=pl.BlockSpec((tm, tn), lambda i, j: (i, j)),
        out_shape=jax.ShapeDtypeStruct((m, n), out_dtype),
        compiler_params=_params("parallel", "arbitrary"),
        name="matmul",
    )(x, w)


def _mm_nt_kernel(x_ref, w_ref, o_ref):
    o_ref[...] = _dot_nt(x_ref[...], _w16(w_ref)).astype(o_ref.dtype)


def matmul_nt(x, w_t, layer, *, tm, tn, out_dtype, n):
    m, k = x.shape
    return pl.pallas_call(
        _mm_nt_kernel,
        grid=(m // tm, n // tn),
        in_specs=[
            pl.BlockSpec((tm, k), lambda i, j: (i, 0)),
            pl.BlockSpec((None, tn, k), lambda i, j: (layer, j, 0)),
        ],
        out_specs=pl.BlockSpec((tm, tn), lambda i, j: (i, j)),
        out_shape=jax.ShapeDtypeStruct((m, n), out_dtype),
        compiler_params=_params("parallel", "arbitrary"),
        name="matmul_nt",
    )(x, w_t)


def _mm_res_kernel(x_ref, w_ref, *refs, mix_cols):
    *r_refs, g_ref, o_ref = refs
    w = _w16(w_ref)
    if mix_cols:
        acc = (_dot(x_ref[:, :MIX_A_COL], w[LRU_WIDTH:]) + _dot(x_ref[:, MIX_A_COL:], w[:LRU_WIDTH]))
    else:
        acc = _dot(x_ref[...], w)
    o_ref[...] = _read_rows(r_refs, o_ref.shape[0]) + g_ref[...] * acc


def matmul_gated_residual(x, w, layer, res, gate, *, tm, tn, mix_cols=False):
    m, k = x.shape
    n = w.shape[-1]
    r_args, r_specs = _row_parts(res, tm, tn)
    return pl.pallas_call(
        functools.partial(_mm_res_kernel, mix_cols=mix_cols),
        grid=(m // tm, n // tn),
        in_specs=[
            pl.BlockSpec((tm, k), lambda i, j: (i, 0)),
            pl.BlockSpec((None, k, tn), lambda i, j: (layer, 0, j)),
        ] + r_specs + [
            pl.BlockSpec((None, 1, tn), lambda i, j: (_cond_of_tile(i, tm), 0, j)),
        ],
        out_specs=pl.BlockSpec((tm, tn), lambda i, j: (i, j)),
        out_shape=jax.ShapeDtypeStruct((m, n), F32),
        compiler_params=_params("parallel", "arbitrary"),
        name="matmul_gated_residual",
    )(x, w, *r_args, gate)


def _swiglu_kernel(x_ref, wg_ref, wu_ref, wd_ref, o_ref, wd16_ref):
    x = x_ref[...]
    g = _dot(x, _w16(wg_ref))
    u = _dot(x, _w16(wu_ref))
    o_ref[...] = (_silu(g) * u).astype(o_ref.dtype)
    wd16_ref[...] = wd_ref[...].astype(BF16)


def swiglu_up(x, w_gate, w_up, w_down, layer, *, tm, tn):
    m, k = x.shape
    n = w_gate.shape[-1]
    nj = n // tn
    steps = (m // tm) * nj
    kd, nd = w_down.shape[1:]
    slab = kd // steps
    assert slab * steps == kd and slab % 16 == 0
    return pl.pallas_call(
        _swiglu_kernel,
        grid=(m // tm, nj),
        in_specs=[
            pl.BlockSpec((tm, k), lambda i, j: (i, 0), pipeline_mode=pl.Buffered(1)),
            pl.BlockSpec((None, k, tn), lambda i, j: (layer, 0, j)),
            pl.BlockSpec((None, k, tn), lambda i, j: (layer, 0, j)),
            pl.BlockSpec((None, slab, nd), lambda i, j: (layer, i * nj + j, 0)),
        ],
        out_specs=[
            pl.BlockSpec((tm, tn), lambda i, j: (i, j)),
            pl.BlockSpec((None, slab, nd), lambda i, j: (0, i * nj + j, 0)),
        ],
        out_shape=[
            jax.ShapeDtypeStruct((m, n), BF16),
            jax.ShapeDtypeStruct((1, kd, nd), BF16),
        ],
        compiler_params=_params("arbitrary", "arbitrary"),
        name="swiglu_up",
    )(x, w_gate, w_up, w_down)


def _rope(x, c, se, so):
    nxt = pltpu.roll(x, LANE - 1, 1)
    prv = pltpu.roll(x, 1, 1)
    return x * c + nxt * se + prv * so


def _mm_rope_kernel(x_ref, w_ref, c_ref, se_ref, so_ref, o_ref, *, first_rope_tile):
    acc = _dot(x_ref[...], _w16(w_ref))
    j = pl.program_id(1)

    @pl.when(j < first_rope_tile)
    def _():
        o_ref[...] = acc.astype(o_ref.dtype)

    @pl.when(j >= first_rope_tile)
    def _():
        c, se, so = c_ref[...], se_ref[...], so_ref[...]
        for h in range(acc.shape[1] // LANE):
            sl = slice(h * LANE, (h + 1) * LANE)
            o_ref[:, sl] = _rope(acc[:, sl], c, se, so).astype(o_ref.dtype)


def matmul_rope_tail(x, w, layer, tables, *, tm, tn, first_rope_tile):
    m, k = x.shape
    n = w.shape[-1]
    nt = DEC_SEQ // tm
    tab = pl.BlockSpec((tm, LANE), lambda i, j: (i % nt, 0))
    return pl.pallas_call(
        functools.partial(_mm_rope_kernel, first_rope_tile=first_rope_tile),
        grid=(m // tm, n // tn),
        in_specs=[
            pl.BlockSpec((tm, k), lambda i, j: (i, 0)),
            pl.BlockSpec((None, k, tn), lambda i, j: (layer, 0, j)),
            tab, tab, tab,
        ],
        out_specs=pl.BlockSpec((tm, tn), lambda i, j: (i, j)),
        out_shape=jax.ShapeDtypeStruct((m, n), BF16),
        compiler_params=_params("parallel", "arbitrary"),
        name="matmul_rope_tail",
    )(x, w, *tables)


def _head_rmsnorm(x, g):
    return x * lax.rsqrt(jnp.mean(x * x, axis=-1, keepdims=True) + EPS) * g


def _prep_kernel(*refs, rope):
    if rope:
        (qk_ref, v_ref, c_ref, kr_ref, qn_ref, kn_ref, mqn_ref, mkvn_ref,
         cb_ref, seb_ref, sob_ref, cc_ref, sec_ref, soc_ref,
         q_o, k_o, v_o, cq_o, ckv_o, kr_o) = refs
    else:
        (qk_ref, v_ref, c_ref, kr_ref, qn_ref, kn_ref, mqn_ref, mkvn_ref,
         q_o, k_o, v_o, cq_o, ckv_o, kr_o, kf_o, ckvf_o) = refs
    qn, kn = qn_ref[...], kn_ref[...]
    if rope:
        cb, seb, sob = cb_ref[...], seb_ref[...], sob_ref[...]
    for h in range(GQA_HEADS + GQA_KV_HEADS):
        sl = slice(h * LANE, (h + 1) * LANE)
        is_q = h < GQA_HEADS
        y = _head_rmsnorm(qk_ref[:, sl], qn if is_q else kn)
        osl = sl if is_q else slice((h - GQA_HEADS) * LANE, (h - GQA_HEADS + 1) * LANE)
        if not rope and not is_q:
            kf_o[:, osl] = y
        if rope:
            y = _rope(y, cb, seb, sob)
        (q_o if is_q else k_o)[:, osl] = y.astype(BF16)
    v_o[...] = v_ref[...].astype(BF16)
    cq = c_ref[:, :MLA_Q_RANK]
    cq_o[...] = _head_rmsnorm(cq, mqn_ref[...]).astype(BF16)
    ckv = _head_rmsnorm(c_ref[:, MLA_Q_RANK:], mkvn_ref[...])
    ckv_o[...] = ckv.astype(BF16)
    if not rope:
        ckvf_o[...] = ckv
    kr = kr_ref[...]
    if rope:
        kr = _rope(kr, cc_ref[...], sec_ref[...], soc_ref[...])
    kr_o[...] = kr.astype(BF16)


def prep(p, kr, gqa_qn, gqa_kn, mla_qn, mla_kvn, *, latent, tables_b=None, tables_c=None):
    tm = ROW_TILE
    n_rows = N_LAT if latent else N_CTX
    off = (N_CTX // tm) if latent else 0
    nt = DEC_SEQ // tm
    row = lambda c: (lambda i: (i + off, c))
    in_specs = [
        pl.BlockSpec((tm, 2048), row(1)),
        pl.BlockSpec((tm, KV_WIDTH), row(8)),
        pl.BlockSpec((tm, 1536), row(3)),
        pl.BlockSpec((tm, LANE), row(0)),
        pl.BlockSpec((1, HEAD_DIM), lambda i: (0, 0)),
        pl.BlockSpec((1, HEAD_DIM), lambda i: (0, 0)),
        pl.BlockSpec((1, MLA_Q_RANK), lambda i: (0, 0)),
        pl.BlockSpec((1, MLA_KV_RANK), lambda i: (0, 0)),
    ]
    args = [p, p, p, kr, gqa_qn.reshape(1, -1), gqa_kn.reshape(1, -1),
            mla_qn.reshape(1, -1), mla_kvn.reshape(1, -1)]
    out = lambda w, dt: (pl.BlockSpec((tm, w), lambda i: (i, 0)), jax.ShapeDtypeStruct((n_rows, w), dt))
    outs = [out(GQA_WIDTH, BF16), out(KV_WIDTH, BF16), out(KV_WIDTH, BF16),
            out(MLA_Q_RANK, BF16), out(MLA_KV_RANK, BF16), out(LANE, BF16)]
    if latent:
        tab = pl.BlockSpec((tm, LANE), lambda i: (i % nt, 0))
        in_specs += [tab] * 6
        args += list(tables_b) + list(tables_c)
    else:
        outs += [out(KV_WIDTH, F32), out(MLA_KV_RANK, F32)]
    return pl.pallas_call(
        functools.partial(_prep_kernel, rope=latent),
        grid=(n_rows // tm,),
        in_specs=in_specs,
        out_specs=[o[0] for o in outs],
        out_shape=[o[1] for o in outs],
        compiler_params=_params("parallel"),
        name="prep_latent" if latent else "prep_context",
    )(*args)


def _gelu_tanh(x):
    return 0.5 * x * (1.0 + jnp.tanh(0.7978845608028654 * (x + 0.044715 * (x * x * x))))


def _softplus(z):
    return jnp.maximum(z, 0.0) + jnp.log1p(jnp.exp(-jnp.abs(z)))


def _lru_kernel(xa_ref, ga_ref, cw_ref, cb_ref, w_ref, p_ref, h0_ref, mix_ref, o_ref, st_ref,
                a_scr, b_scr, hf_scr, hb_scr, *, t_sub, tiles_per_iter):
    del mix_ref
    rows, c = xa_ref.shape
    n_sub = rows // t_sub
    x = xa_ref[...]
    tpos = lax.broadcasted_iota(jnp.int32, x.shape, 0) % t_sub
    cw = cw_ref[...]
    xm2 = jnp.where(tpos >= 2, pltpu.roll(x, 2, 0), 0.0)
    xm1 = jnp.where(tpos >= 1, pltpu.roll(x, 1, 0), 0.0)
    xp1 = jnp.where(tpos < t_sub - 1, pltpu.roll(x, rows - 1, 0), 0.0)
    xc = cb_ref[...] + xm2 * cw[0:1] + xm1 * cw[1:2] + x * cw[2:3] + xp1 * cw[3:4]
    xc16 = xc.astype(BF16)

    for d in range(2):
        g = _dot(xc16, w_ref[d])
        prm = p_ref[d]
        r = jax.nn.sigmoid(g[:, :c] + prm[0:1])
        ig = jax.nn.sigmoid(g[:, c:] + prm[1:2])
        log_a = (-LRU_C) * r * _softplus(-prm[2:3])
        a = jnp.exp(log_a)
        a_scr[d] = a
        b_scr[d] = jnp.sqrt(-jnp.tanh(log_a) * (a * a + 1.0)) * (ig * xc)

    sub_row = lax.broadcasted_iota(jnp.int32, (8, c), 0)
    keep_f = [sub_row >= s for s in (1, 2, 4)]
    keep_b = [sub_row < 8 - s for s in (1, 2, 4)]

    def tile_scan(a, b, carry, reverse):
        for n, s in enumerate((1, 2, 4)):
            keep, shift = (keep_b[n], 8 - s) if reverse else (keep_f[n], s)
            a_prev = jnp.where(keep, pltpu.roll(a, shift, 0), 1.0)
            b_prev = jnp.where(keep, pltpu.roll(b, shift, 0), 0.0)
            b = a * b_prev + b
            a = a * a_prev
        h = a * carry + b
        last = h[0:1] if reverse else h[7:8]
        return h, jnp.broadcast_to(last, h.shape)

    def body(j, hs):
        hs = list(hs)
        for u in range(tiles_per_iter):
            off = pl.multiple_of((j * tiles_per_iter + u) * 8, 8)
            for q in range(n_sub):
                idx_f = pl.ds(q * t_sub + off, 8)
                idx_b = pl.ds(q * t_sub + (t_sub - 8) - off, 8)
                hf, hs[2 * q] = tile_scan(a_scr[0, idx_f, :], b_scr[0, idx_f, :], hs[2 * q], False)
                hb, hs[2 * q + 1] = tile_scan(a_scr[1, idx_b, :], b_scr[1, idx_b, :], hs[2 * q + 1], True)
                hf_scr[idx_f, :] = hf
                hb_scr[idx_b, :] = hb
        return tuple(hs)

    init = tuple(jnp.broadcast_to(h0_ref[q, d:d + 1, :], (8, c)) for q in range(n_sub) for d in range(2))
    hs = lax.fori_loop(0, t_sub // (8 * tiles_per_iter), body, init)
    for q in range(n_sub):
        st_ref[q, 0:1, :] = hs[2 * q][0:1]
        st_ref[q, 1:2, :] = hs[2 * q + 1][0:1]
    o_ref[...] = ((hf_scr[...] + hb_scr[...]) * _gelu_tanh(ga_ref[...])).astype(o_ref.dtype)


def lru_mixer(p, conv_w, conv_b, w_gates, lru_prm, h0, mix, layer, *, latent):
    c = LRU_CHUNK
    rows = LRU_ROWS
    t_sub = DEC_SEQ if latent else SEQ
    n_sub = rows // t_sub
    n_rows = N_LAT if latent else N_CTX
    off = (N_CTX // rows) if latent else 0
    nck = LRU_WIDTH // c
    return pl.pallas_call(
        functools.partial(_lru_kernel, t_sub=t_sub, tiles_per_iter=4 if latent else 1),
        grid=(n_rows // rows, nck),
        in_specs=[
            pl.BlockSpec((rows, c), lambda s, k: (s + off, k)),
            pl.BlockSpec((rows, c), lambda s, k: (s + off, nck + k)),
            pl.BlockSpec((None, 4, c), lambda s, k: (layer, 0, k)),
            pl.BlockSpec((None, 1, c), lambda s, k: (layer, 0, k)),
            pl.BlockSpec((None, 2, None, c, 2 * c), lambda s, k: (layer, 0, k, 0, 0)),
            pl.BlockSpec((None, 2, 3, c), lambda s, k: (layer, 0, 0, k)),
            pl.BlockSpec((n_sub, 2, c), lambda s, k: (s, 0, k)),
            pl.BlockSpec(memory_space=pl.ANY),
        ],
        out_specs=[
            pl.BlockSpec((rows, c), lambda s, k: (s + off, MIX_A_COL // c + k)),
            pl.BlockSpec((n_sub, 2, c), lambda s, k: (s, 0, k)),
        ],
        out_shape=[
            jax.ShapeDtypeStruct(mix.shape, mix.dtype),
            jax.ShapeDtypeStruct((n_rows // t_sub, 2, LRU_WIDTH), F32),
        ],
        input_output_aliases={7: 0},
        scratch_shapes=[pltpu.VMEM((2, rows, c), F32)] * 2 + [pltpu.VMEM((rows, c), F32)] * 2,
        compiler_params=_params("parallel", "parallel"),
        name="lru_latent" if latent else "lru_context",
    )(p, p, conv_w, conv_b.reshape(DEPTH, 1, LRU_WIDTH), w_gates, lru_prm, h0, mix)


LOG2E = 1.4426950408889634


def _softmax_pv(s, v, scale):
    t = s * (scale * LOG2E)
    m = jnp.max(t, axis=-1, keepdims=True)
    p = jnp.exp2(t - m).astype(BF16)
    oa = _dot(p, jnp.concatenate([v, jnp.ones_like(v)], axis=1))
    return oa[:, :LANE] / oa[:, LANE:]


def _gqa_kernel(q_ref, k_ref, v_ref, mix_ref, o_ref, *, scale, groups, sub):
    del mix_ref
    tq = q_ref.shape[0]
    for g in range(groups):
        kv_sl = slice(g * LANE, (g + 1) * LANE)
        k, v = k_ref[:, kv_sl], v_ref[:, kv_sl]
        for h in range(g * GQA_REP, (g + 1) * GQA_REP):
            sl = slice(h * LANE, (h + 1) * LANE)
            for r0 in range(0, tq, sub):
                s = _dot_nt(q_ref[r0:r0 + sub, sl], k)
                o_ref[r0:r0 + sub, sl] = _softmax_pv(s, v, scale).astype(o_ref.dtype)


def gqa_attention(q, k, v, mix, row0, *, n_batch, t_q, t_k, tq, groups, sub):
    nq = t_q // tq
    qw = groups * GQA_REP * HEAD_DIM
    kw = groups * HEAD_DIM
    r_off, c_off = row0 // tq, MIX_B_COL // qw
    return pl.pallas_call(
        functools.partial(_gqa_kernel, scale=HEAD_DIM ** -0.5, groups=groups, sub=sub),
        grid=(n_batch, GQA_KV_HEADS // groups, nq),
        in_specs=[
            pl.BlockSpec((tq, qw), lambda b, g, i: (b * nq + i, g)),
            pl.BlockSpec((None, t_k, kw), lambda b, g, i: (b, 0, g)),
            pl.BlockSpec((None, t_k, kw), lambda b, g, i: (b, 0, g)),
            pl.BlockSpec(memory_space=pl.ANY),
        ],
        out_specs=pl.BlockSpec((tq, qw), lambda b, g, i: (r_off + b * nq + i, c_off + g)),
        out_shape=jax.ShapeDtypeStruct(mix.shape, mix.dtype),
        input_output_aliases={3: 0},
        compiler_params=_params("parallel", "parallel", "arbitrary"),
        name="gqa_attention",
    )(q, k, v, mix)


def _mla_kernel(qn_ref, qr_ref, kv_ref, kr_ref, mix_ref, o_ref, *, scale, heads, sub):
    del mix_ref
    tq = qn_ref.shape[0]
    kr = kr_ref[...]
    for h in range(heads):
        sl = slice(h * LANE, (h + 1) * LANE)
        k0 = h * (MLA_NOPE + MLA_V)
        k = jnp.concatenate([kv_ref[:, k0:k0 + MLA_NOPE], kr], axis=1)
        v = kv_ref[:, k0 + MLA_NOPE:k0 + MLA_NOPE + MLA_V]
        for r0 in range(0, tq, sub):
            q = jnp.concatenate([qn_ref[r0:r0 + sub, sl], qr_ref[r0:r0 + sub, sl]], axis=1)
            o_ref[r0:r0 + sub, sl] = _softmax_pv(_dot_nt(q, k), v, scale).astype(o_ref.dtype)


def mla_attention(q, kv, kr, mix, row0, *, n_batch, t_q, t_k, tq, heads, sub):
    nq = t_q // tq
    nh = MLA_HEADS // heads
    ow = heads * MLA_V
    r_off, c_off = row0 // tq, MIX_C_COL // ow
    return pl.pallas_call(
        functools.partial(_mla_kernel, scale=(MLA_NOPE + MLA_ROPE) ** -0.5, heads=heads, sub=sub),
        grid=(n_batch, nh, nq),
        in_specs=[
            pl.BlockSpec((tq, heads * MLA_NOPE), lambda b, h, i: (b * nq + i, h)),
            pl.BlockSpec((tq, heads * LANE), lambda b, h, i: (b * nq + i, nh + h)),
            pl.BlockSpec((None, t_k, heads * (MLA_NOPE + MLA_V)), lambda b, h, i: (b, 0, h)),
            pl.BlockSpec((None, t_k, LANE), lambda b, h, i: (b, 0, 0)),
            pl.BlockSpec(memory_space=pl.ANY),
        ],
        out_specs=pl.BlockSpec((tq, ow), lambda b, h, i: (r_off + b * nq + i, c_off + h)),
        out_shape=jax.ShapeDtypeStruct(mix.shape, mix.dtype),
        input_output_aliases={4: 0},
        compiler_params=_params("parallel", "parallel", "arbitrary"),
        name="mla_attention",
    )(q, q, kv, kr, mix)


def _rope_tables(rot_dim):
    rows = DEC_SEQ // GRID_W
    row = jnp.repeat(jnp.arange(rows, dtype=F32), GRID_W)
    col = jnp.tile(jnp.arange(GRID_W, dtype=F32), rows)
    quarter = rot_dim // 4
    freqs = ROPE_THETA ** (-jnp.arange(quarter, dtype=F32) / quarter)
    ang = jnp.concatenate([row[:, None] * freqs, col[:, None] * freqs], axis=-1)
    cos, sin = jnp.cos(ang), jnp.sin(ang)
    zero = jnp.zeros_like(sin)
    c = jnp.repeat(cos, 2, axis=-1)
    se = jnp.stack([-sin, zero], axis=-1).reshape(DEC_SEQ, rot_dim)
    so = jnp.stack([zero, sin], axis=-1).reshape(DEC_SEQ, rot_dim)
    pad = ((0, 0), (0, LANE - rot_dim))
    return tuple(jnp.pad(t, pad) for t in (c, se, so))


def _block_diag_gates(lru_wr, lru_wi):
    per = LRU_CHUNK // LRU_BLOCK
    nck = LRU_WIDTH // LRU_CHUNK
    eye = jnp.eye(per, dtype=F32)

    def bd(w):
        w = w.reshape(DEPTH, 2, nck, per, LRU_BLOCK, LRU_BLOCK)
        w = jnp.einsum("dzcakj,ab->dzcakbj", w, eye)
        return w.reshape(DEPTH, 2, nck, LRU_CHUNK, LRU_CHUNK)

    return jnp.concatenate([bd(lru_wr), bd(lru_wi)], axis=-1).astype(BF16)


def _permute_w_uq(w_uq):
    w = w_uq.reshape(DEPTH, MLA_Q_RANK, MLA_HEADS, MLA_NOPE + MLA_ROPE)
    nope = w[..., :MLA_NOPE].reshape(DEPTH, MLA_Q_RANK, MLA_HEADS * MLA_NOPE)
    rope = jnp.pad(w[..., MLA_NOPE:], ((0, 0), (0, 0), (0, 0), (0, LANE - MLA_ROPE)))
    rope = rope.reshape(DEPTH, MLA_Q_RANK, MLA_HEADS * LANE)
    return jnp.concatenate([nope, rope], axis=-1).astype(BF16)


def kernel(x_prompt, x_sample, state_lru, cache_gqa_k, cache_gqa_v, cache_mla_ckv, cache_mla_krope, c,
           c_ctx, w_mod, b_mod, norm_mix_g, norm_ffn_g, w_in, conv_w, conv_b, lru_wr, lru_br, lru_wi, lru_bi,
           lru_lam, gqa_qn, gqa_kn, mla_qn, mla_kvn, w_uq, w_ukv, w_out, w_gate, w_up, w_down, norm_f):
    x = (x_prompt.reshape(N_CTX, D_MODEL), x_sample.reshape(N_LAT, D_MODEL))

    cond8 = jnp.zeros((8, D_MODEL), F32).at[0].set(c_ctx).at[1:1 + DEC_BATCH].set(c)
    mod = modulation_all(cond8, w_mod, b_mod).reshape(DEPTH, 8, 6, 1, D_MODEL)

    w_kr16 = jnp.pad(w_in[:, :, IN_MAIN:], ((0, 0), (0, 0), (0, LANE - MLA_ROPE))).astype(BF16)
    w_in_t = jnp.swapaxes(w_in, 1, 2)
    w_uq16 = _permute_w_uq(w_uq)
    w_gates = _block_diag_gates(lru_wr, lru_wi)
    lru_prm = jnp.stack([lru_br, lru_bi, lru_lam], axis=2)
    tables_b = _rope_tables(HEAD_DIM)
    tables_c = _rope_tables(MLA_ROPE)
    h0_ctx = jnp.zeros((BATCH, 2, LRU_WIDTH), F32)
    pad_kr = ((0, 0), (0, 0), (0, LANE - MLA_ROPE))

    mix = jnp.zeros((N_TOK, D_MIX), BF16)

    st_lru, st_k, st_v, st_ckv, st_kr = [], [], [], [], []
    for l in range(DEPTH):
        sh1, sc1, g1, sh2, sc2, g2 = (mod[l, :, s] for s in range(6))

        h = norm_mod(x, norm_mix_g[l], sc1, sh1)
        p = matmul_nt(h, w_in_t, l, tm=1024, tn=512, out_dtype=F32, n=IN_MAIN)
        kr = matmul(h, w_kr16, l, tm=1024, tn=LANE, out_dtype=F32)

        mix, s_lru = lru_mixer(p, conv_w, conv_b, w_gates, lru_prm, h0_ctx, mix, l, latent=False)
        mix, _ = lru_mixer(p, conv_w, conv_b, w_gates, lru_prm, state_lru[:, l], mix, l, latent=True)

        (q_c, k_c, v_c, cq_c, ckv_c, kr_c, kf_c, ckvf_c) = prep(
            p, kr, gqa_qn[l], gqa_kn[l], mla_qn[l], mla_kvn[l], latent=False)
        (q_l, k_l, v_l, cq_l, ckv_l, kr_l) = prep(
            p, kr, gqa_qn[l], gqa_kn[l], mla_qn[l], mla_kvn[l], latent=True,
            tables_b=tables_b, tables_c=tables_c)

        mix = gqa_attention(q_c, k_c.reshape(BATCH, SEQ, KV_WIDTH), v_c.reshape(BATCH, SEQ, KV_WIDTH), mix, 0,
                            n_batch=BATCH, t_q=SEQ, t_k=SEQ, tq=SEQ, groups=GQA_KV_HEADS, sub=SEQ)
        k_all = jnp.concatenate([cache_gqa_k[:, l].reshape(DEC_BATCH, PAST_LEN, KV_WIDTH).astype(BF16),
                                 k_l.reshape(DEC_BATCH, DEC_SEQ, KV_WIDTH)], axis=1)
        v_all = jnp.concatenate([cache_gqa_v[:, l].reshape(DEC_BATCH, PAST_LEN, KV_WIDTH).astype(BF16),
                                 v_l.reshape(DEC_BATCH, DEC_SEQ, KV_WIDTH)], axis=1)
        mix = gqa_attention(q_l, k_all, v_all, mix, N_CTX, n_batch=DEC_BATCH, t_q=DEC_SEQ, t_k=LAT_KEYS,
                            tq=512, groups=1, sub=128)

        qm_c = matmul(cq_c, w_uq16, l, tm=1024, tn=512, out_dtype=BF16)
        qm_l = matmul_rope_tail(cq_l, w_uq16, l, tables_c, tm=1024, tn=512, first_rope_tile=3)
        kv_c = matmul(ckv_c, w_ukv, l, tm=1024, tn=1024, out_dtype=BF16)
        ckv_all = jnp.concatenate([cache_mla_ckv[:, l].astype(BF16),
                                   ckv_l.reshape(DEC_BATCH, DEC_SEQ, MLA_KV_RANK)], axis=1)
        kv_l = matmul(ckv_all.reshape(DEC_BATCH * LAT_KEYS, MLA_KV_RANK), w_ukv, l,
                      tm=1024, tn=1024, out_dtype=BF16)
        kr_all = jnp.concatenate([jnp.pad(cache_mla_krope[:, l], pad_kr).astype(BF16),
                                  kr_l.reshape(DEC_BATCH, DEC_SEQ, LANE)], axis=1)
        mix = mla_attention(qm_c, kv_c.reshape(BATCH, SEQ, -1), kr_c.reshape(BATCH, SEQ, LANE), mix, 0,
                            n_batch=BATCH, t_q=SEQ, t_k=SEQ, tq=SEQ, heads=MLA_HEADS, sub=SEQ)
        mix = mla_attention(qm_l, kv_l.reshape(DEC_BATCH, LAT_KEYS, -1), kr_all, mix, N_CTX,
                            n_batch=DEC_BATCH, t_q=DEC_SEQ, t_k=LAT_KEYS, tq=1024, heads=1, sub=256)

        x = matmul_gated_residual(mix, w_out, l, x, g1, tm=1024, tn=512, mix_cols=True)

        h = norm_mod(x, norm_ffn_g[l], sc2, sh2)
        ff, w_down16 = swiglu_up(h, w_gate, w_up, w_down, l, tm=2048, tn=256)
        x = matmul_gated_residual(ff, w_down16, 0, x, g2, tm=512, tn=512)

        st_lru.append(s_lru)
        st_k.append(kf_c.reshape(BATCH, SEQ, GQA_KV_HEADS, HEAD_DIM))
        st_v.append(p[:N_CTX, 4096:4096 + KV_WIDTH].reshape(BATCH, SEQ, GQA_KV_HEADS, HEAD_DIM))
        st_ckv.append(ckvf_c.reshape(BATCH, SEQ, MLA_KV_RANK))
        st_kr.append(kr[:N_CTX, :MLA_ROPE].reshape(BATCH, SEQ, MLA_ROPE))

    y_ctx = final_norm(x, norm_f, 0, N_CTX)
    y_lat = final_norm(x, norm_f, N_CTX, N_LAT)
    return (y_ctx.reshape(BATCH, SEQ, D_MODEL), y_lat.reshape(DEC_BATCH, DEC_SEQ, D_MODEL),
            jnp.stack(st_lru, axis=1), jnp.stack(st_k, axis=1), jnp.stack(st_v, axis=1),
            jnp.stack(st_ckv, axis=1), jnp.stack(st_kr, axis=1))
```

```python
import functools

import jax
import jax.numpy as jnp
from jax import lax
from jax.experimental import pallas as pl
from jax.experimental.pallas import tpu as pltpu

F32 = jnp.float32
BF16 = jnp.bfloat16

D_MODEL = 4096
BATCH = 16
SEQ = 256
DEPTH = 4
DEC_BATCH = 2
DEC_SEQ = 2048
PAST_LEN = 512
GRID_W = 64
EPS = 1e-6
ROPE_THETA = 10000.0

LRU_WIDTH = 1024
LRU_BLOCKS = 16
LRU_BLOCK = 64
LRU_C = 8.0
HEAD_DIM = 128
GQA_HEADS = 12
GQA_KV_HEADS = 4
GQA_REP = GQA_HEADS // GQA_KV_HEADS
GQA_WIDTH = GQA_HEADS * HEAD_DIM
KV_WIDTH = GQA_KV_HEADS * HEAD_DIM
MLA_HEADS = 12
MLA_Q_RANK = 1024
MLA_KV_RANK = 512
MLA_NOPE = 128
MLA_ROPE = 64
MLA_V = 128
MLA_WIDTH = MLA_HEADS * MLA_V
D_MIX = LRU_WIDTH + GQA_WIDTH + MLA_WIDTH
MIX_B_COL = 0
MIX_C_COL = GQA_WIDTH
MIX_A_COL = GQA_WIDTH + MLA_WIDTH
D_FF = 11008
IN_MAIN = 6144
LANE = 128
N_CTX = BATCH * SEQ
N_LAT = DEC_BATCH * DEC_SEQ
N_TOK = N_CTX + N_LAT
LAT_KEYS = PAST_LEN + DEC_SEQ
VMEM_LIMIT = 56 * 1024 * 1024
LRU_CHUNK = 256
LRU_ROWS = 2048
ROW_TILE = 256
NORM_TILE = 512


def _params(*sem):
    return pltpu.CompilerParams(dimension_semantics=sem, vmem_limit_bytes=VMEM_LIMIT)


def _cond_of_tile(i, tm):
    row = i * tm
    return jnp.where(row < N_CTX, 0, 1 + (row - N_CTX) // DEC_SEQ)


def _silu(x):
    return x * jax.nn.sigmoid(x)


def _dot(a, b):
    return jnp.dot(a, b, preferred_element_type=F32)


def _dot_nt(a, b):
    return lax.dot_general(a, b, (((1,), (1,)), ((), ())), preferred_element_type=F32)


def _mod_kernel(c_ref, w_ref, b_ref, o_ref):
    s = _silu(c_ref[...]).astype(BF16)
    o_ref[...] = _dot(s, w_ref[...].astype(BF16)) + b_ref[...]


def modulation_all(cond8, w_mod, b_mod):
    tn = 512
    n = 6 * D_MODEL
    return pl.pallas_call(
        _mod_kernel,
        grid=(DEPTH, n // tn),
        in_specs=[
            pl.BlockSpec((8, D_MODEL), lambda l, j: (0, 0)),
            pl.BlockSpec((None, D_MODEL, tn), lambda l, j: (l, 0, j)),
            pl.BlockSpec((None, 1, tn), lambda l, j: (l, 0, j)),
        ],
        out_specs=pl.BlockSpec((None, 8, tn), lambda l, j: (l, 0, j)),
        out_shape=jax.ShapeDtypeStruct((DEPTH, 8, n), F32),
        compiler_params=_params("parallel", "parallel"),
        name="modulation",
    )(cond8, w_mod, b_mod.reshape(DEPTH, 1, n))


def _row_parts(x, tm, tn=None):
    def spec(row_fn, col_fn):
        if tn is None:
            return pl.BlockSpec((tm, D_MODEL), lambda i: (row_fn(i), 0))
        return pl.BlockSpec((tm, tn), lambda i, j: (row_fn(i), col_fn(i, j)))

    if not isinstance(x, tuple):
        return [x], [spec(lambda i: i, lambda i, j: j)]
    n_a = x[0].shape[0] // tm
    return list(x), [
        spec(lambda i: jnp.minimum(i, n_a - 1), lambda i, j: jnp.where(i < n_a, j, 0)),
        spec(lambda i: jnp.maximum(i - n_a, 0), lambda i, j: jnp.where(i < n_a, 0, j)),
    ]


def _read_rows(refs, tm):
    if len(refs) == 1:
        return refs[0][...]
    n_a = N_CTX // tm
    return jnp.where(pl.program_id(0) < n_a, refs[0][...], refs[1][...])


def _norm_mod_kernel(*refs):
    *x_refs, g_ref, sc_ref, sh_ref, o_ref = refs
    x = _read_rows(x_refs, o_ref.shape[0])
    y = x * lax.rsqrt(jnp.mean(x * x, axis=-1, keepdims=True) + EPS) * g_ref[...]
    o_ref[...] = (y * (1.0 + sc_ref[...]) + sh_ref[...]).astype(o_ref.dtype)


def norm_mod(x, g, sc, sh):
    tm = NORM_TILE
    x_args, x_specs = _row_parts(x, tm)
    return pl.pallas_call(
        _norm_mod_kernel,
        grid=(N_TOK // tm,),
        in_specs=x_specs + [
            pl.BlockSpec((1, D_MODEL), lambda i: (0, 0)),
            pl.BlockSpec((None, 1, D_MODEL), lambda i: (_cond_of_tile(i, tm), 0, 0)),
            pl.BlockSpec((None, 1, D_MODEL), lambda i: (_cond_of_tile(i, tm), 0, 0)),
        ],
        out_specs=pl.BlockSpec((tm, D_MODEL), lambda i: (i, 0)),
        out_shape=jax.ShapeDtypeStruct((N_TOK, D_MODEL), BF16),
        compiler_params=_params("parallel"),
        name="norm_mod",
    )(*x_args, g.reshape(1, D_MODEL), sc, sh)


def _final_norm_kernel(x_ref, g_ref, o_ref):
    x = x_ref[...]
    o_ref[...] = x * lax.rsqrt(jnp.mean(x * x, axis=-1, keepdims=True) + EPS) * g_ref[...]


def final_norm(x, g, row0, n_rows):
    tm = NORM_TILE
    off = row0 // tm
    return pl.pallas_call(
        _final_norm_kernel,
        grid=(n_rows // tm,),
        in_specs=[
            pl.BlockSpec((tm, D_MODEL), lambda i: (i + off, 0)),
            pl.BlockSpec((1, D_MODEL), lambda i: (0, 0)),
        ],
        out_specs=pl.BlockSpec((tm, D_MODEL), lambda i: (i, 0)),
        out_shape=jax.ShapeDtypeStruct((n_rows, D_MODEL), F32),
        compiler_params=_params("parallel"),
        name="final_norm",
    )(x, g.reshape(1, D_MODEL))


def _w16(w_ref):
    return w_ref[...].astype(BF16)


def _mm_kernel(x_ref, w_ref, o_ref):
    o_ref[...] = _dot(x_ref[...], _w16(w_ref)).astype(o_ref.dtype)


def matmul(x, w, layer, *, tm, tn, out_dtype, row0=0, m=None):
    k = x.shape[1]
    m = x.shape[0] if m is None else m
    n = w.shape[-1]
    r_off = row0 // tm
    return pl.pallas_call(
        _mm_kernel,
        grid=(m // tm, n // tn),
        in_specs=[
            pl.BlockSpec((tm, k), lambda i, j: (r_off + i, 0)),
            pl.BlockSpec((None, k, tn), lambda i, j: (layer, 0, j)),
        ],
        out_specs=pl.BlockSpec((tm, tn), lambda i, j: (i, j)),
        out_shape=jax.ShapeDtypeStruct((m, n), out_dtype),
        compiler_params=_params("parallel", "arbitrary"),
        name="matmul",
    )(x, w)


def _mm_nt_kernel(x_ref, w_ref, w2_ref, o_ref, o2_ref):
    x = x_ref[...]
    o_ref[...] = _dot_nt(x, _w16(w_ref)).astype(o_ref.dtype)

    @pl.when(pl.program_id(1) == 0)
    def _():
        o2_ref[...] = _dot(x, w2_ref[...])


def matmul_nt(x, w_t, w2, layer, *, tm, tn, n):
    m, k = x.shape
    n2 = w2.shape[-1]
    return pl.pallas_call(
        _mm_nt_kernel,
        grid=(m // tm, n // tn),
        in_specs=[
            pl.BlockSpec((tm, k), lambda i, j: (i, 0)),
            pl.BlockSpec((None, tn, k), lambda i, j: (layer, j, 0)),
            pl.BlockSpec((None, k, n2), lambda i, j: (layer, 0, 0)),
        ],
        out_specs=[
            pl.BlockSpec((tm, tn), lambda i, j: (i, j)),
            pl.BlockSpec((tm, n2), lambda i, j: (i, 0)),
        ],
        out_shape=[
            jax.ShapeDtypeStruct((m, n), F32),
            jax.ShapeDtypeStruct((m, n2), F32),
        ],
        compiler_params=_params("parallel", "arbitrary"),
        name="matmul_nt",
    )(x, w_t, w2)


def _mm_res_kernel(x_ref, w_ref, *refs, mix_cols):
    *r_refs, g_ref, o_ref = refs
    w = _w16(w_ref)
    if mix_cols:
        acc = (_dot(x_ref[:, :MIX_A_COL], w[LRU_WIDTH:]) + _dot(x_ref[:, MIX_A_COL:], w[:LRU_WIDTH]))
    else:
        acc = _dot(x_ref[...], w)
    o_ref[...] = _read_rows(r_refs, o_ref.shape[0]) + g_ref[...] * acc


def matmul_gated_residual(x, w, layer, res, gate, *, tm, tn, mix_cols=False):
    m, k = x.shape
    n = w.shape[-1]
    r_args, r_specs = _row_parts(res, tm, tn)
    return pl.pallas_call(
        functools.partial(_mm_res_kernel, mix_cols=mix_cols),
        grid=(m // tm, n // tn),
        in_specs=[
            pl.BlockSpec((tm, k), lambda i, j: (i, 0)),
            pl.BlockSpec((None, k, tn), lambda i, j: (layer, 0, j)),
        ] + r_specs + [
            pl.BlockSpec((None, 1, tn), lambda i, j: (_cond_of_tile(i, tm), 0, j)),
        ],
        out_specs=pl.BlockSpec((tm, tn), lambda i, j: (i, j)),
        out_shape=jax.ShapeDtypeStruct((m, n), F32),
        compiler_params=_params("parallel", "arbitrary"),
        name="matmul_gated_residual",
    )(x, w, *r_args, gate)


def _swiglu_kernel(x_ref, wg_ref, wu_ref, wd_ref, o_ref, wd16_ref):
    x = x_ref[...]
    g = _dot(x, _w16(wg_ref))
    u = _dot(x, _w16(wu_ref))
    o_ref[...] = (_silu(g) * u).astype(o_ref.dtype)
    wd16_ref[...] = wd_ref[...].astype(BF16)


def swiglu_up(x, w_gate, w_up, w_down, layer, *, tm, tn):
    m, k = x.shape
    n = w_gate.shape[-1]
    nj = n // tn
    steps = (m // tm) * nj
    kd, nd = w_down.shape[1:]
    slab = kd // steps
    assert slab * steps == kd and slab % 16 == 0
    return pl.pallas_call(
        _swiglu_kernel,
        grid=(m // tm, nj),
        in_specs=[
            pl.BlockSpec((tm, k), lambda i, j: (i, 0), pipeline_mode=pl.Buffered(1)),
            pl.BlockSpec((None, k, tn), lambda i, j: (layer, 0, j)),
            pl.BlockSpec((None, k, tn), lambda i, j: (layer, 0, j)),
            pl.BlockSpec((None, slab, nd), lambda i, j: (layer, i * nj + j, 0)),
        ],
        out_specs=[
            pl.BlockSpec((tm, tn), lambda i, j: (i, j)),
            pl.BlockSpec((None, slab, nd), lambda i, j: (0, i * nj + j, 0)),
        ],
        out_shape=[
            jax.ShapeDtypeStruct((m, n), BF16),
            jax.ShapeDtypeStruct((1, kd, nd), BF16),
        ],
        compiler_params=_params("arbitrary", "arbitrary"),
        name="swiglu_up",
    )(x, w_gate, w_up, w_down)


def _rope(x, c, se, so):
    nxt = pltpu.roll(x, LANE - 1, 1)
    prv = pltpu.roll(x, 1, 1)
    return x * c + nxt * se + prv * so


def _mm_rope_kernel(x_ref, w_ref, c_ref, se_ref, so_ref, o_ref, *, first_rope_tile):
    acc = _dot(x_ref[...], _w16(w_ref))
    j = pl.program_id(1)

    @pl.when(j < first_rope_tile)
    def _():
        o_ref[...] = acc.astype(o_ref.dtype)

    @pl.when(j >= first_rope_tile)
    def _():
        c, se, so = c_ref[...], se_ref[...], so_ref[...]
        for h in range(acc.shape[1] // LANE):
            sl = slice(h * LANE, (h + 1) * LANE)
            o_ref[:, sl] = _rope(acc[:, sl], c, se, so).astype(o_ref.dtype)


def matmul_rope_tail(x, w, layer, tables, *, tm, tn, first_rope_tile):
    m, k = x.shape
    n = w.shape[-1]
    nt = DEC_SEQ // tm
    tab = pl.BlockSpec((tm, LANE), lambda i, j: (i % nt, 0))
    return pl.pallas_call(
        functools.partial(_mm_rope_kernel, first_rope_tile=first_rope_tile),
        grid=(m // tm, n // tn),
        in_specs=[
            pl.BlockSpec((tm, k), lambda i, j: (i, 0)),
            pl.BlockSpec((None, k, tn), lambda i, j: (layer, 0, j)),
            tab, tab, tab,
        ],
        out_specs=pl.BlockSpec((tm, tn), lambda i, j: (i, j)),
        out_shape=jax.ShapeDtypeStruct((m, n), BF16),
        compiler_params=_params("parallel", "arbitrary"),
        name="matmul_rope_tail",
    )(x, w, *tables)


def _head_rmsnorm(x, g):
    return x * lax.rsqrt(jnp.mean(x * x, axis=-1, keepdims=True) + EPS) * g


def _prep_kernel(*refs, rope):
    if rope:
        (qk_ref, v_ref, c_ref, kr_ref, qn_ref, kn_ref, mqn_ref, mkvn_ref,
         cb_ref, seb_ref, sob_ref, cc_ref, sec_ref, soc_ref,
         _k_buf, _v_buf, _ckv_buf, _kr_buf,
         q_o, k_o, v_o, cq_o, ckv_o, kr_o) = refs
    else:
        (qk_ref, v_ref, c_ref, kr_ref, qn_ref, kn_ref, mqn_ref, mkvn_ref,
         q_o, k_o, v_o, cq_o, ckv_o, kr_o, kf_o, ckvf_o) = refs
    qn, kn = qn_ref[...], kn_ref[...]
    if rope:
        cb, seb, sob = cb_ref[...], seb_ref[...], sob_ref[...]
    for h in range(GQA_HEADS + GQA_KV_HEADS):
        sl = slice(h * LANE, (h + 1) * LANE)
        is_q = h < GQA_HEADS
        y = _head_rmsnorm(qk_ref[:, sl], qn if is_q else kn)
        osl = sl if is_q else slice((h - GQA_HEADS) * LANE, (h - GQA_HEADS + 1) * LANE)
        if not rope and not is_q:
            kf_o[:, osl] = y
        if rope:
            y = _rope(y, cb, seb, sob)
        (q_o if is_q else k_o)[:, osl] = y.astype(BF16)
    v_o[...] = v_ref[...].astype(BF16)
    cq = c_ref[:, :MLA_Q_RANK]
    cq_o[...] = _head_rmsnorm(cq, mqn_ref[...]).astype(BF16)
    ckv = _head_rmsnorm(c_ref[:, MLA_Q_RANK:], mkvn_ref[...])
    ckv_o[...] = ckv.astype(BF16)
    if not rope:
        ckvf_o[...] = ckv
    kr = kr_ref[...]
    if rope:
        kr = _rope(kr, cc_ref[...], sec_ref[...], soc_ref[...])
    kr_o[...] = kr.astype(BF16)


def prep(p, kr, gqa_qn, gqa_kn, mla_qn, mla_kvn, *, latent, tables_b=None, tables_c=None,
         layer=None, key_bufs=None):
    tm = ROW_TILE
    n_rows = N_LAT if latent else N_CTX
    off = (N_CTX // tm) if latent else 0
    nt = DEC_SEQ // tm
    row = lambda c: (lambda i: (i + off, c))
    in_specs = [
        pl.BlockSpec((tm, 2048), row(1)),
        pl.BlockSpec((tm, KV_WIDTH), row(8)),
        pl.BlockSpec((tm, 1536), row(3)),
        pl.BlockSpec((tm, LANE), row(0)),
        pl.BlockSpec((1, HEAD_DIM), lambda i: (0, 0)),
        pl.BlockSpec((1, HEAD_DIM), lambda i: (0, 0)),
        pl.BlockSpec((1, MLA_Q_RANK), lambda i: (0, 0)),
        pl.BlockSpec((1, MLA_KV_RANK), lambda i: (0, 0)),
    ]
    args = [p, p, p, kr, gqa_qn.reshape(1, -1), gqa_kn.reshape(1, -1),
            mla_qn.reshape(1, -1), mla_kvn.reshape(1, -1)]
    out = lambda w, dt: (pl.BlockSpec((tm, w), lambda i: (i, 0)), jax.ShapeDtypeStruct((n_rows, w), dt))
    outs = [out(GQA_WIDTH, BF16), out(KV_WIDTH, BF16), out(KV_WIDTH, BF16),
            out(MLA_Q_RANK, BF16), out(MLA_KV_RANK, BF16), out(LANE, BF16)]
    aliases = {}
    if latent:
        tab = pl.BlockSpec((tm, LANE), lambda i: (i % nt, 0))
        in_specs += [tab] * 6
        args += list(tables_b) + list(tables_c)
        blocks_per_req = LAT_KEYS // tm

        def key_rows(i):
            return ((layer * DEC_BATCH + i // nt) * blocks_per_req + PAST_LEN // tm + i % nt, 0)

        for n_out, buf in zip((1, 2, 4, 5), key_bufs):
            aliases[len(args)] = n_out
            in_specs.append(pl.BlockSpec(memory_space=pl.ANY))
            args.append(buf)
            outs[n_out] = (pl.BlockSpec((tm, buf.shape[1]), key_rows), jax.ShapeDtypeStruct(buf.shape, buf.dtype))
    else:
        outs += [out(KV_WIDTH, F32), out(MLA_KV_RANK, F32)]
    return pl.pallas_call(
        functools.partial(_prep_kernel, rope=latent),
        grid=(n_rows // tm,),
        in_specs=in_specs,
        out_specs=[o[0] for o in outs],
        out_shape=[o[1] for o in outs],
        input_output_aliases=aliases,
        compiler_params=_params("parallel"),
        name="prep_latent" if latent else "prep_context",
    )(*args)


def _gelu_tanh(x):
    return 0.5 * x * (1.0 + jnp.tanh(0.7978845608028654 * (x + 0.044715 * (x * x * x))))


def _softplus(z):
    return jnp.maximum(z, 0.0) + jnp.log1p(jnp.exp(-jnp.abs(z)))


def _lru_kernel(xa_ref, ga_ref, cw_ref, cb_ref, w_ref, p_ref, h0_ref, mix_ref, o_ref, st_ref,
                a_scr, b_scr, hf_scr, hb_scr, *, t_sub, tiles_per_iter):
    del mix_ref
    rows, c = xa_ref.shape
    n_sub = rows // t_sub
    x = xa_ref[...]
    tpos = lax.broadcasted_iota(jnp.int32, x.shape, 0) % t_sub
    cw = cw_ref[...]
    xm2 = jnp.where(tpos >= 2, pltpu.roll(x, 2, 0), 0.0)
    xm1 = jnp.where(tpos >= 1, pltpu.roll(x, 1, 0), 0.0)
    xp1 = jnp.where(tpos < t_sub - 1, pltpu.roll(x, rows - 1, 0), 0.0)
    xc = cb_ref[...] + xm2 * cw[0:1] + xm1 * cw[1:2] + x * cw[2:3] + xp1 * cw[3:4]
    xc16 = xc.astype(BF16)

    for d in range(2):
        g = _dot(xc16, w_ref[d])
        prm = p_ref[d]
        r = jax.nn.sigmoid(g[:, :c] + prm[0:1])
        ig = jax.nn.sigmoid(g[:, c:] + prm[1:2])
        log_a = (-LRU_C) * r * _softplus(-prm[2:3])
        a = jnp.exp(log_a)
        a_scr[d] = a
        b_scr[d] = jnp.sqrt(-jnp.tanh(log_a) * (a * a + 1.0)) * (ig * xc)

    sub_row = lax.broadcasted_iota(jnp.int32, (8, c), 0)
    keep_f = [sub_row >= s for s in (1, 2, 4)]
    keep_b = [sub_row < 8 - s for s in (1, 2, 4)]

    def tile_scan(a, b, carry, reverse):
        for n, s in enumerate((1, 2, 4)):
            keep, shift = (keep_b[n], 8 - s) if reverse else (keep_f[n], s)
            a_prev = jnp.where(keep, pltpu.roll(a, shift, 0), 1.0)
            b_prev = jnp.where(keep, pltpu.roll(b, shift, 0), 0.0)
            b = a * b_prev + b
            a = a * a_prev
        h = a * carry + b
        last = h[0:1] if reverse else h[7:8]
        return h, jnp.broadcast_to(last, h.shape)

    def body(j, hs):
        hs = list(hs)
        for u in range(tiles_per_iter):
            off = pl.multiple_of((j * tiles_per_iter + u) * 8, 8)
            for q in range(n_sub):
                idx_f = pl.ds(q * t_sub + off, 8)
                idx_b = pl.ds(q * t_sub + (t_sub - 8) - off, 8)
                hf, hs[2 * q] = tile_scan(a_scr[0, idx_f, :], b_scr[0, idx_f, :], hs[2 * q], False)
                hb, hs[2 * q + 1] = tile_scan(a_scr[1, idx_b, :], b_scr[1, idx_b, :], hs[2 * q + 1], True)
                hf_scr[idx_f, :] = hf
                hb_scr[idx_b, :] = hb
        return tuple(hs)

    init = tuple(jnp.broadcast_to(h0_ref[q, d:d + 1, :], (8, c)) for q in range(n_sub) for d in range(2))
    hs = lax.fori_loop(0, t_sub // (8 * tiles_per_iter), body, init)
    for q in range(n_sub):
        st_ref[q, 0:1, :] = hs[2 * q][0:1]
        st_ref[q, 1:2, :] = hs[2 * q + 1][0:1]
    o_ref[...] = ((hf_scr[...] + hb_scr[...]) * _gelu_tanh(ga_ref[...])).astype(o_ref.dtype)


def lru_mixer(p, conv_w, conv_b, w_gates, lru_prm, h0, mix, layer, *, latent):
    c = LRU_CHUNK
    rows = LRU_ROWS
    t_sub = DEC_SEQ if latent else SEQ
    n_sub = rows // t_sub
    n_rows = N_LAT if latent else N_CTX
    off = (N_CTX // rows) if latent else 0
    nck = LRU_WIDTH // c
    return pl.pallas_call(
        functools.partial(_lru_kernel, t_sub=t_sub, tiles_per_iter=4 if latent else 1),
        grid=(n_rows // rows, nck),
        in_specs=[
            pl.BlockSpec((rows, c), lambda s, k: (s + off, k)),
            pl.BlockSpec((rows, c), lambda s, k: (s + off, nck + k)),
            pl.BlockSpec((None, 4, c), lambda s, k: (layer, 0, k)),
            pl.BlockSpec((None, 1, c), lambda s, k: (layer, 0, k)),
            pl.BlockSpec((None, 2, None, c, 2 * c), lambda s, k: (layer, 0, k, 0, 0)),
            pl.BlockSpec((None, 2, 3, c), lambda s, k: (layer, 0, 0, k)),
            pl.BlockSpec((n_sub, 2, c), lambda s, k: (s, 0, k)),
            pl.BlockSpec(memory_space=pl.ANY),
        ],
        out_specs=[
            pl.BlockSpec((rows, c), lambda s, k: (s + off, MIX_A_COL // c + k)),
            pl.BlockSpec((n_sub, 2, c), lambda s, k: (s, 0, k)),
        ],
        out_shape=[
            jax.ShapeDtypeStruct(mix.shape, mix.dtype),
            jax.ShapeDtypeStruct((n_rows // t_sub, 2, LRU_WIDTH), F32),
        ],
        input_output_aliases={7: 0},
        scratch_shapes=[pltpu.VMEM((2, rows, c), F32)] * 2 + [pltpu.VMEM((rows, c), F32)] * 2,
        compiler_params=_params("parallel", "parallel"),
        name="lru_latent" if latent else "lru_context",
    )(p, p, conv_w, conv_b.reshape(DEPTH, 1, LRU_WIDTH), w_gates, lru_prm, h0, mix)


LOG2E = 1.4426950408889634


def _softmax_pv(s, v, scale):
    t = s * (scale * LOG2E)
    m = jnp.max(t, axis=-1, keepdims=True)
    p = jnp.exp2(t - m).astype(BF16)
    oa = _dot(p, jnp.concatenate([v, jnp.ones_like(v)], axis=1))
    return oa[:, :LANE] / oa[:, LANE:]


def _gqa_kernel(q_ref, k_ref, v_ref, mix_ref, o_ref, *, scale, groups, sub):
    del mix_ref
    tq = q_ref.shape[0]
    for g in range(groups):
        kv_sl = slice(g * LANE, (g + 1) * LANE)
        k, v = k_ref[:, kv_sl], v_ref[:, kv_sl]
        for h in range(g * GQA_REP, (g + 1) * GQA_REP):
            sl = slice(h * LANE, (h + 1) * LANE)
            for r0 in range(0, tq, sub):
                s = _dot_nt(q_ref[r0:r0 + sub, sl], k)
                o_ref[r0:r0 + sub, sl] = _softmax_pv(s, v, scale).astype(o_ref.dtype)


def gqa_attention(q, k, v, mix, row0, *, n_batch, t_q, t_k, tq, groups, sub, kv_b0=0):
    nq = t_q // tq
    qw = groups * GQA_REP * HEAD_DIM
    kw = groups * HEAD_DIM
    r_off, c_off = row0 // tq, MIX_B_COL // qw
    return pl.pallas_call(
        functools.partial(_gqa_kernel, scale=HEAD_DIM ** -0.5, groups=groups, sub=sub),
        grid=(n_batch, GQA_KV_HEADS // groups, nq),
        in_specs=[
            pl.BlockSpec((tq, qw), lambda b, g, i: (b * nq + i, g)),
            pl.BlockSpec((None, t_k, kw), lambda b, g, i: (kv_b0 + b, 0, g)),
            pl.BlockSpec((None, t_k, kw), lambda b, g, i: (kv_b0 + b, 0, g)),
            pl.BlockSpec(memory_space=pl.ANY),
        ],
        out_specs=pl.BlockSpec((tq, qw), lambda b, g, i: (r_off + b * nq + i, c_off + g)),
        out_shape=jax.ShapeDtypeStruct(mix.shape, mix.dtype),
        input_output_aliases={3: 0},
        compiler_params=_params("parallel", "parallel", "arbitrary"),
        name="gqa_attention",
    )(q, k, v, mix)


def _mla_kernel(qn_ref, qr_ref, kv_ref, kr_ref, mix_ref, o_ref, *, scale, heads, sub):
    del mix_ref
    tq = qn_ref.shape[0]
    kr = kr_ref[...]
    for h in range(heads):
        sl = slice(h * LANE, (h + 1) * LANE)
        k0 = h * (MLA_NOPE + MLA_V)
        k = jnp.concatenate([kv_ref[:, k0:k0 + MLA_NOPE], kr], axis=1)
        v = kv_ref[:, k0 + MLA_NOPE:k0 + MLA_NOPE + MLA_V]
        for r0 in range(0, tq, sub):
            q = jnp.concatenate([qn_ref[r0:r0 + sub, sl], qr_ref[r0:r0 + sub, sl]], axis=1)
            o_ref[r0:r0 + sub, sl] = _softmax_pv(_dot_nt(q, k), v, scale).astype(o_ref.dtype)


def mla_attention(q, kv, kr, mix, row0, *, n_batch, t_q, t_k, tq, heads, sub, kr_b0=0):
    nq = t_q // tq
    nh = MLA_HEADS // heads
    ow = heads * MLA_V
    r_off, c_off = row0 // tq, MIX_C_COL // ow
    return pl.pallas_call(
        functools.partial(_mla_kernel, scale=(MLA_NOPE + MLA_ROPE) ** -0.5, heads=heads, sub=sub),
        grid=(n_batch, nh, nq),
        in_specs=[
            pl.BlockSpec((tq, heads * MLA_NOPE), lambda b, h, i: (b * nq + i, h)),
            pl.BlockSpec((tq, heads * LANE), lambda b, h, i: (b * nq + i, nh + h)),
            pl.BlockSpec((None, t_k, heads * (MLA_NOPE + MLA_V)), lambda b, h, i: (b, 0, h)),
            pl.BlockSpec((None, t_k, LANE), lambda b, h, i: (kr_b0 + b, 0, 0)),
            pl.BlockSpec(memory_space=pl.ANY),
        ],
        out_specs=pl.BlockSpec((tq, ow), lambda b, h, i: (r_off + b * nq + i, c_off + h)),
        out_shape=jax.ShapeDtypeStruct(mix.shape, mix.dtype),
        input_output_aliases={4: 0},
        compiler_params=_params("parallel", "parallel", "arbitrary"),
        name="mla_attention",
    )(q, q, kv, kr, mix)


def _rope_tables(rot_dim):
    rows = DEC_SEQ // GRID_W
    row = jnp.repeat(jnp.arange(rows, dtype=F32), GRID_W)
    col = jnp.tile(jnp.arange(GRID_W, dtype=F32), rows)
    quarter = rot_dim // 4
    freqs = ROPE_THETA ** (-jnp.arange(quarter, dtype=F32) / quarter)
    ang = jnp.concatenate([row[:, None] * freqs, col[:, None] * freqs], axis=-1)
    cos, sin = jnp.cos(ang), jnp.sin(ang)
    zero = jnp.zeros_like(sin)
    c = jnp.repeat(cos, 2, axis=-1)
    se = jnp.stack([-sin, zero], axis=-1).reshape(DEC_SEQ, rot_dim)
    so = jnp.stack([zero, sin], axis=-1).reshape(DEC_SEQ, rot_dim)
    pad = ((0, 0), (0, LANE - rot_dim))
    return tuple(jnp.pad(t, pad) for t in (c, se, so))


def _block_diag_gates(lru_wr, lru_wi):
    per = LRU_CHUNK // LRU_BLOCK
    nck = LRU_WIDTH // LRU_CHUNK
    eye = jnp.eye(per, dtype=F32)

    def bd(w):
        w = w.reshape(DEPTH, 2, nck, per, LRU_BLOCK, LRU_BLOCK)
        w = jnp.einsum("dzcakj,ab->dzcakbj", w, eye)
        return w.reshape(DEPTH, 2, nck, LRU_CHUNK, LRU_CHUNK)

    return jnp.concatenate([bd(lru_wr), bd(lru_wi)], axis=-1).astype(BF16)


def _key_buffer(cache, width):
    c = jnp.swapaxes(cache, 0, 1).astype(BF16)
    c = jnp.pad(c, ((0, 0), (0, 0), (0, DEC_SEQ), (0, width - cache.shape[-1])))
    return c.reshape(DEPTH * DEC_BATCH * LAT_KEYS, width)


def _permute_w_uq(w_uq):
    w = w_uq.reshape(DEPTH, MLA_Q_RANK, MLA_HEADS, MLA_NOPE + MLA_ROPE)
    nope = w[..., :MLA_NOPE].reshape(DEPTH, MLA_Q_RANK, MLA_HEADS * MLA_NOPE)
    rope = jnp.pad(w[..., MLA_NOPE:], ((0, 0), (0, 0), (0, 0), (0, LANE - MLA_ROPE)))
    rope = rope.reshape(DEPTH, MLA_Q_RANK, MLA_HEADS * LANE)
    return jnp.concatenate([nope, rope], axis=-1).astype(BF16)


def kernel(x_prompt, x_sample, state_lru, cache_gqa_k, cache_gqa_v, cache_mla_ckv, cache_mla_krope, c,
           c_ctx, w_mod, b_mod, norm_mix_g, norm_ffn_g, w_in, conv_w, conv_b, lru_wr, lru_br, lru_wi, lru_bi,
           lru_lam, gqa_qn, gqa_kn, mla_qn, mla_kvn, w_uq, w_ukv, w_out, w_gate, w_up, w_down, norm_f):
    x = (x_prompt.reshape(N_CTX, D_MODEL), x_sample.reshape(N_LAT, D_MODEL))

    cond8 = jnp.zeros((8, D_MODEL), F32).at[0].set(c_ctx).at[1:1 + DEC_BATCH].set(c)
    mod = modulation_all(cond8, w_mod, b_mod).reshape(DEPTH, 8, 6, 1, D_MODEL)

    w_kr16 = jnp.pad(w_in[:, :, IN_MAIN:], ((0, 0), (0, 0), (0, LANE - MLA_ROPE))).astype(BF16)
    w_in_t = jnp.swapaxes(w_in, 1, 2)
    w_uq16 = _permute_w_uq(w_uq)
    w_gates = _block_diag_gates(lru_wr, lru_wi)
    lru_prm = jnp.stack([lru_br, lru_bi, lru_lam], axis=2)
    tables_b = _rope_tables(HEAD_DIM)
    tables_c = _rope_tables(MLA_ROPE)
    h0_ctx = jnp.zeros((BATCH, 2, LRU_WIDTH), F32)
    key_bufs = (_key_buffer(cache_gqa_k.reshape(DEC_BATCH, DEPTH, PAST_LEN, KV_WIDTH), KV_WIDTH),
                _key_buffer(cache_gqa_v.reshape(DEC_BATCH, DEPTH, PAST_LEN, KV_WIDTH), KV_WIDTH),
                _key_buffer(cache_mla_ckv, MLA_KV_RANK),
                _key_buffer(cache_mla_krope, LANE))

    mix = jnp.zeros((N_TOK, D_MIX), BF16)

    st_lru, st_k, st_v, st_ckv, st_kr = [], [], [], [], []
    for l in range(DEPTH):
        sh1, sc1, g1, sh2, sc2, g2 = (mod[l, :, s] for s in range(6))

        h = norm_mod(x, norm_mix_g[l], sc1, sh1)
        p, kr = matmul_nt(h, w_in_t, w_kr16, l, tm=1024, tn=512, n=IN_MAIN)

        mix, s_lru = lru_mixer(p, conv_w, conv_b, w_gates, lru_prm, h0_ctx, mix, l, latent=False)
        mix, _ = lru_mixer(p, conv_w, conv_b, w_gates, lru_prm, state_lru[:, l], mix, l, latent=True)

        (q_c, k_c, v_c, cq_c, ckv_c, kr_c, kf_c, ckvf_c) = prep(
            p, kr, gqa_qn[l], gqa_kn[l], mla_qn[l], mla_kvn[l], latent=False)
        q_l, k_buf, v_buf, cq_l, ckv_buf, kr_buf = prep(
            p, kr, gqa_qn[l], gqa_kn[l], mla_qn[l], mla_kvn[l], latent=True,
            tables_b=tables_b, tables_c=tables_c, layer=l, key_bufs=key_bufs)
        key_bufs = (k_buf, v_buf, ckv_buf, kr_buf)
        req0 = l * DEC_BATCH

        mix = gqa_attention(q_c, k_c.reshape(BATCH, SEQ, KV_WIDTH), v_c.reshape(BATCH, SEQ, KV_WIDTH), mix, 0,
                            n_batch=BATCH, t_q=SEQ, t_k=SEQ, tq=SEQ, groups=GQA_KV_HEADS, sub=SEQ)
        mix = gqa_attention(q_l, k_buf.reshape(-1, LAT_KEYS, KV_WIDTH), v_buf.reshape(-1, LAT_KEYS, KV_WIDTH),
                            mix, N_CTX, n_batch=DEC_BATCH, t_q=DEC_SEQ, t_k=LAT_KEYS, tq=512, groups=1, sub=128,
                            kv_b0=req0)

        qm_c = matmul(cq_c, w_uq16, l, tm=2048, tn=1024, out_dtype=BF16)
        qm_l = matmul_rope_tail(cq_l, w_uq16, l, tables_c, tm=1024, tn=MLA_WIDTH, first_rope_tile=1)
        kv_c = matmul(ckv_c, w_ukv, l, tm=2048, tn=1024, out_dtype=BF16)
        kv_l = matmul(ckv_buf, w_ukv, l, tm=LAT_KEYS, tn=1024, out_dtype=BF16,
                      row0=req0 * LAT_KEYS, m=DEC_BATCH * LAT_KEYS)
        mix = mla_attention(qm_c, kv_c.reshape(BATCH, SEQ, -1), kr_c.reshape(BATCH, SEQ, LANE), mix, 0,
                            n_batch=BATCH, t_q=SEQ, t_k=SEQ, tq=SEQ, heads=MLA_HEADS, sub=SEQ)
        mix = mla_attention(qm_l, kv_l.reshape(DEC_BATCH, LAT_KEYS, -1), kr_buf.reshape(-1, LAT_KEYS, LANE),
                            mix, N_CTX, n_batch=DEC_BATCH, t_q=DEC_SEQ, t_k=LAT_KEYS, tq=1024, heads=1, sub=256,
                            kr_b0=req0)

        x = matmul_gated_residual(mix, w_out, l, x, g1, tm=1024, tn=512, mix_cols=True)

        h = norm_mod(x, norm_ffn_g[l], sc2, sh2)
        ff, w_down16 = swiglu_up(h, w_gate, w_up, w_down, l, tm=2048, tn=256)
        x = matmul_gated_residual(ff, w_down16, 0, x, g2, tm=512, tn=512)

        st_lru.append(s_lru)
        st_k.append(kf_c.reshape(BATCH, SEQ, GQA_KV_HEADS, HEAD_DIM))
        st_v.append(p[:N_CTX, 4096:4096 + KV_WIDTH].reshape(BATCH, SEQ, GQA_KV_HEADS, HEAD_DIM))
        st_ckv.append(ckvf_c.reshape(BATCH, SEQ, MLA_KV_RANK))
        st_kr.append(kr[:N_CTX, :MLA_ROPE].reshape(BATCH, SEQ, MLA_ROPE))

    y_ctx = final_norm(x, norm_f, 0, N_CTX)
    y_lat = final_norm(x, norm_f, N_CTX, N_LAT)
    return (y_ctx.reshape(BATCH, SEQ, D_MODEL), y_lat.reshape(DEC_BATCH, DEC_SEQ, D_MODEL),
            jnp.stack(st_lru, axis=1), jnp.stack(st_k, axis=1), jnp.stack(st_v, axis=1),
            jnp.stack(st_ckv, axis=1), jnp.stack(st_kr, axis=1))
```

```python
import functools

import jax
import jax.numpy as jnp
from jax import lax
from jax.experimental import pallas as pl
from jax.experimental.pallas import tpu as pltpu

F32 = jnp.float32
BF16 = jnp.bfloat16

D_MODEL = 4096
BATCH = 16
SEQ = 256
DEPTH = 4
DEC_BATCH = 2
DEC_SEQ = 2048
PAST_LEN = 512
GRID_W = 64
EPS = 1e-6
ROPE_THETA = 10000.0

LRU_WIDTH = 1024
LRU_BLOCKS = 16
LRU_BLOCK = 64
LRU_C = 8.0
HEAD_DIM = 128
GQA_HEADS = 12
GQA_KV_HEADS = 4
GQA_REP = GQA_HEADS // GQA_KV_HEADS
GQA_WIDTH = GQA_HEADS * HEAD_DIM
KV_WIDTH = GQA_KV_HEADS * HEAD_DIM
MLA_HEADS = 12
MLA_Q_RANK = 1024
MLA_KV_RANK = 512
MLA_NOPE = 128
MLA_ROPE = 64
MLA_V = 128
MLA_WIDTH = MLA_HEADS * MLA_V
D_MIX = LRU_WIDTH + GQA_WIDTH + MLA_WIDTH
MIX_B_COL = 0
MIX_C_COL = GQA_WIDTH
MIX_A_COL = GQA_WIDTH + MLA_WIDTH
D_FF = 11008
IN_MAIN = 6144
LANE = 128
N_CTX = BATCH * SEQ
N_LAT = DEC_BATCH * DEC_SEQ
N_TOK = N_CTX + N_LAT
LAT_KEYS = PAST_LEN + DEC_SEQ
VMEM_LIMIT = 56 * 1024 * 1024
LRU_CHUNK = 256
LRU_ROWS = 2048
ROW_TILE = 256
NORM_TILE = 512


def _params(*sem):
    return pltpu.CompilerParams(dimension_semantics=sem, vmem_limit_bytes=VMEM_LIMIT)


def _cond_of_tile(i, tm):
    row = i * tm
    return jnp.where(row < N_CTX, 0, 1 + (row - N_CTX) // DEC_SEQ)


def _silu(x):
    return x * jax.nn.sigmoid(x)


def _dot(a, b):
    return jnp.dot(a, b, preferred_element_type=F32)


def _dot_nt(a, b):
    return lax.dot_general(a, b, (((1,), (1,)), ((), ())), preferred_element_type=F32)


N_MOD = 6 * D_MODEL
MOD_SLAB = 256


def _mod_tile(c_ref, w_ref, b_ref):
    return _dot(_silu(c_ref[...]).astype(BF16), _w16(w_ref)) + b_ref[...]


def _mod_kernel(c_ref, w_ref, b_ref, o_ref):
    o_ref[...] = _mod_tile(c_ref, w_ref, b_ref)


def modulation(cond8, w_mod, b_mod3, layer):
    tn = 512
    return pl.pallas_call(
        _mod_kernel,
        grid=(N_MOD // tn,),
        in_specs=[
            pl.BlockSpec((8, D_MODEL), lambda j: (0, 0)),
            pl.BlockSpec((None, D_MODEL, tn), lambda j: (layer, 0, j)),
            pl.BlockSpec((None, 1, tn), lambda j: (layer, 0, j)),
        ],
        out_specs=pl.BlockSpec((8, tn), lambda j: (0, j)),
        out_shape=jax.ShapeDtypeStruct((8, N_MOD), F32),
        compiler_params=_params("parallel"),
        name="modulation",
    )(cond8, w_mod, b_mod3)


def _row_parts(x, tm, tn=None):
    def spec(row_fn, col_fn):
        if tn is None:
            return pl.BlockSpec((tm, D_MODEL), lambda i: (row_fn(i), 0))
        return pl.BlockSpec((tm, tn), lambda i, j: (row_fn(i), col_fn(i, j)))

    if not isinstance(x, tuple):
        return [x], [spec(lambda i: i, lambda i, j: j)]
    n_a = x[0].shape[0] // tm
    return list(x), [
        spec(lambda i: jnp.minimum(i, n_a - 1), lambda i, j: jnp.where(i < n_a, j, 0)),
        spec(lambda i: jnp.maximum(i - n_a, 0), lambda i, j: jnp.where(i < n_a, 0, j)),
    ]


def _read_rows(refs, tm):
    if len(refs) == 1:
        return refs[0][...]
    n_a = N_CTX // tm
    return jnp.where(pl.program_id(0) < n_a, refs[0][...], refs[1][...])


def _norm_mod_kernel(*refs):
    *x_refs, g_ref, sc_ref, sh_ref, o_ref = refs
    x = _read_rows(x_refs, o_ref.shape[0])
    y = x * lax.rsqrt(jnp.mean(x * x, axis=-1, keepdims=True) + EPS) * g_ref[...]
    o_ref[...] = (y * (1.0 + sc_ref[...]) + sh_ref[...]).astype(o_ref.dtype)


def norm_mod(x, g, sc, sh):
    tm = NORM_TILE
    x_args, x_specs = _row_parts(x, tm)
    return pl.pallas_call(
        _norm_mod_kernel,
        grid=(N_TOK // tm,),
        in_specs=x_specs + [
            pl.BlockSpec((1, D_MODEL), lambda i: (0, 0)),
            pl.BlockSpec((None, 1, D_MODEL), lambda i: (_cond_of_tile(i, tm), 0, 0)),
            pl.BlockSpec((None, 1, D_MODEL), lambda i: (_cond_of_tile(i, tm), 0, 0)),
        ],
        out_specs=pl.BlockSpec((tm, D_MODEL), lambda i: (i, 0)),
        out_shape=jax.ShapeDtypeStruct((N_TOK, D_MODEL), BF16),
        compiler_params=_params("parallel"),
        name="norm_mod",
    )(*x_args, g.reshape(1, D_MODEL), sc, sh)


def _final_norm_kernel(x_ref, g_ref, o_ref):
    x = x_ref[...]
    o_ref[...] = x * lax.rsqrt(jnp.mean(x * x, axis=-1, keepdims=True) + EPS) * g_ref[...]


def final_norm(x, g, row0, n_rows):
    tm = NORM_TILE
    off = row0 // tm
    return pl.pallas_call(
        _final_norm_kernel,
        grid=(n_rows // tm,),
        in_specs=[
            pl.BlockSpec((tm, D_MODEL), lambda i: (i + off, 0)),
            pl.BlockSpec((1, D_MODEL), lambda i: (0, 0)),
        ],
        out_specs=pl.BlockSpec((tm, D_MODEL), lambda i: (i, 0)),
        out_shape=jax.ShapeDtypeStruct((n_rows, D_MODEL), F32),
        compiler_params=_params("parallel"),
        name="final_norm",
    )(x, g.reshape(1, D_MODEL))


def _w16(w_ref):
    return w_ref[...].astype(BF16)


def _mm_kernel(x_ref, w_ref, o_ref):
    o_ref[...] = _dot(x_ref[...], _w16(w_ref)).astype(o_ref.dtype)


def matmul(x, w, layer, *, tm, tn, out_dtype, row0=0, m=None):
    k = x.shape[1]
    m = x.shape[0] if m is None else m
    n = w.shape[-1]
    r_off = row0 // tm
    return pl.pallas_call(
        _mm_kernel,
        grid=(m // tm, n // tn),
        in_specs=[
            pl.BlockSpec((tm, k), lambda i, j: (r_off + i, 0)),
            pl.BlockSpec((None, k, tn), lambda i, j: (layer, 0, j)),
        ],
        out_specs=pl.BlockSpec((tm, tn), lambda i, j: (i, j)),
        out_shape=jax.ShapeDtypeStruct((m, n), out_dtype),
        compiler_params=_params("parallel", "arbitrary"),
        name="matmul",
    )(x, w)


def _mm_nt_kernel(x_ref, w_ref, w2_ref, o_ref, o2_ref):
    x = x_ref[...]
    o_ref[...] = _dot_nt(x, _w16(w_ref)).astype(o_ref.dtype)

    @pl.when(pl.program_id(1) == 0)
    def _():
        o2_ref[...] = _dot(x, w2_ref[...])


def matmul_nt(x, w_t, w2, layer, *, tm, tn, n):
    m, k = x.shape
    n2 = w2.shape[-1]
    return pl.pallas_call(
        _mm_nt_kernel,
        grid=(m // tm, n // tn),
        in_specs=[
            pl.BlockSpec((tm, k), lambda i, j: (i, 0)),
            pl.BlockSpec((None, tn, k), lambda i, j: (layer, j, 0)),
            pl.BlockSpec((None, k, n2), lambda i, j: (layer, 0, 0)),
        ],
        out_specs=[
            pl.BlockSpec((tm, tn), lambda i, j: (i, j)),
            pl.BlockSpec((tm, n2), lambda i, j: (i, 0)),
        ],
        out_shape=[
            jax.ShapeDtypeStruct((m, n), F32),
            jax.ShapeDtypeStruct((m, n2), F32),
        ],
        compiler_params=_params("parallel", "arbitrary"),
        name="matmul_nt",
    )(x, w_t, w2)


def _mm_res_kernel(x_ref, w_ref, *refs, mix_cols):
    *r_refs, g_ref, o_ref = refs
    w = _w16(w_ref)
    if mix_cols:
        acc = (_dot(x_ref[:, :MIX_A_COL], w[LRU_WIDTH:]) + _dot(x_ref[:, MIX_A_COL:], w[:LRU_WIDTH]))
    else:
        acc = _dot(x_ref[...], w)
    o_ref[...] = _read_rows(r_refs, o_ref.shape[0]) + g_ref[...] * acc


def matmul_gated_residual(x, w, layer, res, gate, *, tm, tn, mix_cols=False):
    m, k = x.shape
    n = w.shape[-1]
    r_args, r_specs = _row_parts(res, tm, tn)
    return pl.pallas_call(
        functools.partial(_mm_res_kernel, mix_cols=mix_cols),
        grid=(m // tm, n // tn),
        in_specs=[
            pl.BlockSpec((tm, k), lambda i, j: (i, 0)),
            pl.BlockSpec((None, k, tn), lambda i, j: (layer, 0, j)),
        ] + r_specs + [
            pl.BlockSpec((None, 1, tn), lambda i, j: (_cond_of_tile(i, tm), 0, j)),
        ],
        out_specs=pl.BlockSpec((tm, tn), lambda i, j: (i, j)),
        out_shape=jax.ShapeDtypeStruct((m, n), F32),
        compiler_params=_params("parallel", "arbitrary"),
        name="matmul_gated_residual",
    )(x, w, *r_args, gate)


def _swiglu_kernel(x_ref, wg_ref, wu_ref, wd_ref, *refs, mod_steps):
    if mod_steps:
        c_ref, wm_ref, bm_ref, o_ref, wd16_ref, mod_ref = refs
    else:
        o_ref, wd16_ref = refs
    x = x_ref[...]
    g = _dot(x, _w16(wg_ref))
    u = _dot(x, _w16(wu_ref))
    o_ref[...] = (_silu(g) * u).astype(o_ref.dtype)
    wd16_ref[...] = wd_ref[...].astype(BF16)
    if mod_steps:
        @pl.when(pl.program_id(0) * pl.num_programs(1) + pl.program_id(1) < mod_steps)
        def _():
            mod_ref[...] = _mod_tile(c_ref, wm_ref, bm_ref)


def swiglu_up(x, w_gate, w_up, w_down, layer, *, tm, tn, mod_args=None):
    m, k = x.shape
    n = w_gate.shape[-1]
    nj = n // tn
    steps = (m // tm) * nj
    kd, nd = w_down.shape[1:]
    slab = kd // steps
    assert slab * steps == kd and slab % 16 == 0
    in_specs = [
        pl.BlockSpec((tm, k), lambda i, j: (i, 0), pipeline_mode=pl.Buffered(1)),
        pl.BlockSpec((None, k, tn), lambda i, j: (layer, 0, j)),
        pl.BlockSpec((None, k, tn), lambda i, j: (layer, 0, j)),
        pl.BlockSpec((None, slab, nd), lambda i, j: (layer, i * nj + j, 0)),
    ]
    out_specs = [
        pl.BlockSpec((tm, tn), lambda i, j: (i, j)),
        pl.BlockSpec((None, slab, nd), lambda i, j: (0, i * nj + j, 0)),
    ]
    out_shape = [
        jax.ShapeDtypeStruct((m, n), BF16),
        jax.ShapeDtypeStruct((1, kd, nd), BF16),
    ]
    args = [x, w_gate, w_up, w_down]
    mod_steps = 0
    if mod_args is not None:
        mod_steps = N_MOD // MOD_SLAB
        assert mod_steps <= steps
        slab_of = lambda i, j: jnp.minimum(i * nj + j, mod_steps - 1)
        in_specs += [
            pl.BlockSpec((8, D_MODEL), lambda i, j: (0, 0)),
            pl.BlockSpec((None, D_MODEL, MOD_SLAB), lambda i, j: (layer + 1, 0, slab_of(i, j))),
            pl.BlockSpec((None, 1, MOD_SLAB), lambda i, j: (layer + 1, 0, slab_of(i, j))),
        ]
        out_specs.append(pl.BlockSpec((8, MOD_SLAB), lambda i, j: (0, slab_of(i, j))))
        out_shape.append(jax.ShapeDtypeStruct((8, N_MOD), F32))
        args += list(mod_args)
    return pl.pallas_call(
        functools.partial(_swiglu_kernel, mod_steps=mod_steps),
        grid=(m // tm, nj),
        in_specs=in_specs,
        out_specs=out_specs,
        out_shape=out_shape,
        compiler_params=_params("arbitrary", "arbitrary"),
        name="swiglu_up",
    )(*args)


def _rope(x, c, se, so):
    nxt = pltpu.roll(x, LANE - 1, 1)
    prv = pltpu.roll(x, 1, 1)
    return x * c + nxt * se + prv * so


def _mm_rope_kernel(x_ref, w_ref, c_ref, se_ref, so_ref, o_ref, *, first_rope_tile):
    acc = _dot(x_ref[...], _w16(w_ref))
    j = pl.program_id(1)

    @pl.when(j < first_rope_tile)
    def _():
        o_ref[...] = acc.astype(o_ref.dtype)

    @pl.when(j >= first_rope_tile)
    def _():
        c, se, so = c_ref[...], se_ref[...], so_ref[...]
        for h in range(acc.shape[1] // LANE):
            sl = slice(h * LANE, (h + 1) * LANE)
            o_ref[:, sl] = _rope(acc[:, sl], c, se, so).astype(o_ref.dtype)


def matmul_rope_tail(x, w, layer, tables, *, tm, tn, first_rope_tile):
    m, k = x.shape
    n = w.shape[-1]
    nt = DEC_SEQ // tm
    tab = pl.BlockSpec((tm, LANE), lambda i, j: (i % nt, 0))
    return pl.pallas_call(
        functools.partial(_mm_rope_kernel, first_rope_tile=first_rope_tile),
        grid=(m // tm, n // tn),
        in_specs=[
            pl.BlockSpec((tm, k), lambda i, j: (i, 0)),
            pl.BlockSpec((None, k, tn), lambda i, j: (layer, 0, j)),
            tab, tab, tab,
        ],
        out_specs=pl.BlockSpec((tm, tn), lambda i, j: (i, j)),
        out_shape=jax.ShapeDtypeStruct((m, n), BF16),
        compiler_params=_params("parallel", "arbitrary"),
        name="matmul_rope_tail",
    )(x, w, *tables)


def _head_rmsnorm(x, g):
    return x * lax.rsqrt(jnp.mean(x * x, axis=-1, keepdims=True) + EPS) * g


def _prep_kernel(*refs, rope):
    if rope:
        (qk_ref, v_ref, c_ref, kr_ref, qn_ref, kn_ref, mqn_ref, mkvn_ref,
         cb_ref, seb_ref, sob_ref, cc_ref, sec_ref, soc_ref,
         _k_buf, _v_buf, _ckv_buf, _kr_buf,
         q_o, k_o, v_o, cq_o, ckv_o, kr_o) = refs
    else:
        (qk_ref, v_ref, c_ref, kr_ref, qn_ref, kn_ref, mqn_ref, mkvn_ref,
         q_o, k_o, v_o, cq_o, ckv_o, kr_o, kf_o, ckvf_o) = refs
    qn, kn = qn_ref[...], kn_ref[...]
    if rope:
        cb, seb, sob = cb_ref[...], seb_ref[...], sob_ref[...]
    for h in range(GQA_HEADS + GQA_KV_HEADS):
        sl = slice(h * LANE, (h + 1) * LANE)
        is_q = h < GQA_HEADS
        y = _head_rmsnorm(qk_ref[:, sl], qn if is_q else kn)
        osl = sl if is_q else slice((h - GQA_HEADS) * LANE, (h - GQA_HEADS + 1) * LANE)
        if not rope and not is_q:
            kf_o[:, osl] = y
        if rope:
            y = _rope(y, cb, seb, sob)
        (q_o if is_q else k_o)[:, osl] = y.astype(BF16)
    v_o[...] = v_ref[...].astype(BF16)
    cq = c_ref[:, :MLA_Q_RANK]
    cq_o[...] = _head_rmsnorm(cq, mqn_ref[...]).astype(BF16)
    ckv = _head_rmsnorm(c_ref[:, MLA_Q_RANK:], mkvn_ref[...])
    ckv_o[...] = ckv.astype(BF16)
    if not rope:
        ckvf_o[...] = ckv
    kr = kr_ref[...]
    if rope:
        kr = _rope(kr, cc_ref[...], sec_ref[...], soc_ref[...])
    kr_o[...] = kr.astype(BF16)


def prep(p, kr, gqa_qn, gqa_kn, mla_qn, mla_kvn, *, latent, tables_b=None, tables_c=None,
         layer=None, key_bufs=None):
    tm = ROW_TILE
    n_rows = N_LAT if latent else N_CTX
    off = (N_CTX // tm) if latent else 0
    nt = DEC_SEQ // tm
    row = lambda c: (lambda i: (i + off, c))
    in_specs = [
        pl.BlockSpec((tm, 2048), row(1)),
        pl.BlockSpec((tm, KV_WIDTH), row(8)),
        pl.BlockSpec((tm, 1536), row(3)),
        pl.BlockSpec((tm, LANE), row(0)),
        pl.BlockSpec((1, HEAD_DIM), lambda i: (0, 0)),
        pl.BlockSpec((1, HEAD_DIM), lambda i: (0, 0)),
        pl.BlockSpec((1, MLA_Q_RANK), lambda i: (0, 0)),
        pl.BlockSpec((1, MLA_KV_RANK), lambda i: (0, 0)),
    ]
    args = [p, p, p, kr, gqa_qn.reshape(1, -1), gqa_kn.reshape(1, -1),
            mla_qn.reshape(1, -1), mla_kvn.reshape(1, -1)]
    out = lambda w, dt: (pl.BlockSpec((tm, w), lambda i: (i, 0)), jax.ShapeDtypeStruct((n_rows, w), dt))
    outs = [out(GQA_WIDTH, BF16), out(KV_WIDTH, BF16), out(KV_WIDTH, BF16),
            out(MLA_Q_RANK, BF16), out(MLA_KV_RANK, BF16), out(LANE, BF16)]
    aliases = {}
    if latent:
        tab = pl.BlockSpec((tm, LANE), lambda i: (i % nt, 0))
        in_specs += [tab] * 6
        args += list(tables_b) + list(tables_c)
        blocks_per_req = LAT_KEYS // tm

        def key_rows(i):
            return ((layer * DEC_BATCH + i // nt) * blocks_per_req + PAST_LEN // tm + i % nt, 0)

        for n_out, buf in zip((1, 2, 4, 5), key_bufs):
            aliases[len(args)] = n_out
            in_specs.append(pl.BlockSpec(memory_space=pl.ANY))
            args.append(buf)
            outs[n_out] = (pl.BlockSpec((tm, buf.shape[1]), key_rows), jax.ShapeDtypeStruct(buf.shape, buf.dtype))
    else:
        outs += [out(KV_WIDTH, F32), out(MLA_KV_RANK, F32)]
    return pl.pallas_call(
        functools.partial(_prep_kernel, rope=latent),
        grid=(n_rows // tm,),
        in_specs=in_specs,
        out_specs=[o[0] for o in outs],
        out_shape=[o[1] for o in outs],
        input_output_aliases=aliases,
        compiler_params=_params("parallel"),
        name="prep_latent" if latent else "prep_context",
    )(*args)


def _gelu_tanh(x):
    return 0.5 * x * (1.0 + jnp.tanh(0.7978845608028654 * (x + 0.044715 * (x * x * x))))


def _softplus(z):
    return jnp.maximum(z, 0.0) + jnp.log1p(jnp.exp(-jnp.abs(z)))


def _lru_kernel(xa_ref, ga_ref, cw_ref, cb_ref, w_ref, p_ref, h0_ref, mix_ref, o_ref, st_ref,
                a_scr, b_scr, hf_scr, hb_scr, *, t_sub, tiles_per_iter):
    del mix_ref
    rows, c = xa_ref.shape
    n_sub = rows // t_sub
    x = xa_ref[...]
    tpos = lax.broadcasted_iota(jnp.int32, x.shape, 0) % t_sub
    cw = cw_ref[...]
    xm2 = jnp.where(tpos >= 2, pltpu.roll(x, 2, 0), 0.0)
    xm1 = jnp.where(tpos >= 1, pltpu.roll(x, 1, 0), 0.0)
    xp1 = jnp.where(tpos < t_sub - 1, pltpu.roll(x, rows - 1, 0), 0.0)
    xc = cb_ref[...] + xm2 * cw[0:1] + xm1 * cw[1:2] + x * cw[2:3] + xp1 * cw[3:4]
    xc16 = xc.astype(BF16)

    for d in range(2):
        g = _dot(xc16, w_ref[d])
        prm = p_ref[d]
        r = jax.nn.sigmoid(g[:, :c] + prm[0:1])
        ig = jax.nn.sigmoid(g[:, c:] + prm[1:2])
        log_a = (-LRU_C) * r * _softplus(-prm[2:3])
        a = jnp.exp(log_a)
        a_scr[d] = a
        b_scr[d] = jnp.sqrt(-jnp.tanh(log_a) * (a * a + 1.0)) * (ig * xc)

    sub_row = lax.broadcasted_iota(jnp.int32, (8, c), 0)
    keep_f = [sub_row >= s for s in (1, 2, 4)]
    keep_b = [sub_row < 8 - s for s in (1, 2, 4)]

    def tile_scan(a, b, carry, reverse):
        for n, s in enumerate((1, 2, 4)):
            keep, shift = (keep_b[n], 8 - s) if reverse else (keep_f[n], s)
            a_prev = jnp.where(keep, pltpu.roll(a, shift, 0), 1.0)
            b_prev = jnp.where(keep, pltpu.roll(b, shift, 0), 0.0)
            b = a * b_prev + b
            a = a * a_prev
        h = a * carry + b
        last = h[0:1] if reverse else h[7:8]
        return h, jnp.broadcast_to(last, h.shape)

    def body(j, hs):
        hs = list(hs)
        for u in range(tiles_per_iter):
            off = pl.multiple_of((j * tiles_per_iter + u) * 8, 8)
            for q in range(n_sub):
                idx_f = pl.ds(q * t_sub + off, 8)
                idx_b = pl.ds(q * t_sub + (t_sub - 8) - off, 8)
                hf, hs[2 * q] = tile_scan(a_scr[0, idx_f, :], b_scr[0, idx_f, :], hs[2 * q], False)
                hb, hs[2 * q + 1] = tile_scan(a_scr[1, idx_b, :], b_scr[1, idx_b, :], hs[2 * q + 1], True)
                hf_scr[idx_f, :] = hf
                hb_scr[idx_b, :] = hb
        return tuple(hs)

    init = tuple(jnp.broadcast_to(h0_ref[q, d:d + 1, :], (8, c)) for q in range(n_sub) for d in range(2))
    hs = lax.fori_loop(0, t_sub // (8 * tiles_per_iter), body, init)
    for q in range(n_sub):
        st_ref[q, 0:1, :] = hs[2 * q][0:1]
        st_ref[q, 1:2, :] = hs[2 * q + 1][0:1]
    o_ref[...] = ((hf_scr[...] + hb_scr[...]) * _gelu_tanh(ga_ref[...])).astype(o_ref.dtype)


def lru_mixer(p, conv_w, conv_b, w_gates, lru_prm, h0, mix, layer, *, latent):
    c = LRU_CHUNK
    rows = LRU_ROWS
    t_sub = DEC_SEQ if latent else SEQ
    n_sub = rows // t_sub
    n_rows = N_LAT if latent else N_CTX
    off = (N_CTX // rows) if latent else 0
    nck = LRU_WIDTH // c
    return pl.pallas_call(
        functools.partial(_lru_kernel, t_sub=t_sub, tiles_per_iter=4 if latent else 1),
        grid=(n_rows // rows, nck),
        in_specs=[
            pl.BlockSpec((rows, c), lambda s, k: (s + off, k)),
            pl.BlockSpec((rows, c), lambda s, k: (s + off, nck + k)),
            pl.BlockSpec((None, 4, c), lambda s, k: (layer, 0, k)),
            pl.BlockSpec((None, 1, c), lambda s, k: (layer, 0, k)),
            pl.BlockSpec((None, 2, None, c, 2 * c), lambda s, k: (layer, 0, k, 0, 0)),
            pl.BlockSpec((None, 2, 3, c), lambda s, k: (layer, 0, 0, k)),
            pl.BlockSpec((n_sub, 2, c), lambda s, k: (s, 0, k)),
            pl.BlockSpec(memory_space=pl.ANY),
        ],
        out_specs=[
            pl.BlockSpec((rows, c), lambda s, k: (s + off, MIX_A_COL // c + k)),
            pl.BlockSpec((n_sub, 2, c), lambda s, k: (s, 0, k)),
        ],
        out_shape=[
            jax.ShapeDtypeStruct(mix.shape, mix.dtype),
            jax.ShapeDtypeStruct((n_rows // t_sub, 2, LRU_WIDTH), F32),
        ],
        input_output_aliases={7: 0},
        scratch_shapes=[pltpu.VMEM((2, rows, c), F32)] * 2 + [pltpu.VMEM((rows, c), F32)] * 2,
        compiler_params=_params("parallel", "parallel"),
        name="lru_latent" if latent else "lru_context",
    )(p, p, conv_w, conv_b.reshape(DEPTH, 1, LRU_WIDTH), w_gates, lru_prm, h0, mix)


LOG2E = 1.4426950408889634


def _softmax_pv(s, v, scale):
    t = s * (scale * LOG2E)
    m = jnp.max(t, axis=-1, keepdims=True)
    p = jnp.exp2(t - m).astype(BF16)
    oa = _dot(p, jnp.concatenate([v, jnp.ones_like(v)], axis=1))
    return oa[:, :LANE] / oa[:, LANE:]


def _gqa_kernel(q_ref, k_ref, v_ref, mix_ref, o_ref, *, scale, groups, sub):
    del mix_ref
    tq = q_ref.shape[0]
    for g in range(groups):
        kv_sl = slice(g * LANE, (g + 1) * LANE)
        k, v = k_ref[:, kv_sl], v_ref[:, kv_sl]
        for h in range(g * GQA_REP, (g + 1) * GQA_REP):
            sl = slice(h * LANE, (h + 1) * LANE)
            for r0 in range(0, tq, sub):
                s = _dot_nt(q_ref[r0:r0 + sub, sl], k)
                o_ref[r0:r0 + sub, sl] = _softmax_pv(s, v, scale).astype(o_ref.dtype)


def gqa_attention(q, k, v, mix, row0, *, n_batch, t_q, t_k, tq, groups, sub, kv_b0=0):
    nq = t_q // tq
    qw = groups * GQA_REP * HEAD_DIM
    kw = groups * HEAD_DIM
    r_off, c_off = row0 // tq, MIX_B_COL // qw
    return pl.pallas_call(
        functools.partial(_gqa_kernel, scale=HEAD_DIM ** -0.5, groups=groups, sub=sub),
        grid=(n_batch, GQA_KV_HEADS // groups, nq),
        in_specs=[
            pl.BlockSpec((tq, qw), lambda b, g, i: (b * nq + i, g)),
            pl.BlockSpec((None, t_k, kw), lambda b, g, i: (kv_b0 + b, 0, g)),
            pl.BlockSpec((None, t_k, kw), lambda b, g, i: (kv_b0 + b, 0, g)),
            pl.BlockSpec(memory_space=pl.ANY),
        ],
        out_specs=pl.BlockSpec((tq, qw), lambda b, g, i: (r_off + b * nq + i, c_off + g)),
        out_shape=jax.ShapeDtypeStruct(mix.shape, mix.dtype),
        input_output_aliases={3: 0},
        compiler_params=_params("parallel", "parallel", "arbitrary"),
        name="gqa_attention",
    )(q, k, v, mix)


def _mla_kernel(qn_ref, qr_ref, kv_ref, kr_ref, mix_ref, o_ref, *, scale, heads, sub):
    del mix_ref
    tq = qn_ref.shape[0]
    kr = kr_ref[...]
    for h in range(heads):
        sl = slice(h * LANE, (h + 1) * LANE)
        k0 = h * (MLA_NOPE + MLA_V)
        k = jnp.concatenate([kv_ref[:, k0:k0 + MLA_NOPE], kr], axis=1)
        v = kv_ref[:, k0 + MLA_NOPE:k0 + MLA_NOPE + MLA_V]
        for r0 in range(0, tq, sub):
            q = jnp.concatenate([qn_ref[r0:r0 + sub, sl], qr_ref[r0:r0 + sub, sl]], axis=1)
            o_ref[r0:r0 + sub, sl] = _softmax_pv(_dot_nt(q, k), v, scale).astype(o_ref.dtype)


def mla_attention(q, kv, kr, mix, row0, *, n_batch, t_q, t_k, tq, heads, sub, kr_b0=0):
    nq = t_q // tq
    nh = MLA_HEADS // heads
    ow = heads * MLA_V
    r_off, c_off = row0 // tq, MIX_C_COL // ow
    return pl.pallas_call(
        functools.partial(_mla_kernel, scale=(MLA_NOPE + MLA_ROPE) ** -0.5, heads=heads, sub=sub),
        grid=(n_batch, nh, nq),
        in_specs=[
            pl.BlockSpec((tq, heads * MLA_NOPE), lambda b, h, i: (b * nq + i, h)),
            pl.BlockSpec((tq, heads * LANE), lambda b, h, i: (b * nq + i, nh + h)),
            pl.BlockSpec((None, t_k, heads * (MLA_NOPE + MLA_V)), lambda b, h, i: (b, 0, h)),
            pl.BlockSpec((None, t_k, LANE), lambda b, h, i: (kr_b0 + b, 0, 0)),
            pl.BlockSpec(memory_space=pl.ANY),
        ],
        out_specs=pl.BlockSpec((tq, ow), lambda b, h, i: (r_off + b * nq + i, c_off + h)),
        out_shape=jax.ShapeDtypeStruct(mix.shape, mix.dtype),
        input_output_aliases={4: 0},
        compiler_params=_params("parallel", "parallel", "arbitrary"),
        name="mla_attention",
    )(q, q, kv, kr, mix)


def _rope_tables(rot_dim):
    rows = DEC_SEQ // GRID_W
    row = jnp.repeat(jnp.arange(rows, dtype=F32), GRID_W)
    col = jnp.tile(jnp.arange(GRID_W, dtype=F32), rows)
    quarter = rot_dim // 4
    freqs = ROPE_THETA ** (-jnp.arange(quarter, dtype=F32) / quarter)
    ang = jnp.concatenate([row[:, None] * freqs, col[:, None] * freqs], axis=-1)
    cos, sin = jnp.cos(ang), jnp.sin(ang)
    zero = jnp.zeros_like(sin)
    c = jnp.repeat(cos, 2, axis=-1)
    se = jnp.stack([-sin, zero], axis=-1).reshape(DEC_SEQ, rot_dim)
    so = jnp.stack([zero, sin], axis=-1).reshape(DEC_SEQ, rot_dim)
    pad = ((0, 0), (0, LANE - rot_dim))
    return tuple(jnp.pad(t, pad) for t in (c, se, so))


def _block_diag_gates(lru_wr, lru_wi):
    per = LRU_CHUNK // LRU_BLOCK
    nck = LRU_WIDTH // LRU_CHUNK
    eye = jnp.eye(per, dtype=F32)

    def bd(w):
        w = w.reshape(DEPTH, 2, nck, per, LRU_BLOCK, LRU_BLOCK)
        w = jnp.einsum("dzcakj,ab->dzcakbj", w, eye)
        return w.reshape(DEPTH, 2, nck, LRU_CHUNK, LRU_CHUNK)

    return jnp.concatenate([bd(lru_wr), bd(lru_wi)], axis=-1).astype(BF16)


def _key_buffer(cache, width):
    c = jnp.swapaxes(cache, 0, 1).astype(BF16)
    c = jnp.pad(c, ((0, 0), (0, 0), (0, DEC_SEQ), (0, width - cache.shape[-1])))
    return c.reshape(DEPTH * DEC_BATCH * LAT_KEYS, width)


def _permute_w_uq(w_uq):
    w = w_uq.reshape(DEPTH, MLA_Q_RANK, MLA_HEADS, MLA_NOPE + MLA_ROPE)
    nope = w[..., :MLA_NOPE].reshape(DEPTH, MLA_Q_RANK, MLA_HEADS * MLA_NOPE)
    rope = jnp.pad(w[..., MLA_NOPE:], ((0, 0), (0, 0), (0, 0), (0, LANE - MLA_ROPE)))
    rope = rope.reshape(DEPTH, MLA_Q_RANK, MLA_HEADS * LANE)
    return jnp.concatenate([nope, rope], axis=-1).astype(BF16)


def kernel(x_prompt, x_sample, state_lru, cache_gqa_k, cache_gqa_v, cache_mla_ckv, cache_mla_krope, c,
           c_ctx, w_mod, b_mod, norm_mix_g, norm_ffn_g, w_in, conv_w, conv_b, lru_wr, lru_br, lru_wi, lru_bi,
           lru_lam, gqa_qn, gqa_kn, mla_qn, mla_kvn, w_uq, w_ukv, w_out, w_gate, w_up, w_down, norm_f):
    x = (x_prompt.reshape(N_CTX, D_MODEL), x_sample.reshape(N_LAT, D_MODEL))

    cond8 = jnp.zeros((8, D_MODEL), F32).at[0].set(c_ctx).at[1:1 + DEC_BATCH].set(c)
    b_mod3 = b_mod.reshape(DEPTH, 1, N_MOD)
    mod = modulation(cond8, w_mod, b_mod3, 0)

    w_kr16 = jnp.pad(w_in[:, :, IN_MAIN:], ((0, 0), (0, 0), (0, LANE - MLA_ROPE))).astype(BF16)
    w_in_t = jnp.swapaxes(w_in, 1, 2)
    w_uq16 = _permute_w_uq(w_uq)
    w_gates = _block_diag_gates(lru_wr, lru_wi)
    lru_prm = jnp.stack([lru_br, lru_bi, lru_lam], axis=2)
    tables_b = _rope_tables(HEAD_DIM)
    tables_c = _rope_tables(MLA_ROPE)
    h0_ctx = jnp.zeros((BATCH, 2, LRU_WIDTH), F32)
    key_bufs = (_key_buffer(cache_gqa_k.reshape(DEC_BATCH, DEPTH, PAST_LEN, KV_WIDTH), KV_WIDTH),
                _key_buffer(cache_gqa_v.reshape(DEC_BATCH, DEPTH, PAST_LEN, KV_WIDTH), KV_WIDTH),
                _key_buffer(cache_mla_ckv, MLA_KV_RANK),
                _key_buffer(cache_mla_krope, LANE))

    mix = jnp.zeros((N_TOK, D_MIX), BF16)

    st_lru, st_k, st_v, st_ckv, st_kr = [], [], [], [], []
    for l in range(DEPTH):
        mod6 = mod.reshape(8, 6, 1, D_MODEL)
        sh1, sc1, g1, sh2, sc2, g2 = (mod6[:, s] for s in range(6))

        h = norm_mod(x, norm_mix_g[l], sc1, sh1)
        p, kr = matmul_nt(h, w_in_t, w_kr16, l, tm=1024, tn=512, n=IN_MAIN)

        mix, s_lru = lru_mixer(p, conv_w, conv_b, w_gates, lru_prm, h0_ctx, mix, l, latent=False)
        mix, _ = lru_mixer(p, conv_w, conv_b, w_gates, lru_prm, state_lru[:, l], mix, l, latent=True)

        (q_c, k_c, v_c, cq_c, ckv_c, kr_c, kf_c, ckvf_c) = prep(
            p, kr, gqa_qn[l], gqa_kn[l], mla_qn[l], mla_kvn[l], latent=False)
        q_l, k_buf, v_buf, cq_l, ckv_buf, kr_buf = prep(
            p, kr, gqa_qn[l], gqa_kn[l], mla_qn[l], mla_kvn[l], latent=True,
            tables_b=tables_b, tables_c=tables_c, layer=l, key_bufs=key_bufs)
        key_bufs = (k_buf, v_buf, ckv_buf, kr_buf)
        req0 = l * DEC_BATCH

        mix = gqa_attention(q_c, k_c.reshape(BATCH, SEQ, KV_WIDTH), v_c.reshape(BATCH, SEQ, KV_WIDTH), mix, 0,
                            n_batch=BATCH, t_q=SEQ, t_k=SEQ, tq=SEQ, groups=GQA_KV_HEADS, sub=SEQ)
        mix = gqa_attention(q_l, k_buf.reshape(-1, LAT_KEYS, KV_WIDTH), v_buf.reshape(-1, LAT_KEYS, KV_WIDTH),
                            mix, N_CTX, n_batch=DEC_BATCH, t_q=DEC_SEQ, t_k=LAT_KEYS, tq=512, groups=1, sub=128,
                            kv_b0=req0)

        qm_c = matmul(cq_c, w_uq16, l, tm=2048, tn=1024, out_dtype=BF16)
        qm_l = matmul_rope_tail(cq_l, w_uq16, l, tables_c, tm=1024, tn=MLA_WIDTH, first_rope_tile=1)
        kv_c = matmul(ckv_c, w_ukv, l, tm=2048, tn=1024, out_dtype=BF16)
        kv_l = matmul(ckv_buf, w_ukv, l, tm=LAT_KEYS, tn=1024, out_dtype=BF16,
                      row0=req0 * LAT_KEYS, m=DEC_BATCH * LAT_KEYS)
        mix = mla_attention(qm_c, kv_c.reshape(BATCH, SEQ, -1), kr_c.reshape(BATCH, SEQ, LANE), mix, 0,
                            n_batch=BATCH, t_q=SEQ, t_k=SEQ, tq=SEQ, heads=MLA_HEADS, sub=SEQ)
        mix = mla_attention(qm_l, kv_l.reshape(DEC_BATCH, LAT_KEYS, -1), kr_buf.reshape(-1, LAT_KEYS, LANE),
                            mix, N_CTX, n_batch=DEC_BATCH, t_q=DEC_SEQ, t_k=LAT_KEYS, tq=1024, heads=1, sub=256,
                            kr_b0=req0)

        x = matmul_gated_residual(mix, w_out, l, x, g1, tm=1024, tn=512, mix_cols=True)

        h = norm_mod(x, norm_ffn_g[l], sc2, sh2)
        if l + 1 < DEPTH:
            ff, w_down16, mod = swiglu_up(h, w_gate, w_up, w_down, l, tm=2048, tn=256,
                                          mod_args=(cond8, w_mod, b_mod3))
        else:
            ff, w_down16 = swiglu_up(h, w_gate, w_up, w_down, l, tm=2048, tn=256)
        x = matmul_gated_residual(ff, w_down16, 0, x, g2, tm=512, tn=512)

        st_lru.append(s_lru)
        st_k.append(kf_c.reshape(BATCH, SEQ, GQA_KV_HEADS, HEAD_DIM))
        st_v.append(p[:N_CTX, 4096:4096 + KV_WIDTH].reshape(BATCH, SEQ, GQA_KV_HEADS, HEAD_DIM))
        st_ckv.append(ckvf_c.reshape(BATCH, SEQ, MLA_KV_RANK))
        st_kr.append(kr[:N_CTX, :MLA_ROPE].reshape(BATCH, SEQ, MLA_ROPE))

    y_ctx = final_norm(x, norm_f, 0, N_CTX)
    y_lat = final_norm(x, norm_f, N_CTX, N_LAT)
    return (y_ctx.reshape(BATCH, SEQ, D_MODEL), y_lat.reshape(DEC_BATCH, DEC_SEQ, D_MODEL),
            jnp.stack(st_lru, axis=1), jnp.stack(st_k, axis=1), jnp.stack(st_v, axis=1),
            jnp.stack(st_ckv, axis=1), jnp.stack(st_kr, axis=1))
```

```python
import functools

import jax
import jax.numpy as jnp
from jax import lax
from jax.experimental import pallas as pl
from jax.experimental.pallas import tpu as pltpu

F32 = jnp.float32
BF16 = jnp.bfloat16

D_MODEL = 4096
BATCH = 16
SEQ = 256
DEPTH = 4
DEC_BATCH = 2
DEC_SEQ = 2048
PAST_LEN = 512
GRID_W = 64
EPS = 1e-6
ROPE_THETA = 10000.0

LRU_WIDTH = 1024
LRU_BLOCKS = 16
LRU_BLOCK = 64
LRU_C = 8.0
HEAD_DIM = 128
GQA_HEADS = 12
GQA_KV_HEADS = 4
GQA_REP = GQA_HEADS // GQA_KV_HEADS
GQA_WIDTH = GQA_HEADS * HEAD_DIM
KV_WIDTH = GQA_KV_HEADS * HEAD_DIM
MLA_HEADS = 12
MLA_Q_RANK = 1024
MLA_KV_RANK = 512
MLA_NOPE = 128
MLA_ROPE = 64
MLA_V = 128
MLA_WIDTH = MLA_HEADS * MLA_V
D_MIX = LRU_WIDTH + GQA_WIDTH + MLA_WIDTH
MIX_B_COL = 0
MIX_C_COL = GQA_WIDTH
MIX_A_COL = GQA_WIDTH + MLA_WIDTH
D_FF = 11008
IN_MAIN = 6144
LANE = 128
N_CTX = BATCH * SEQ
N_LAT = DEC_BATCH * DEC_SEQ
N_TOK = N_CTX + N_LAT
LAT_KEYS = PAST_LEN + DEC_SEQ
VMEM_LIMIT = 56 * 1024 * 1024
LRU_CHUNK = 256
LRU_ROWS = 2048
ROW_TILE = 256
NORM_TILE = 512
NORM_CHUNK = 16


def _params(*sem):
    return pltpu.CompilerParams(dimension_semantics=sem, vmem_limit_bytes=VMEM_LIMIT)


def _cond_of_tile(i, tm):
    row = i * tm
    return jnp.where(row < N_CTX, 0, 1 + (row - N_CTX) // DEC_SEQ)


def _silu(x):
    return x * jax.nn.sigmoid(x)


def _dot(a, b):
    return jnp.dot(a, b, preferred_element_type=F32)


def _dot_nt(a, b):
    return lax.dot_general(a, b, (((1,), (1,)), ((), ())), preferred_element_type=F32)


N_MOD = 6 * D_MODEL
MOD_SLAB = 256


def _mod_tile(c_ref, w_ref, b_ref):
    return _dot(_silu(c_ref[...]).astype(BF16), _w16(w_ref)) + b_ref[...]


def _mod_kernel(c_ref, w_ref, b_ref, o_ref):
    o_ref[...] = _mod_tile(c_ref, w_ref, b_ref)


def modulation(cond8, w_mod, b_mod3, layer):
    tn = 512
    return pl.pallas_call(
        _mod_kernel,
        grid=(N_MOD // tn,),
        in_specs=[
            pl.BlockSpec((8, D_MODEL), lambda j: (0, 0)),
            pl.BlockSpec((None, D_MODEL, tn), lambda j: (layer, 0, j)),
            pl.BlockSpec((None, 1, tn), lambda j: (layer, 0, j)),
        ],
        out_specs=pl.BlockSpec((8, tn), lambda j: (0, j)),
        out_shape=jax.ShapeDtypeStruct((8, N_MOD), F32),
        compiler_params=_params("parallel"),
        name="modulation",
    )(cond8, w_mod, b_mod3)


def _row_parts(x, tm, tn=None):
    def spec(row_fn, col_fn):
        if tn is None:
            return pl.BlockSpec((tm, D_MODEL), lambda i: (row_fn(i), 0))
        return pl.BlockSpec((tm, tn), lambda i, j: (row_fn(i), col_fn(i, j)))

    if not isinstance(x, tuple):
        return [x], [spec(lambda i: i, lambda i, j: j)]
    n_a = x[0].shape[0] // tm
    return list(x), [
        spec(lambda i: jnp.minimum(i, n_a - 1), lambda i, j: jnp.where(i < n_a, j, 0)),
        spec(lambda i: jnp.maximum(i - n_a, 0), lambda i, j: jnp.where(i < n_a, 0, j)),
    ]


def _read_rows(refs, tm, rows=slice(None)):
    if len(refs) == 1:
        return refs[0][rows, :]
    n_a = N_CTX // tm
    return jnp.where(pl.program_id(0) < n_a, refs[0][rows, :], refs[1][rows, :])


def _rmsnorm_rows(x_refs, o_ref, scale, shift):
    tm = o_ref.shape[0]

    def body(c, carry):
        rows = pl.ds(pl.multiple_of(c * NORM_CHUNK, NORM_CHUNK), NORM_CHUNK)
        x = _read_rows(x_refs, tm, rows)
        y = x * lax.rsqrt(jnp.mean(x * x, axis=-1, keepdims=True) + EPS) * scale
        if shift is not None:
            y = y + shift
        o_ref[rows, :] = y.astype(o_ref.dtype)
        return carry

    lax.fori_loop(0, tm // NORM_CHUNK, body, 0, unroll=4)


def _norm_mod_kernel(*refs):
    *x_refs, g_ref, sc_ref, sh_ref, o_ref = refs
    _rmsnorm_rows(x_refs, o_ref, g_ref[...] * (1.0 + sc_ref[...]), sh_ref[...])


def norm_mod(x, g, sc, sh):
    tm = NORM_TILE
    x_args, x_specs = _row_parts(x, tm)
    return pl.pallas_call(
        _norm_mod_kernel,
        grid=(N_TOK // tm,),
        in_specs=x_specs + [
            pl.BlockSpec((1, D_MODEL), lambda i: (0, 0)),
            pl.BlockSpec((None, 1, D_MODEL), lambda i: (_cond_of_tile(i, tm), 0, 0)),
            pl.BlockSpec((None, 1, D_MODEL), lambda i: (_cond_of_tile(i, tm), 0, 0)),
        ],
        out_specs=pl.BlockSpec((tm, D_MODEL), lambda i: (i, 0)),
        out_shape=jax.ShapeDtypeStruct((N_TOK, D_MODEL), BF16),
        compiler_params=_params("parallel"),
        name="norm_mod",
    )(*x_args, g.reshape(1, D_MODEL), sc, sh)


def _final_norm_kernel(x_ref, g_ref, o_ref):
    _rmsnorm_rows([x_ref], o_ref, g_ref[...], None)


def final_norm(x, g, row0, n_rows):
    tm = NORM_TILE
    off = row0 // tm
    return pl.pallas_call(
        _final_norm_kernel,
        grid=(n_rows // tm,),
        in_specs=[
            pl.BlockSpec((tm, D_MODEL), lambda i: (i + off, 0)),
            pl.BlockSpec((1, D_MODEL), lambda i: (0, 0)),
        ],
        out_specs=pl.BlockSpec((tm, D_MODEL), lambda i: (i, 0)),
        out_shape=jax.ShapeDtypeStruct((n_rows, D_MODEL), F32),
        compiler_params=_params("parallel"),
        name="final_norm",
    )(x, g.reshape(1, D_MODEL))


def _w16(w_ref):
    return w_ref[...].astype(BF16)


def _mm_kernel(x_ref, w_ref, o_ref):
    o_ref[...] = _dot(x_ref[...], _w16(w_ref)).astype(o_ref.dtype)


def matmul(x, w, layer, *, tm, tn, out_dtype, row0=0, m=None):
    k = x.shape[1]
    m = x.shape[0] if m is None else m
    n = w.shape[-1]
    r_off = row0 // tm
    return pl.pallas_call(
        _mm_kernel,
        grid=(m // tm, n // tn),
        in_specs=[
            pl.BlockSpec((tm, k), lambda i, j: (r_off + i, 0)),
            pl.BlockSpec((None, k, tn), lambda i, j: (layer, 0, j)),
        ],
        out_specs=pl.BlockSpec((tm, tn), lambda i, j: (i, j)),
        out_shape=jax.ShapeDtypeStruct((m, n), out_dtype),
        compiler_params=_params("parallel", "arbitrary"),
        name="matmul",
    )(x, w)


def _mm_nt_kernel(x_ref, w_ref, w2_ref, o_ref, o2_ref):
    x = x_ref[...]
    o_ref[...] = _dot_nt(x, _w16(w_ref)).astype(o_ref.dtype)

    @pl.when(pl.program_id(1) == 0)
    def _():
        o2_ref[...] = _dot(x, w2_ref[...])


def matmul_nt(x, w_t, w2, layer, *, tm, tn, n):
    m, k = x.shape
    n2 = w2.shape[-1]
    return pl.pallas_call(
        _mm_nt_kernel,
        grid=(m // tm, n // tn),
        in_specs=[
            pl.BlockSpec((tm, k), lambda i, j: (i, 0)),
            pl.BlockSpec((None, tn, k), lambda i, j: (layer, j, 0)),
            pl.BlockSpec((None, k, n2), lambda i, j: (layer, 0, 0)),
        ],
        out_specs=[
            pl.BlockSpec((tm, tn), lambda i, j: (i, j)),
            pl.BlockSpec((tm, n2), lambda i, j: (i, 0)),
        ],
        out_shape=[
            jax.ShapeDtypeStruct((m, n), F32),
            jax.ShapeDtypeStruct((m, n2), F32),
        ],
        compiler_params=_params("parallel", "arbitrary"),
        name="matmul_nt",
    )(x, w_t, w2)


def _mm_res_kernel(x_ref, w_ref, *refs, mix_cols):
    *r_refs, g_ref, o_ref = refs
    w = _w16(w_ref)
    if mix_cols:
        acc = (_dot(x_ref[:, :MIX_A_COL], w[LRU_WIDTH:]) + _dot(x_ref[:, MIX_A_COL:], w[:LRU_WIDTH]))
    else:
        acc = _dot(x_ref[...], w)
    o_ref[...] = _read_rows(r_refs, o_ref.shape[0]) + g_ref[...] * acc


def matmul_gated_residual(x, w, layer, res, gate, *, tm, tn, mix_cols=False):
    m, k = x.shape
    n = w.shape[-1]
    r_args, r_specs = _row_parts(res, tm, tn)
    return pl.pallas_call(
        functools.partial(_mm_res_kernel, mix_cols=mix_cols),
        grid=(m // tm, n // tn),
        in_specs=[
            pl.BlockSpec((tm, k), lambda i, j: (i, 0)),
            pl.BlockSpec((None, k, tn), lambda i, j: (layer, 0, j)),
        ] + r_specs + [
            pl.BlockSpec((None, 1, tn), lambda i, j: (_cond_of_tile(i, tm), 0, j)),
        ],
        out_specs=pl.BlockSpec((tm, tn), lambda i, j: (i, j)),
        out_shape=jax.ShapeDtypeStruct((m, n), F32),
        compiler_params=_params("parallel", "arbitrary"),
        name="matmul_gated_residual",
    )(x, w, *r_args, gate)


def _swiglu_kernel(x_ref, wg_ref, wu_ref, wd_ref, *refs, mod_steps):
    if mod_steps:
        c_ref, wm_ref, bm_ref, o_ref, wd16_ref, mod_ref = refs
    else:
        o_ref, wd16_ref = refs
    x = x_ref[...]
    g = _dot(x, _w16(wg_ref))
    u = _dot(x, _w16(wu_ref))
    o_ref[...] = (_silu(g) * u).astype(o_ref.dtype)
    wd16_ref[...] = wd_ref[...].astype(BF16)
    if mod_steps:
        @pl.when(pl.program_id(0) * pl.num_programs(1) + pl.program_id(1) < mod_steps)
        def _():
            mod_ref[...] = _mod_tile(c_ref, wm_ref, bm_ref)


def swiglu_up(x, w_gate, w_up, w_down, layer, *, tm, tn, mod_args=None):
    m, k = x.shape
    n = w_gate.shape[-1]
    nj = n // tn
    steps = (m // tm) * nj
    kd, nd = w_down.shape[1:]
    slab = kd // steps
    assert slab * steps == kd and slab % 16 == 0
    in_specs = [
        pl.BlockSpec((tm, k), lambda i, j: (i, 0), pipeline_mode=pl.Buffered(1)),
        pl.BlockSpec((None, k, tn), lambda i, j: (layer, 0, j)),
        pl.BlockSpec((None, k, tn), lambda i, j: (layer, 0, j)),
        pl.BlockSpec((None, slab, nd), lambda i, j: (layer, i * nj + j, 0)),
    ]
    out_specs = [
        pl.BlockSpec((tm, tn), lambda i, j: (i, j)),
        pl.BlockSpec((None, slab, nd), lambda i, j: (0, i * nj + j, 0)),
    ]
    out_shape = [
        jax.ShapeDtypeStruct((m, n), BF16),
        jax.ShapeDtypeStruct((1, kd, nd), BF16),
    ]
    args = [x, w_gate, w_up, w_down]
    mod_steps = 0
    if mod_args is not None:
        mod_steps = N_MOD // MOD_SLAB
        assert mod_steps <= steps
        slab_of = lambda i, j: jnp.minimum(i * nj + j, mod_steps - 1)
        in_specs += [
            pl.BlockSpec((8, D_MODEL), lambda i, j: (0, 0)),
            pl.BlockSpec((None, D_MODEL, MOD_SLAB), lambda i, j: (layer + 1, 0, slab_of(i, j))),
            pl.BlockSpec((None, 1, MOD_SLAB), lambda i, j: (layer + 1, 0, slab_of(i, j))),
        ]
        out_specs.append(pl.BlockSpec((8, MOD_SLAB), lambda i, j: (0, slab_of(i, j))))
        out_shape.append(jax.ShapeDtypeStruct((8, N_MOD), F32))
        args += list(mod_args)
    return pl.pallas_call(
        functools.partial(_swiglu_kernel, mod_steps=mod_steps),
        grid=(m // tm, nj),
        in_specs=in_specs,
        out_specs=out_specs,
        out_shape=out_shape,
        compiler_params=_params("arbitrary", "arbitrary"),
        name="swiglu_up",
    )(*args)


def _rope(x, c, se, so):
    nxt = pltpu.roll(x, LANE - 1, 1)
    prv = pltpu.roll(x, 1, 1)
    return x * c + nxt * se + prv * so


def _mm_rope_kernel(x_ref, w_ref, c_ref, se_ref, so_ref, o_ref, *, first_rope_tile):
    acc = _dot(x_ref[...], _w16(w_ref))
    j = pl.program_id(1)

    @pl.when(j < first_rope_tile)
    def _():
        o_ref[...] = acc.astype(o_ref.dtype)

    @pl.when(j >= first_rope_tile)
    def _():
        c, se, so = c_ref[...], se_ref[...], so_ref[...]
        for h in range(acc.shape[1] // LANE):
            sl = slice(h * LANE, (h + 1) * LANE)
            o_ref[:, sl] = _rope(acc[:, sl], c, se, so).astype(o_ref.dtype)


def matmul_rope_tail(x, w, layer, tables, *, tm, tn, first_rope_tile):
    m, k = x.shape
    n = w.shape[-1]
    nt = DEC_SEQ // tm
    tab = pl.BlockSpec((tm, LANE), lambda i, j: (i % nt, 0))
    return pl.pallas_call(
        functools.partial(_mm_rope_kernel, first_rope_tile=first_rope_tile),
        grid=(m // tm, n // tn),
        in_specs=[
            pl.BlockSpec((tm, k), lambda i, j: (i, 0)),
            pl.BlockSpec((None, k, tn), lambda i, j: (layer, 0, j)),
            tab, tab, tab,
        ],
        out_specs=pl.BlockSpec((tm, tn), lambda i, j: (i, j)),
        out_shape=jax.ShapeDtypeStruct((m, n), BF16),
        compiler_params=_params("parallel", "arbitrary"),
        name="matmul_rope_tail",
    )(x, w, *tables)


def _head_rmsnorm(x, g):
    return x * lax.rsqrt(jnp.mean(x * x, axis=-1, keepdims=True) + EPS) * g


def _prep_kernel(*refs, rope):
    if rope:
        (qk_ref, v_ref, c_ref, kr_ref, qn_ref, kn_ref, mqn_ref, mkvn_ref,
         cb_ref, seb_ref, sob_ref, cc_ref, sec_ref, soc_ref,
         _k_buf, _v_buf, _ckv_buf, _kr_buf,
         q_o, k_o, v_o, cq_o, ckv_o, kr_o) = refs
    else:
        (qk_ref, v_ref, c_ref, kr_ref, qn_ref, kn_ref, mqn_ref, mkvn_ref,
         q_o, k_o, v_o, cq_o, ckv_o, kr_o, kf_o, ckvf_o) = refs
    qn, kn = qn_ref[...], kn_ref[...]
    if rope:
        cb, seb, sob = cb_ref[...], seb_ref[...], sob_ref[...]
    for h in range(GQA_HEADS + GQA_KV_HEADS):
        sl = slice(h * LANE, (h + 1) * LANE)
        is_q = h < GQA_HEADS
        y = _head_rmsnorm(qk_ref[:, sl], qn if is_q else kn)
        osl = sl if is_q else slice((h - GQA_HEADS) * LANE, (h - GQA_HEADS + 1) * LANE)
        if not rope and not is_q:
            kf_o[:, osl] = y
        if rope:
            y = _rope(y, cb, seb, sob)
        (q_o if is_q else k_o)[:, osl] = y.astype(BF16)
    v_o[...] = v_ref[...].astype(BF16)
    cq = c_ref[:, :MLA_Q_RANK]
    cq_o[...] = _head_rmsnorm(cq, mqn_ref[...]).astype(BF16)
    ckv = _head_rmsnorm(c_ref[:, MLA_Q_RANK:], mkvn_ref[...])
    ckv_o[...] = ckv.astype(BF16)
    if not rope:
        ckvf_o[...] = ckv
    kr = kr_ref[...]
    if rope:
        kr = _rope(kr, cc_ref[...], sec_ref[...], soc_ref[...])
    kr_o[...] = kr.astype(BF16)


def prep(p, kr, gqa_qn, gqa_kn, mla_qn, mla_kvn, *, latent, tables_b=None, tables_c=None,
         layer=None, key_bufs=None):
    tm = ROW_TILE
    n_rows = N_LAT if latent else N_CTX
    off = (N_CTX // tm) if latent else 0
    nt = DEC_SEQ // tm
    row = lambda c: (lambda i: (i + off, c))
    in_specs = [
        pl.BlockSpec((tm, 2048), row(1)),
        pl.BlockSpec((tm, KV_WIDTH), row(8)),
        pl.BlockSpec((tm, 1536), row(3)),
        pl.BlockSpec((tm, LANE), row(0)),
        pl.BlockSpec((1, HEAD_DIM), lambda i: (0, 0)),
        pl.BlockSpec((1, HEAD_DIM), lambda i: (0, 0)),
        pl.BlockSpec((1, MLA_Q_RANK), lambda i: (0, 0)),
        pl.BlockSpec((1, MLA_KV_RANK), lambda i: (0, 0)),
    ]
    args = [p, p, p, kr, gqa_qn.reshape(1, -1), gqa_kn.reshape(1, -1),
            mla_qn.reshape(1, -1), mla_kvn.reshape(1, -1)]
    out = lambda w, dt: (pl.BlockSpec((tm, w), lambda i: (i, 0)), jax.ShapeDtypeStruct((n_rows, w), dt))
    outs = [out(GQA_WIDTH, BF16), out(KV_WIDTH, BF16), out(KV_WIDTH, BF16),
            out(MLA_Q_RANK, BF16), out(MLA_KV_RANK, BF16), out(LANE, BF16)]
    aliases = {}
    if latent:
        tab = pl.BlockSpec((tm, LANE), lambda i: (i % nt, 0))
        in_specs += [tab] * 6
        args += list(tables_b) + list(tables_c)
        blocks_per_req = LAT_KEYS // tm

        def key_rows(i):
            return ((layer * DEC_BATCH + i // nt) * blocks_per_req + PAST_LEN // tm + i % nt, 0)

        for n_out, buf in zip((1, 2, 4, 5), key_bufs):
            aliases[len(args)] = n_out
            in_specs.append(pl.BlockSpec(memory_space=pl.ANY))
            args.append(buf)
            outs[n_out] = (pl.BlockSpec((tm, buf.shape[1]), key_rows), jax.ShapeDtypeStruct(buf.shape, buf.dtype))
    else:
        outs += [out(KV_WIDTH, F32), out(MLA_KV_RANK, F32)]
    return pl.pallas_call(
        functools.partial(_prep_kernel, rope=latent),
        grid=(n_rows // tm,),
        in_specs=in_specs,
        out_specs=[o[0] for o in outs],
        out_shape=[o[1] for o in outs],
        input_output_aliases=aliases,
        compiler_params=_params("parallel"),
        name="prep_latent" if latent else "prep_context",
    )(*args)


def _gelu_tanh(x):
    return 0.5 * x * (1.0 + jnp.tanh(0.7978845608028654 * (x + 0.044715 * (x * x * x))))


def _softplus(z):
    return jnp.maximum(z, 0.0) + jnp.log1p(jnp.exp(-jnp.abs(z)))


def _lru_kernel(xa_ref, ga_ref, cw_ref, cb_ref, w_ref, p_ref, h0_ref, mix_ref, o_ref, st_ref,
                a_scr, b_scr, hf_scr, hb_scr, *, t_sub, tiles_per_iter):
    del mix_ref
    rows, c = xa_ref.shape
    n_sub = rows // t_sub
    x = xa_ref[...]
    tpos = lax.broadcasted_iota(jnp.int32, x.shape, 0) % t_sub
    cw = cw_ref[...]
    xm2 = jnp.where(tpos >= 2, pltpu.roll(x, 2, 0), 0.0)
    xm1 = jnp.where(tpos >= 1, pltpu.roll(x, 1, 0), 0.0)
    xp1 = jnp.where(tpos < t_sub - 1, pltpu.roll(x, rows - 1, 0), 0.0)
    xc = cb_ref[...] + xm2 * cw[0:1] + xm1 * cw[1:2] + x * cw[2:3] + xp1 * cw[3:4]
    xc16 = xc.astype(BF16)

    for d in range(2):
        g = _dot(xc16, w_ref[d])
        prm = p_ref[d]
        r = jax.nn.sigmoid(g[:, :c] + prm[0:1])
        ig = jax.nn.sigmoid(g[:, c:] + prm[1:2])
        log_a = (-LRU_C) * r * _softplus(-prm[2:3])
        a = jnp.exp(log_a)
        a_scr[d] = a
        b_scr[d] = jnp.sqrt(-jnp.tanh(log_a) * (a * a + 1.0)) * (ig * xc)

    sub_row = lax.broadcasted_iota(jnp.int32, (8, c), 0)
    keep_f = [sub_row >= s for s in (1, 2, 4)]
    keep_b = [sub_row < 8 - s for s in (1, 2, 4)]

    def tile_scan(a, b, carry, reverse):
        for n, s in enumerate((1, 2, 4)):
            keep, shift = (keep_b[n], 8 - s) if reverse else (keep_f[n], s)
            a_prev = jnp.where(keep, pltpu.roll(a, shift, 0), 1.0)
            b_prev = jnp.where(keep, pltpu.roll(b, shift, 0), 0.0)
            b = a * b_prev + b
            a = a * a_prev
        h = a * carry + b
        last = h[0:1] if reverse else h[7:8]
        return h, jnp.broadcast_to(last, h.shape)

    def body(j, hs):
        hs = list(hs)
        for u in range(tiles_per_iter):
            off = pl.multiple_of((j * tiles_per_iter + u) * 8, 8)
            for q in range(n_sub):
                idx_f = pl.ds(q * t_sub + off, 8)
                idx_b = pl.ds(q * t_sub + (t_sub - 8) - off, 8)
                hf, hs[2 * q] = tile_scan(a_scr[0, idx_f, :], b_scr[0, idx_f, :], hs[2 * q], False)
                hb, hs[2 * q + 1] = tile_scan(a_scr[1, idx_b, :], b_scr[1, idx_b, :], hs[2 * q + 1], True)
                hf_scr[idx_f, :] = hf
                hb_scr[idx_b, :] = hb
        return tuple(hs)

    init = tuple(jnp.broadcast_to(h0_ref[q, d:d + 1, :], (8, c)) for q in range(n_sub) for d in range(2))
    hs = lax.fori_loop(0, t_sub // (8 * tiles_per_iter), body, init)
    for q in range(n_sub):
        st_ref[q, 0:1, :] = hs[2 * q][0:1]
        st_ref[q, 1:2, :] = hs[2 * q + 1][0:1]
    o_ref[...] = ((hf_scr[...] + hb_scr[...]) * _gelu_tanh(ga_ref[...])).astype(o_ref.dtype)


def lru_mixer(p, conv_w, conv_b, w_gates, lru_prm, h0, mix, layer, *, latent):
    c = LRU_CHUNK
    rows = LRU_ROWS
    t_sub = DEC_SEQ if latent else SEQ
    n_sub = rows // t_sub
    n_rows = N_LAT if latent else N_CTX
    off = (N_CTX // rows) if latent else 0
    nck = LRU_WIDTH // c
    return pl.pallas_call(
        functools.partial(_lru_kernel, t_sub=t_sub, tiles_per_iter=4 if latent else 1),
        grid=(n_rows // rows, nck),
        in_specs=[
            pl.BlockSpec((rows, c), lambda s, k: (s + off, k)),
            pl.BlockSpec((rows, c), lambda s, k: (s + off, nck + k)),
            pl.BlockSpec((None, 4, c), lambda s, k: (layer, 0, k)),
            pl.BlockSpec((None, 1, c), lambda s, k: (layer, 0, k)),
            pl.BlockSpec((None, 2, None, c, 2 * c), lambda s, k: (layer, 0, k, 0, 0)),
            pl.BlockSpec((None, 2, 3, c), lambda s, k: (layer, 0, 0, k)),
            pl.BlockSpec((n_sub, 2, c), lambda s, k: (s, 0, k)),
            pl.BlockSpec(memory_space=pl.ANY),
        ],
        out_specs=[
            pl.BlockSpec((rows, c), lambda s, k: (s + off, MIX_A_COL // c + k)),
            pl.BlockSpec((n_sub, 2, c), lambda s, k: (s, 0, k)),
        ],
        out_shape=[
            jax.ShapeDtypeStruct(mix.shape, mix.dtype),
            jax.ShapeDtypeStruct((n_rows // t_sub, 2, LRU_WIDTH), F32),
        ],
        input_output_aliases={7: 0},
        scratch_shapes=[pltpu.VMEM((2, rows, c), F32)] * 2 + [pltpu.VMEM((rows, c), F32)] * 2,
        compiler_params=_params("parallel", "parallel"),
        name="lru_latent" if latent else "lru_context",
    )(p, p, conv_w, conv_b.reshape(DEPTH, 1, LRU_WIDTH), w_gates, lru_prm, h0, mix)


LOG2E = 1.4426950408889634


def _softmax_pv(s, v, scale):
    t = s * (scale * LOG2E)
    m = jnp.max(t, axis=-1, keepdims=True)
    p = jnp.exp2(t - m).astype(BF16)
    oa = _dot(p, jnp.concatenate([v, jnp.ones_like(v)], axis=1))
    return oa[:, :LANE] / oa[:, LANE:]


def _gqa_kernel(q_ref, k_ref, v_ref, mix_ref, o_ref, *, scale, groups, sub):
    del mix_ref
    tq = q_ref.shape[0]
    for g in range(groups):
        kv_sl = slice(g * LANE, (g + 1) * LANE)
        k, v = k_ref[:, kv_sl], v_ref[:, kv_sl]
        for h in range(g * GQA_REP, (g + 1) * GQA_REP):
            sl = slice(h * LANE, (h + 1) * LANE)
            for r0 in range(0, tq, sub):
                s = _dot_nt(q_ref[r0:r0 + sub, sl], k)
                o_ref[r0:r0 + sub, sl] = _softmax_pv(s, v, scale).astype(o_ref.dtype)


def gqa_attention(q, k, v, mix, row0, *, n_batch, t_q, t_k, tq, groups, sub, kv_b0=0):
    nq = t_q // tq
    qw = groups * GQA_REP * HEAD_DIM
    kw = groups * HEAD_DIM
    r_off, c_off = row0 // tq, MIX_B_COL // qw
    return pl.pallas_call(
        functools.partial(_gqa_kernel, scale=HEAD_DIM ** -0.5, groups=groups, sub=sub),
        grid=(n_batch, GQA_KV_HEADS // groups, nq),
        in_specs=[
            pl.BlockSpec((tq, qw), lambda b, g, i: (b * nq + i, g)),
            pl.BlockSpec((None, t_k, kw), lambda b, g, i: (kv_b0 + b, 0, g)),
            pl.BlockSpec((None, t_k, kw), lambda b, g, i: (kv_b0 + b, 0, g)),
            pl.BlockSpec(memory_space=pl.ANY),
        ],
        out_specs=pl.BlockSpec((tq, qw), lambda b, g, i: (r_off + b * nq + i, c_off + g)),
        out_shape=jax.ShapeDtypeStruct(mix.shape, mix.dtype),
        input_output_aliases={3: 0},
        compiler_params=_params("parallel", "parallel", "arbitrary"),
        name="gqa_attention",
    )(q, k, v, mix)


def _mla_kernel(qn_ref, qr_ref, kv_ref, kr_ref, mix_ref, o_ref, *, scale, heads, sub):
    del mix_ref
    tq = qn_ref.shape[0]
    kr = kr_ref[...]
    for h in range(heads):
        sl = slice(h * LANE, (h + 1) * LANE)
        k0 = h * (MLA_NOPE + MLA_V)
        k = jnp.concatenate([kv_ref[:, k0:k0 + MLA_NOPE], kr], axis=1)
        v = kv_ref[:, k0 + MLA_NOPE:k0 + MLA_NOPE + MLA_V]
        for r0 in range(0, tq, sub):
            q = jnp.concatenate([qn_ref[r0:r0 + sub, sl], qr_ref[r0:r0 + sub, sl]], axis=1)
            o_ref[r0:r0 + sub, sl] = _softmax_pv(_dot_nt(q, k), v, scale).astype(o_ref.dtype)


def mla_attention(q, kv, kr, mix, row0, *, n_batch, t_q, t_k, tq, heads, sub, kr_b0=0):
    nq = t_q // tq
    nh = MLA_HEADS // heads
    ow = heads * MLA_V
    r_off, c_off = row0 // tq, MIX_C_COL // ow
    return pl.pallas_call(
        functools.partial(_mla_kernel, scale=(MLA_NOPE + MLA_ROPE) ** -0.5, heads=heads, sub=sub),
        grid=(n_batch, nh, nq),
        in_specs=[
            pl.BlockSpec((tq, heads * MLA_NOPE), lambda b, h, i: (b * nq + i, h)),
            pl.BlockSpec((tq, heads * LANE), lambda b, h, i: (b * nq + i, nh + h)),
            pl.BlockSpec((None, t_k, heads * (MLA_NOPE + MLA_V)), lambda b, h, i: (b, 0, h)),
            pl.BlockSpec((None, t_k, LANE), lambda b, h, i: (kr_b0 + b, 0, 0)),
            pl.BlockSpec(memory_space=pl.ANY),
        ],
        out_specs=pl.BlockSpec((tq, ow), lambda b, h, i: (r_off + b * nq + i, c_off + h)),
        out_shape=jax.ShapeDtypeStruct(mix.shape, mix.dtype),
        input_output_aliases={4: 0},
        compiler_params=_params("parallel", "parallel", "arbitrary"),
        name="mla_attention",
    )(q, q, kv, kr, mix)


def _rope_tables(rot_dim):
    rows = DEC_SEQ // GRID_W
    row = jnp.repeat(jnp.arange(rows, dtype=F32), GRID_W)
    col = jnp.tile(jnp.arange(GRID_W, dtype=F32), rows)
    quarter = rot_dim // 4
    freqs = ROPE_THETA ** (-jnp.arange(quarter, dtype=F32) / quarter)
    ang = jnp.concatenate([row[:, None] * freqs, col[:, None] * freqs], axis=-1)
    cos, sin = jnp.cos(ang), jnp.sin(ang)
    zero = jnp.zeros_like(sin)
    c = jnp.repeat(cos, 2, axis=-1)
    se = jnp.stack([-sin, zero], axis=-1).reshape(DEC_SEQ, rot_dim)
    so = jnp.stack([zero, sin], axis=-1).reshape(DEC_SEQ, rot_dim)
    pad = ((0, 0), (0, LANE - rot_dim))
    return tuple(jnp.pad(t, pad) for t in (c, se, so))


def _block_diag_gates(lru_wr, lru_wi):
    per = LRU_CHUNK // LRU_BLOCK
    nck = LRU_WIDTH // LRU_CHUNK
    eye = jnp.eye(per, dtype=F32)

    def bd(w):
        w = w.reshape(DEPTH, 2, nck, per, LRU_BLOCK, LRU_BLOCK)
        w = jnp.einsum("dzcakj,ab->dzcakbj", w, eye)
        return w.reshape(DEPTH, 2, nck, LRU_CHUNK, LRU_CHUNK)

    return jnp.concatenate([bd(lru_wr), bd(lru_wi)], axis=-1).astype(BF16)


def _key_buffer(cache, width):
    c = jnp.swapaxes(cache, 0, 1).astype(BF16)
    c = jnp.pad(c, ((0, 0), (0, 0), (0, DEC_SEQ), (0, width - cache.shape[-1])))
    return c.reshape(DEPTH * DEC_BATCH * LAT_KEYS, width)


def _permute_w_uq(w_uq):
    w = w_uq.reshape(DEPTH, MLA_Q_RANK, MLA_HEADS, MLA_NOPE + MLA_ROPE)
    nope = w[..., :MLA_NOPE].reshape(DEPTH, MLA_Q_RANK, MLA_HEADS * MLA_NOPE)
    rope = jnp.pad(w[..., MLA_NOPE:], ((0, 0), (0, 0), (0, 0), (0, LANE - MLA_ROPE)))
    rope = rope.reshape(DEPTH, MLA_Q_RANK, MLA_HEADS * LANE)
    return jnp.concatenate([nope, rope], axis=-1).astype(BF16)


def kernel(x_prompt, x_sample, state_lru, cache_gqa_k, cache_gqa_v, cache_mla_ckv, cache_mla_krope, c,
           c_ctx, w_mod, b_mod, norm_mix_g, norm_ffn_g, w_in, conv_w, conv_b, lru_wr, lru_br, lru_wi, lru_bi,
           lru_lam, gqa_qn, gqa_kn, mla_qn, mla_kvn, w_uq, w_ukv, w_out, w_gate, w_up, w_down, norm_f):
    x = (x_prompt.reshape(N_CTX, D_MODEL), x_sample.reshape(N_LAT, D_MODEL))

    cond8 = jnp.zeros((8, D_MODEL), F32).at[0].set(c_ctx).at[1:1 + DEC_BATCH].set(c)
    b_mod3 = b_mod.reshape(DEPTH, 1, N_MOD)
    mod = modulation(cond8, w_mod, b_mod3, 0)

    w_kr16 = jnp.pad(w_in[:, :, IN_MAIN:], ((0, 0), (0, 0), (0, LANE - MLA_ROPE))).astype(BF16)
    w_in_t = jnp.swapaxes(w_in, 1, 2)
    w_uq16 = _permute_w_uq(w_uq)
    w_gates = _block_diag_gates(lru_wr, lru_wi)
    lru_prm = jnp.stack([lru_br, lru_bi, lru_lam], axis=2)
    tables_b = _rope_tables(HEAD_DIM)
    tables_c = _rope_tables(MLA_ROPE)
    h0_ctx = jnp.zeros((BATCH, 2, LRU_WIDTH), F32)
    key_bufs = (_key_buffer(cache_gqa_k.reshape(DEC_BATCH, DEPTH, PAST_LEN, KV_WIDTH), KV_WIDTH),
                _key_buffer(cache_gqa_v.reshape(DEC_BATCH, DEPTH, PAST_LEN, KV_WIDTH), KV_WIDTH),
                _key_buffer(cache_mla_ckv, MLA_KV_RANK),
                _key_buffer(cache_mla_krope, LANE))

    mix = jnp.zeros((N_TOK, D_MIX), BF16)

    st_lru, st_k, st_v, st_ckv, st_kr = [], [], [], [], []
    for l in range(DEPTH):
        mod6 = mod.reshape(8, 6, 1, D_MODEL)
        sh1, sc1, g1, sh2, sc2, g2 = (mod6[:, s] for s in range(6))

        h = norm_mod(x, norm_mix_g[l], sc1, sh1)
        p, kr = matmul_nt(h, w_in_t, w_kr16, l, tm=1024, tn=512, n=IN_MAIN)

        mix, s_lru = lru_mixer(p, conv_w, conv_b, w_gates, lru_prm, h0_ctx, mix, l, latent=False)
        mix, _ = lru_mixer(p, conv_w, conv_b, w_gates, lru_prm, state_lru[:, l], mix, l, latent=True)

        (q_c, k_c, v_c, cq_c, ckv_c, kr_c, kf_c, ckvf_c) = prep(
            p, kr, gqa_qn[l], gqa_kn[l], mla_qn[l], mla_kvn[l], latent=False)
        q_l, k_buf, v_buf, cq_l, ckv_buf, kr_buf = prep(
            p, kr, gqa_qn[l], gqa_kn[l], mla_qn[l], mla_kvn[l], latent=True,
            tables_b=tables_b, tables_c=tables_c, layer=l, key_bufs=key_bufs)
        key_bufs = (k_buf, v_buf, ckv_buf, kr_buf)
        req0 = l * DEC_BATCH

        mix = gqa_attention(q_c, k_c.reshape(BATCH, SEQ, KV_WIDTH), v_c.reshape(BATCH, SEQ, KV_WIDTH), mix, 0,
                            n_batch=BATCH, t_q=SEQ, t_k=SEQ, tq=SEQ, groups=GQA_KV_HEADS, sub=SEQ)
        mix = gqa_attention(q_l, k_buf.reshape(-1, LAT_KEYS, KV_WIDTH), v_buf.reshape(-1, LAT_KEYS, KV_WIDTH),
                            mix, N_CTX, n_batch=DEC_BATCH, t_q=DEC_SEQ, t_k=LAT_KEYS, tq=512, groups=1, sub=128,
                            kv_b0=req0)

        qm_c = matmul(cq_c, w_uq16, l, tm=2048, tn=1024, out_dtype=BF16)
        qm_l = matmul_rope_tail(cq_l, w_uq16, l, tables_c, tm=1024, tn=MLA_WIDTH, first_rope_tile=1)
        kv_c = matmul(ckv_c, w_ukv, l, tm=2048, tn=1024, out_dtype=BF16)
        kv_l = matmul(ckv_buf, w_ukv, l, tm=LAT_KEYS, tn=1024, out_dtype=BF16,
                      row0=req0 * LAT_KEYS, m=DEC_BATCH * LAT_KEYS)
        mix = mla_attention(qm_c, kv_c.reshape(BATCH, SEQ, -1), kr_c.reshape(BATCH, SEQ, LANE), mix, 0,
                            n_batch=BATCH, t_q=SEQ, t_k=SEQ, tq=SEQ, heads=MLA_HEADS, sub=SEQ)
        mix = mla_attention(qm_l, kv_l.reshape(DEC_BATCH, LAT_KEYS, -1), kr_buf.reshape(-1, LAT_KEYS, LANE),
                            mix, N_CTX, n_batch=DEC_BATCH, t_q=DEC_SEQ, t_k=LAT_KEYS, tq=1024, heads=1, sub=256,
                            kr_b0=req0)

        x = matmul_gated_residual(mix, w_out, l, x, g1, tm=1024, tn=512, mix_cols=True)

        h = norm_mod(x, norm_ffn_g[l], sc2, sh2)
        if l + 1 < DEPTH:
            ff, w_down16, mod = swiglu_up(h, w_gate, w_up, w_down, l, tm=2048, tn=256,
                                          mod_args=(cond8, w_mod, b_mod3))
        else:
            ff, w_down16 = swiglu_up(h, w_gate, w_up, w_down, l, tm=2048, tn=256)
        x = matmul_gated_residual(ff, w_down16, 0, x, g2, tm=512, tn=512)

        st_lru.append(s_lru)
        st_k.append(kf_c.reshape(BATCH, SEQ, GQA_KV_HEADS, HEAD_DIM))
        st_v.append(p[:N_CTX, 4096:4096 + KV_WIDTH].reshape(BATCH, SEQ, GQA_KV_HEADS, HEAD_DIM))
        st_ckv.append(ckvf_c.reshape(BATCH, SEQ, MLA_KV_RANK))
        st_kr.append(kr[:N_CTX, :MLA_ROPE].reshape(BATCH, SEQ, MLA_ROPE))

    y_ctx = final_norm(x, norm_f, 0, N_CTX)
    y_lat = final_norm(x, norm_f, N_CTX, N_LAT)
    return (y_ctx.reshape(BATCH, SEQ, D_MODEL), y_lat.reshape(DEC_BATCH, DEC_SEQ, D_MODEL),
            jnp.stack(st_lru, axis=1), jnp.stack(st_k, axis=1), jnp.stack(st_v, axis=1),
            jnp.stack(st_ckv, axis=1), jnp.stack(st_kr, axis=1))
```

```python
import functools

import jax
import jax.numpy as jnp
from jax import lax
from jax.experimental import pallas as pl
from jax.experimental.pallas import tpu as pltpu

F32 = jnp.float32
BF16 = jnp.bfloat16

D_MODEL = 4096
BATCH = 16
SEQ = 256
DEPTH = 4
DEC_BATCH = 2
DEC_SEQ = 2048
PAST_LEN = 512
GRID_W = 64
EPS = 1e-6
ROPE_THETA = 10000.0

LRU_WIDTH = 1024
LRU_BLOCKS = 16
LRU_BLOCK = 64
LRU_C = 8.0
HEAD_DIM = 128
GQA_HEADS = 12
GQA_KV_HEADS = 4
GQA_REP = GQA_HEADS // GQA_KV_HEADS
GQA_WIDTH = GQA_HEADS * HEAD_DIM
KV_WIDTH = GQA_KV_HEADS * HEAD_DIM
MLA_HEADS = 12
MLA_Q_RANK = 1024
MLA_KV_RANK = 512
MLA_NOPE = 128
MLA_ROPE = 64
MLA_V = 128
MLA_WIDTH = MLA_HEADS * MLA_V
D_MIX = LRU_WIDTH + GQA_WIDTH + MLA_WIDTH
MIX_B_COL = 0
MIX_C_COL = GQA_WIDTH
MIX_A_COL = GQA_WIDTH + MLA_WIDTH
D_FF = 11008
IN_MAIN = 6144
LANE = 128
N_CTX = BATCH * SEQ
N_LAT = DEC_BATCH * DEC_SEQ
N_TOK = N_CTX + N_LAT
LAT_KEYS = PAST_LEN + DEC_SEQ
VMEM_LIMIT = 56 * 1024 * 1024
LRU_CHUNK = 256
LRU_ROWS = 2048
ROW_TILE = 256
NORM_TILE = 512
NORM_CHUNK = 16


def _params(*sem):
    return pltpu.CompilerParams(dimension_semantics=sem, vmem_limit_bytes=VMEM_LIMIT)


def _cond_of_tile(i, tm):
    row = i * tm
    return jnp.where(row < N_CTX, 0, 1 + (row - N_CTX) // DEC_SEQ)


def _silu(x):
    return x * jax.nn.sigmoid(x)


def _dot(a, b):
    return jnp.dot(a, b, preferred_element_type=F32)


def _dot_nt(a, b):
    return lax.dot_general(a, b, (((1,), (1,)), ((), ())), preferred_element_type=F32)


N_MOD = 6 * D_MODEL
MOD_SLAB = 256


def _mod_tile(c_ref, w_ref, b_ref):
    return _dot(_silu(c_ref[...]).astype(BF16), _w16(w_ref)) + b_ref[...]


def _mod_kernel(c_ref, w_ref, b_ref, o_ref):
    o_ref[...] = _mod_tile(c_ref, w_ref, b_ref)


def modulation(cond8, w_mod, b_mod3, layer):
    tn = 512
    return pl.pallas_call(
        _mod_kernel,
        grid=(N_MOD // tn,),
        in_specs=[
            pl.BlockSpec((8, D_MODEL), lambda j: (0, 0)),
            pl.BlockSpec((None, D_MODEL, tn), lambda j: (layer, 0, j)),
            pl.BlockSpec((None, 1, tn), lambda j: (layer, 0, j)),
        ],
        out_specs=pl.BlockSpec((8, tn), lambda j: (0, j)),
        out_shape=jax.ShapeDtypeStruct((8, N_MOD), F32),
        compiler_params=_params("parallel"),
        name="modulation",
    )(cond8, w_mod, b_mod3)


def _row_parts(x, tm, tn=None):
    def spec(row_fn, col_fn):
        if tn is None:
            return pl.BlockSpec((tm, D_MODEL), lambda i: (row_fn(i), 0))
        return pl.BlockSpec((tm, tn), lambda i, j: (row_fn(i), col_fn(i, j)))

    if not isinstance(x, tuple):
        return [x], [spec(lambda i: i, lambda i, j: j)]
    n_a = x[0].shape[0] // tm
    return list(x), [
        spec(lambda i: jnp.minimum(i, n_a - 1), lambda i, j: jnp.where(i < n_a, j, 0)),
        spec(lambda i: jnp.maximum(i - n_a, 0), lambda i, j: jnp.where(i < n_a, 0, j)),
    ]


def _read_rows(refs, tm, rows=slice(None)):
    if len(refs) == 1:
        return refs[0][rows, :]
    n_a = N_CTX // tm
    return jnp.where(pl.program_id(0) < n_a, refs[0][rows, :], refs[1][rows, :])


def _rmsnorm_rows(x_refs, o_ref, scale, shift):
    tm = o_ref.shape[0]

    def body(c, carry):
        rows = pl.ds(pl.multiple_of(c * NORM_CHUNK, NORM_CHUNK), NORM_CHUNK)
        x = _read_rows(x_refs, tm, rows)
        y = x * lax.rsqrt(jnp.mean(x * x, axis=-1, keepdims=True) + EPS) * scale
        if shift is not None:
            y = y + shift
        o_ref[rows, :] = y.astype(o_ref.dtype)
        return carry

    lax.fori_loop(0, tm // NORM_CHUNK, body, 0, unroll=4)


def _norm_mod_kernel(*refs):
    *x_refs, g_ref, sc_ref, sh_ref, o_ref = refs
    _rmsnorm_rows(x_refs, o_ref, g_ref[...] * (1.0 + sc_ref[...]), sh_ref[...])


def norm_mod(x, g, sc, sh):
    tm = NORM_TILE
    x_args, x_specs = _row_parts(x, tm)
    return pl.pallas_call(
        _norm_mod_kernel,
        grid=(N_TOK // tm,),
        in_specs=x_specs + [
            pl.BlockSpec((1, D_MODEL), lambda i: (0, 0)),
            pl.BlockSpec((None, 1, D_MODEL), lambda i: (_cond_of_tile(i, tm), 0, 0)),
            pl.BlockSpec((None, 1, D_MODEL), lambda i: (_cond_of_tile(i, tm), 0, 0)),
        ],
        out_specs=pl.BlockSpec((tm, D_MODEL), lambda i: (i, 0)),
        out_shape=jax.ShapeDtypeStruct((N_TOK, D_MODEL), BF16),
        compiler_params=_params("parallel"),
        name="norm_mod",
    )(*x_args, g.reshape(1, D_MODEL), sc, sh)


def _final_norm_kernel(x_ref, g_ref, o_ref):
    _rmsnorm_rows([x_ref], o_ref, g_ref[...], None)


def final_norm(x, g, row0, n_rows):
    tm = NORM_TILE
    off = row0 // tm
    return pl.pallas_call(
        _final_norm_kernel,
        grid=(n_rows // tm,),
        in_specs=[
            pl.BlockSpec((tm, D_MODEL), lambda i: (i + off, 0)),
            pl.BlockSpec((1, D_MODEL), lambda i: (0, 0)),
        ],
        out_specs=pl.BlockSpec((tm, D_MODEL), lambda i: (i, 0)),
        out_shape=jax.ShapeDtypeStruct((n_rows, D_MODEL), F32),
        compiler_params=_params("parallel"),
        name="final_norm",
    )(x, g.reshape(1, D_MODEL))


def _w16(w_ref):
    return w_ref[...].astype(BF16)


def _mm_kernel(x_ref, w_ref, o_ref):
    o_ref[...] = _dot(x_ref[...], _w16(w_ref)).astype(o_ref.dtype)


def matmul(x, w, layer, *, tm, tn, out_dtype, row0=0, m=None):
    k = x.shape[1]
    m = x.shape[0] if m is None else m
    n = w.shape[-1]
    r_off = row0 // tm
    return pl.pallas_call(
        _mm_kernel,
        grid=(m // tm, n // tn),
        in_specs=[
            pl.BlockSpec((tm, k), lambda i, j: (r_off + i, 0)),
            pl.BlockSpec((None, k, tn), lambda i, j: (layer, 0, j)),
        ],
        out_specs=pl.BlockSpec((tm, tn), lambda i, j: (i, j)),
        out_shape=jax.ShapeDtypeStruct((m, n), out_dtype),
        compiler_params=_params("parallel", "arbitrary"),
        name="matmul",
    )(x, w)


def _mm_nt_kernel(x_ref, w_ref, w2_ref, o_ref, o2_ref):
    x = x_ref[...]
    o_ref[...] = _dot_nt(x, _w16(w_ref)).astype(o_ref.dtype)

    @pl.when(pl.program_id(1) == 0)
    def _():
        o2_ref[...] = _dot(x, w2_ref[...])


def matmul_nt(x, w_t, w2, layer, *, tm, tn, n):
    m, k = x.shape
    n2 = w2.shape[-1]
    return pl.pallas_call(
        _mm_nt_kernel,
        grid=(m // tm, n // tn),
        in_specs=[
            pl.BlockSpec((tm, k), lambda i, j: (i, 0)),
            pl.BlockSpec((None, tn, k), lambda i, j: (layer, j, 0)),
            pl.BlockSpec((None, k, n2), lambda i, j: (layer, 0, 0)),
        ],
        out_specs=[
            pl.BlockSpec((tm, tn), lambda i, j: (i, j)),
            pl.BlockSpec((tm, n2), lambda i, j: (i, 0)),
        ],
        out_shape=[
            jax.ShapeDtypeStruct((m, n), F32),
            jax.ShapeDtypeStruct((m, n2), F32),
        ],
        compiler_params=_params("parallel", "arbitrary"),
        name="matmul_nt",
    )(x, w_t, w2)


def _mm_res_kernel(x_ref, w_ref, *refs, mix_cols):
    *r_refs, g_ref, o_ref = refs
    w = _w16(w_ref)
    if mix_cols:
        acc = (_dot(x_ref[:, :MIX_A_COL], w[LRU_WIDTH:]) + _dot(x_ref[:, MIX_A_COL:], w[:LRU_WIDTH]))
    else:
        acc = _dot(x_ref[...], w)
    o_ref[...] = _read_rows(r_refs, o_ref.shape[0]) + g_ref[...] * acc


def matmul_gated_residual(x, w, layer, res, gate, *, tm, tn, mix_cols=False):
    m, k = x.shape
    n = w.shape[-1]
    r_args, r_specs = _row_parts(res, tm, tn)
    return pl.pallas_call(
        functools.partial(_mm_res_kernel, mix_cols=mix_cols),
        grid=(m // tm, n // tn),
        in_specs=[
            pl.BlockSpec((tm, k), lambda i, j: (i, 0)),
            pl.BlockSpec((None, k, tn), lambda i, j: (layer, 0, j)),
        ] + r_specs + [
            pl.BlockSpec((None, 1, tn), lambda i, j: (_cond_of_tile(i, tm), 0, j)),
        ],
        out_specs=pl.BlockSpec((tm, tn), lambda i, j: (i, j)),
        out_shape=jax.ShapeDtypeStruct((m, n), F32),
        compiler_params=_params("parallel", "arbitrary"),
        name="matmul_gated_residual",
    )(x, w, *r_args, gate)


def _swiglu_kernel(x_ref, wg_ref, wu_ref, wd_ref, *refs, mod_steps):
    if mod_steps:
        c_ref, wm_ref, bm_ref, o_ref, wd16_ref, mod_ref = refs
    else:
        o_ref, wd16_ref = refs
    x = x_ref[...]
    g = _dot(x, _w16(wg_ref))
    u = _dot(x, _w16(wu_ref))
    o_ref[...] = (_silu(g) * u).astype(o_ref.dtype)
    wd16_ref[...] = wd_ref[...].astype(BF16)
    if mod_steps:
        @pl.when(pl.program_id(0) * pl.num_programs(1) + pl.program_id(1) < mod_steps)
        def _():
            mod_ref[...] = _mod_tile(c_ref, wm_ref, bm_ref)


def swiglu_up(x, w_gate, w_up, w_down, layer, *, tm, tn, mod_args=None):
    m, k = x.shape
    n = w_gate.shape[-1]
    nj = n // tn
    steps = (m // tm) * nj
    kd, nd = w_down.shape[1:]
    slab = kd // steps
    assert slab * steps == kd and slab % 16 == 0
    in_specs = [
        pl.BlockSpec((tm, k), lambda i, j: (i, 0), pipeline_mode=pl.Buffered(1)),
        pl.BlockSpec((None, k, tn), lambda i, j: (layer, 0, j)),
        pl.BlockSpec((None, k, tn), lambda i, j: (layer, 0, j)),
        pl.BlockSpec((None, slab, nd), lambda i, j: (layer, i * nj + j, 0)),
    ]
    out_specs = [
        pl.BlockSpec((tm, tn), lambda i, j: (i, j)),
        pl.BlockSpec((None, slab, nd), lambda i, j: (0, i * nj + j, 0)),
    ]
    out_shape = [
        jax.ShapeDtypeStruct((m, n), BF16),
        jax.ShapeDtypeStruct((1, kd, nd), BF16),
    ]
    args = [x, w_gate, w_up, w_down]
    mod_steps = 0
    if mod_args is not None:
        mod_steps = N_MOD // MOD_SLAB
        assert mod_steps <= steps
        slab_of = lambda i, j: jnp.minimum(i * nj + j, mod_steps - 1)
        in_specs += [
            pl.BlockSpec((8, D_MODEL), lambda i, j: (0, 0)),
            pl.BlockSpec((None, D_MODEL, MOD_SLAB), lambda i, j: (layer + 1, 0, slab_of(i, j))),
            pl.BlockSpec((None, 1, MOD_SLAB), lambda i, j: (layer + 1, 0, slab_of(i, j))),
        ]
        out_specs.append(pl.BlockSpec((8, MOD_SLAB), lambda i, j: (0, slab_of(i, j))))
        out_shape.append(jax.ShapeDtypeStruct((8, N_MOD), F32))
        args += list(mod_args)
    return pl.pallas_call(
        functools.partial(_swiglu_kernel, mod_steps=mod_steps),
        grid=(m // tm, nj),
        in_specs=in_specs,
        out_specs=out_specs,
        out_shape=out_shape,
        compiler_params=_params("arbitrary", "arbitrary"),
        name="swiglu_up",
    )(*args)


def _rope(x, c, se, so):
    nxt = pltpu.roll(x, LANE - 1, 1)
    prv = pltpu.roll(x, 1, 1)
    return x * c + nxt * se + prv * so


def _mm_rope_kernel(x_ref, w_ref, c_ref, se_ref, so_ref, o_ref, *, first_rope_tile):
    acc = _dot(x_ref[...], _w16(w_ref))
    j = pl.program_id(1)

    @pl.when(j < first_rope_tile)
    def _():
        o_ref[...] = acc.astype(o_ref.dtype)

    @pl.when(j >= first_rope_tile)
    def _():
        c, se, so = c_ref[...], se_ref[...], so_ref[...]
        for h in range(acc.shape[1] // LANE):
            sl = slice(h * LANE, (h + 1) * LANE)
            o_ref[:, sl] = _rope(acc[:, sl], c, se, so).astype(o_ref.dtype)


def matmul_rope_tail(x, w, layer, tables, *, tm, tn, first_rope_tile):
    m, k = x.shape
    n = w.shape[-1]
    nt = DEC_SEQ // tm
    tab = pl.BlockSpec((tm, LANE), lambda i, j: (i % nt, 0))
    return pl.pallas_call(
        functools.partial(_mm_rope_kernel, first_rope_tile=first_rope_tile),
        grid=(m // tm, n // tn),
        in_specs=[
            pl.BlockSpec((tm, k), lambda i, j: (i, 0)),
            pl.BlockSpec((None, k, tn), lambda i, j: (layer, 0, j)),
            tab, tab, tab,
        ],
        out_specs=pl.BlockSpec((tm, tn), lambda i, j: (i, j)),
        out_shape=jax.ShapeDtypeStruct((m, n), BF16),
        compiler_params=_params("parallel", "arbitrary"),
        name="matmul_rope_tail",
    )(x, w, *tables)


def _head_rmsnorm(x, g):
    return x * lax.rsqrt(jnp.mean(x * x, axis=-1, keepdims=True) + EPS) * g


def _prep_kernel(*refs, rope):
    if rope:
        (qk_ref, v_ref, c_ref, kr_ref, qn_ref, kn_ref, mqn_ref, mkvn_ref,
         cb_ref, seb_ref, sob_ref, cc_ref, sec_ref, soc_ref,
         _k_buf, _v_buf, _ckv_buf, _kr_buf,
         q_o, k_o, v_o, cq_o, ckv_o, kr_o) = refs
    else:
        (qk_ref, v_ref, c_ref, kr_ref, qn_ref, kn_ref, mqn_ref, mkvn_ref,
         q_o, k_o, v_o, cq_o, ckv_o, kr_o, kf_o, ckvf_o) = refs
    qn, kn = qn_ref[...], kn_ref[...]
    if rope:
        cb, seb, sob = cb_ref[...], seb_ref[...], sob_ref[...]
    for h in range(GQA_HEADS + GQA_KV_HEADS):
        sl = slice(h * LANE, (h + 1) * LANE)
        is_q = h < GQA_HEADS
        y = _head_rmsnorm(qk_ref[:, sl], qn if is_q else kn)
        osl = sl if is_q else slice((h - GQA_HEADS) * LANE, (h - GQA_HEADS + 1) * LANE)
        if not rope and not is_q:
            kf_o[:, osl] = y
        if rope:
            y = _rope(y, cb, seb, sob)
        (q_o if is_q else k_o)[:, osl] = y.astype(BF16)
    v_o[...] = v_ref[...].astype(BF16)
    cq = c_ref[:, :MLA_Q_RANK]
    cq_o[...] = _head_rmsnorm(cq, mqn_ref[...]).astype(BF16)
    ckv = _head_rmsnorm(c_ref[:, MLA_Q_RANK:], mkvn_ref[...])
    ckv_o[...] = ckv.astype(BF16)
    if not rope:
        ckvf_o[...] = ckv
    kr = kr_ref[...]
    if rope:
        kr = _rope(kr, cc_ref[...], sec_ref[...], soc_ref[...])
    kr_o[...] = kr.astype(BF16)


def prep(p, kr, gqa_qn, gqa_kn, mla_qn, mla_kvn, *, latent, tables_b=None, tables_c=None,
         layer=None, key_bufs=None):
    tm = ROW_TILE
    n_rows = N_LAT if latent else N_CTX
    off = (N_CTX // tm) if latent else 0
    nt = DEC_SEQ // tm
    row = lambda c: (lambda i: (i + off, c))
    in_specs = [
        pl.BlockSpec((tm, 2048), row(1)),
        pl.BlockSpec((tm, KV_WIDTH), row(8)),
        pl.BlockSpec((tm, 1536), row(3)),
        pl.BlockSpec((tm, LANE), row(0)),
        pl.BlockSpec((1, HEAD_DIM), lambda i: (0, 0)),
        pl.BlockSpec((1, HEAD_DIM), lambda i: (0, 0)),
        pl.BlockSpec((1, MLA_Q_RANK), lambda i: (0, 0)),
        pl.BlockSpec((1, MLA_KV_RANK), lambda i: (0, 0)),
    ]
    args = [p, p, p, kr, gqa_qn.reshape(1, -1), gqa_kn.reshape(1, -1),
            mla_qn.reshape(1, -1), mla_kvn.reshape(1, -1)]
    out = lambda w, dt: (pl.BlockSpec((tm, w), lambda i: (i, 0)), jax.ShapeDtypeStruct((n_rows, w), dt))
    outs = [out(GQA_WIDTH, BF16), out(KV_WIDTH, BF16), out(KV_WIDTH, BF16),
            out(MLA_Q_RANK, BF16), out(MLA_KV_RANK, BF16), out(LANE, BF16)]
    aliases = {}
    if latent:
        tab = pl.BlockSpec((tm, LANE), lambda i: (i % nt, 0))
        in_specs += [tab] * 6
        args += list(tables_b) + list(tables_c)
        blocks_per_req = LAT_KEYS // tm

        def key_rows(i):
            return ((layer * DEC_BATCH + i // nt) * blocks_per_req + PAST_LEN // tm + i % nt, 0)

        for n_out, buf in zip((1, 2, 4, 5), key_bufs):
            aliases[len(args)] = n_out
            in_specs.append(pl.BlockSpec(memory_space=pl.ANY))
            args.append(buf)
            outs[n_out] = (pl.BlockSpec((tm, buf.shape[1]), key_rows), jax.ShapeDtypeStruct(buf.shape, buf.dtype))
    else:
        outs += [out(KV_WIDTH, F32), out(MLA_KV_RANK, F32)]
    return pl.pallas_call(
        functools.partial(_prep_kernel, rope=latent),
        grid=(n_rows // tm,),
        in_specs=in_specs,
        out_specs=[o[0] for o in outs],
        out_shape=[o[1] for o in outs],
        input_output_aliases=aliases,
        compiler_params=_params("parallel"),
        name="prep_latent" if latent else "prep_context",
    )(*args)


def _gelu_tanh(x):
    return 0.5 * x * (1.0 + jnp.tanh(0.7978845608028654 * (x + 0.044715 * (x * x * x))))


def _softplus(z):
    return jnp.maximum(z, 0.0) + jnp.log1p(jnp.exp(-jnp.abs(z)))


def _lru_kernel(xa_ref, ga_ref, cw_ref, cb_ref, w_ref, p_ref, h0_ref, mix_ref, o_ref, st_ref,
                a_scr, b_scr, hf_scr, hb_scr, *, t_sub, tiles_per_iter):
    del mix_ref
    rows, c = xa_ref.shape
    n_sub = rows // t_sub
    x = xa_ref[...]
    tpos = lax.broadcasted_iota(jnp.int32, x.shape, 0) % t_sub
    cw = cw_ref[...]
    xm2 = jnp.where(tpos >= 2, pltpu.roll(x, 2, 0), 0.0)
    xm1 = jnp.where(tpos >= 1, pltpu.roll(x, 1, 0), 0.0)
    xp1 = jnp.where(tpos < t_sub - 1, pltpu.roll(x, rows - 1, 0), 0.0)
    xc = cb_ref[...] + xm2 * cw[0:1] + xm1 * cw[1:2] + x * cw[2:3] + xp1 * cw[3:4]
    xc16 = xc.astype(BF16)

    for d in range(2):
        g = _dot(xc16, w_ref[d])
        prm = p_ref[d]
        r = jax.nn.sigmoid(g[:, :c] + prm[0:1])
        ig = jax.nn.sigmoid(g[:, c:] + prm[1:2])
        log_a = (-LRU_C) * r * _softplus(-prm[2:3])
        a = jnp.exp(log_a)
        a_scr[d] = a
        b_scr[d] = jnp.sqrt(-jnp.tanh(log_a) * (a * a + 1.0)) * (ig * xc)

    sub_row = lax.broadcasted_iota(jnp.int32, (8, c), 0)
    keep_f = [sub_row >= s for s in (1, 2, 4)]
    keep_b = [sub_row < 8 - s for s in (1, 2, 4)]

    def tile_scan(a, b, carry, reverse):
        for n, s in enumerate((1, 2, 4)):
            keep, shift = (keep_b[n], 8 - s) if reverse else (keep_f[n], s)
            a_prev = jnp.where(keep, pltpu.roll(a, shift, 0), 1.0)
            b_prev = jnp.where(keep, pltpu.roll(b, shift, 0), 0.0)
            b = a * b_prev + b
            a = a * a_prev
        h = a * carry + b
        last = h[0:1] if reverse else h[7:8]
        return h, jnp.broadcast_to(last, h.shape)

    def body(j, hs):
        hs = list(hs)
        for u in range(tiles_per_iter):
            off = pl.multiple_of((j * tiles_per_iter + u) * 8, 8)
            for q in range(n_sub):
                idx_f = pl.ds(q * t_sub + off, 8)
                idx_b = pl.ds(q * t_sub + (t_sub - 8) - off, 8)
                hf, hs[2 * q] = tile_scan(a_scr[0, idx_f, :], b_scr[0, idx_f, :], hs[2 * q], False)
                hb, hs[2 * q + 1] = tile_scan(a_scr[1, idx_b, :], b_scr[1, idx_b, :], hs[2 * q + 1], True)
                hf_scr[idx_f, :] = hf
                hb_scr[idx_b, :] = hb
        return tuple(hs)

    init = tuple(jnp.broadcast_to(h0_ref[q, d:d + 1, :], (8, c)) for q in range(n_sub) for d in range(2))
    hs = lax.fori_loop(0, t_sub // (8 * tiles_per_iter), body, init)
    for q in range(n_sub):
        st_ref[q, 0:1, :] = hs[2 * q][0:1]
        st_ref[q, 1:2, :] = hs[2 * q + 1][0:1]
    o_ref[...] = ((hf_scr[...] + hb_scr[...]) * _gelu_tanh(ga_ref[...])).astype(o_ref.dtype)


def lru_mixer(p, conv_w, conv_b, w_gates, lru_prm, h0, mix, layer, *, latent):
    c = LRU_CHUNK
    rows = LRU_ROWS
    t_sub = DEC_SEQ if latent else SEQ
    n_sub = rows // t_sub
    n_rows = N_LAT if latent else N_CTX
    off = (N_CTX // rows) if latent else 0
    nck = LRU_WIDTH // c
    return pl.pallas_call(
        functools.partial(_lru_kernel, t_sub=t_sub, tiles_per_iter=4 if latent else 1),
        grid=(n_rows // rows, nck),
        in_specs=[
            pl.BlockSpec((rows, c), lambda s, k: (s + off, k)),
            pl.BlockSpec((rows, c), lambda s, k: (s + off, nck + k)),
            pl.BlockSpec((None, 4, c), lambda s, k: (layer, 0, k)),
            pl.BlockSpec((None, 1, c), lambda s, k: (layer, 0, k)),
            pl.BlockSpec((None, 2, None, c, 2 * c), lambda s, k: (layer, 0, k, 0, 0)),
            pl.BlockSpec((None, 2, 3, c), lambda s, k: (layer, 0, 0, k)),
            pl.BlockSpec((n_sub, 2, c), lambda s, k: (s, 0, k)),
            pl.BlockSpec(memory_space=pl.ANY),
        ],
        out_specs=[
            pl.BlockSpec((rows, c), lambda s, k: (s + off, MIX_A_COL // c + k)),
            pl.BlockSpec((n_sub, 2, c), lambda s, k: (s, 0, k)),
        ],
        out_shape=[
            jax.ShapeDtypeStruct(mix.shape, mix.dtype),
            jax.ShapeDtypeStruct((n_rows // t_sub, 2, LRU_WIDTH), F32),
        ],
        input_output_aliases={7: 0},
        scratch_shapes=[pltpu.VMEM((2, rows, c), F32)] * 2 + [pltpu.VMEM((rows, c), F32)] * 2,
        compiler_params=_params("parallel", "parallel"),
        name="lru_latent" if latent else "lru_context",
    )(p, p, conv_w, conv_b.reshape(DEPTH, 1, LRU_WIDTH), w_gates, lru_prm, h0, mix)


LOG2E = 1.4426950408889634


def _softmax_pv(s, v, scale):
    t = s * (scale * LOG2E)
    m = jnp.max(t, axis=-1, keepdims=True)
    p = jnp.exp2(t - m).astype(BF16)
    oa = _dot(p, jnp.concatenate([v, jnp.ones_like(v)], axis=1))
    return oa[:, :LANE] / oa[:, LANE:]


def _gqa_kernel(q_ref, k_ref, v_ref, mix_ref, o_ref, *, scale, groups, sub):
    del mix_ref
    tq = q_ref.shape[0]
    for g in range(groups):
        kv_sl = slice(g * LANE, (g + 1) * LANE)
        k, v = k_ref[:, kv_sl], v_ref[:, kv_sl]
        for h in range(g * GQA_REP, (g + 1) * GQA_REP):
            sl = slice(h * LANE, (h + 1) * LANE)
            for r0 in range(0, tq, sub):
                s = _dot_nt(q_ref[r0:r0 + sub, sl], k)
                o_ref[r0:r0 + sub, sl] = _softmax_pv(s, v, scale).astype(o_ref.dtype)


def gqa_attention(q, k, v, mix, row0, *, n_batch, t_q, t_k, tq, groups, sub, kv_b0=0):
    nq = t_q // tq
    qw = groups * GQA_REP * HEAD_DIM
    kw = groups * HEAD_DIM
    r_off, c_off = row0 // tq, MIX_B_COL // qw
    return pl.pallas_call(
        functools.partial(_gqa_kernel, scale=HEAD_DIM ** -0.5, groups=groups, sub=sub),
        grid=(n_batch, GQA_KV_HEADS // groups, nq),
        in_specs=[
            pl.BlockSpec((tq, qw), lambda b, g, i: (b * nq + i, g)),
            pl.BlockSpec((None, t_k, kw), lambda b, g, i: (kv_b0 + b, 0, g)),
            pl.BlockSpec((None, t_k, kw), lambda b, g, i: (kv_b0 + b, 0, g)),
            pl.BlockSpec(memory_space=pl.ANY),
        ],
        out_specs=pl.BlockSpec((tq, qw), lambda b, g, i: (r_off + b * nq + i, c_off + g)),
        out_shape=jax.ShapeDtypeStruct(mix.shape, mix.dtype),
        input_output_aliases={3: 0},
        compiler_params=_params("parallel", "parallel", "arbitrary"),
        name="gqa_attention",
    )(q, k, v, mix)


def _mla_kernel(qn_ref, qr_ref, kv_ref, kr_ref, mix_ref, o_ref, *, scale, heads, sub):
    del mix_ref
    tq = qn_ref.shape[0]
    kr = kr_ref[...]
    for h in range(heads):
        sl = slice(h * LANE, (h + 1) * LANE)
        k0 = h * (MLA_NOPE + MLA_V)
        k = jnp.concatenate([kv_ref[:, k0:k0 + MLA_NOPE], kr], axis=1)
        v = kv_ref[:, k0 + MLA_NOPE:k0 + MLA_NOPE + MLA_V]
        for r0 in range(0, tq, sub):
            q = jnp.concatenate([qn_ref[r0:r0 + sub, sl], qr_ref[r0:r0 + sub, sl]], axis=1)
            o_ref[r0:r0 + sub, sl] = _softmax_pv(_dot_nt(q, k), v, scale).astype(o_ref.dtype)


def mla_attention(q, kv, kr, mix, row0, *, n_batch, t_q, t_k, tq, heads, sub, kr_b0=0):
    nq = t_q // tq
    nh = MLA_HEADS // heads
    ow = heads * MLA_V
    r_off, c_off = row0 // tq, MIX_C_COL // ow
    return pl.pallas_call(
        functools.partial(_mla_kernel, scale=(MLA_NOPE + MLA_ROPE) ** -0.5, heads=heads, sub=sub),
        grid=(n_batch, nh, nq),
        in_specs=[
            pl.BlockSpec((tq, heads * MLA_NOPE), lambda b, h, i: (b * nq + i, h)),
            pl.BlockSpec((tq, heads * LANE), lambda b, h, i: (b * nq + i, nh + h)),
            pl.BlockSpec((None, t_k, heads * (MLA_NOPE + MLA_V)), lambda b, h, i: (b, 0, h)),
            pl.BlockSpec((None, t_k, LANE), lambda b, h, i: (kr_b0 + b, 0, 0)),
            pl.BlockSpec(memory_space=pl.ANY),
        ],
        out_specs=pl.BlockSpec((tq, ow), lambda b, h, i: (r_off + b * nq + i, c_off + h)),
        out_shape=jax.ShapeDtypeStruct(mix.shape, mix.dtype),
        input_output_aliases={4: 0},
        compiler_params=_params("parallel", "parallel", "arbitrary"),
        name="mla_attention",
    )(q, q, kv, kr, mix)


def _rope_tables(rot_dim):
    rows = DEC_SEQ // GRID_W
    row = jnp.repeat(jnp.arange(rows, dtype=F32), GRID_W)
    col = jnp.tile(jnp.arange(GRID_W, dtype=F32), rows)
    quarter = rot_dim // 4
    freqs = ROPE_THETA ** (-jnp.arange(quarter, dtype=F32) / quarter)
    ang = jnp.concatenate([row[:, None] * freqs, col[:, None] * freqs], axis=-1)
    cos, sin = jnp.cos(ang), jnp.sin(ang)
    zero = jnp.zeros_like(sin)
    c = jnp.repeat(cos, 2, axis=-1)
    se = jnp.stack([-sin, zero], axis=-1).reshape(DEC_SEQ, rot_dim)
    so = jnp.stack([zero, sin], axis=-1).reshape(DEC_SEQ, rot_dim)
    pad = ((0, 0), (0, LANE - rot_dim))
    return tuple(jnp.pad(t, pad) for t in (c, se, so))


def _block_diag_gates(lru_wr, lru_wi):
    per = LRU_CHUNK // LRU_BLOCK
    nck = LRU_WIDTH // LRU_CHUNK
    eye = jnp.eye(per, dtype=F32)

    def bd(w):
        w = w.reshape(DEPTH, 2, nck, per, LRU_BLOCK, LRU_BLOCK)
        w = jnp.einsum("dzcakj,ab->dzcakbj", w, eye)
        return w.reshape(DEPTH, 2, nck, LRU_CHUNK, LRU_CHUNK)

    return jnp.concatenate([bd(lru_wr), bd(lru_wi)], axis=-1).astype(BF16)


def _key_buffer(cache, width):
    c = jnp.swapaxes(cache, 0, 1).astype(BF16)
    c = jnp.pad(c, ((0, 0), (0, 0), (0, DEC_SEQ), (0, width - cache.shape[-1])))
    return c.reshape(DEPTH * DEC_BATCH * LAT_KEYS, width)


def _permute_w_uq(w_uq):
    w = w_uq.reshape(DEPTH, MLA_Q_RANK, MLA_HEADS, MLA_NOPE + MLA_ROPE)
    nope = w[..., :MLA_NOPE].reshape(DEPTH, MLA_Q_RANK, MLA_HEADS * MLA_NOPE)
    rope = jnp.pad(w[..., MLA_NOPE:], ((0, 0), (0, 0), (0, 0), (0, LANE - MLA_ROPE)))
    rope = rope.reshape(DEPTH, MLA_Q_RANK, MLA_HEADS * LANE)
    return jnp.concatenate([nope, rope], axis=-1).astype(BF16)


def kernel(x_prompt, x_sample, state_lru, cache_gqa_k, cache_gqa_v, cache_mla_ckv, cache_mla_krope, c,
           c_ctx, w_mod, b_mod, norm_mix_g, norm_ffn_g, w_in, conv_w, conv_b, lru_wr, lru_br, lru_wi, lru_bi,
           lru_lam, gqa_qn, gqa_kn, mla_qn, mla_kvn, w_uq, w_ukv, w_out, w_gate, w_up, w_down, norm_f):
    x = (x_prompt.reshape(N_CTX, D_MODEL), x_sample.reshape(N_LAT, D_MODEL))

    cond8 = jnp.zeros((8, D_MODEL), F32).at[0].set(c_ctx).at[1:1 + DEC_BATCH].set(c)
    b_mod3 = b_mod.reshape(DEPTH, 1, N_MOD)
    mod = modulation(cond8, w_mod, b_mod3, 0)

    w_kr16 = jnp.pad(w_in[:, :, IN_MAIN:], ((0, 0), (0, 0), (0, LANE - MLA_ROPE))).astype(BF16)
    w_in_t = jnp.swapaxes(w_in, 1, 2)
    w_uq16 = _permute_w_uq(w_uq)
    w_gates = _block_diag_gates(lru_wr, lru_wi)
    lru_prm = jnp.stack([lru_br, lru_bi, lru_lam], axis=2)
    tables_b = _rope_tables(HEAD_DIM)
    tables_c = _rope_tables(MLA_ROPE)
    h0_ctx = jnp.zeros((BATCH, 2, LRU_WIDTH), F32)
    key_bufs = (_key_buffer(cache_gqa_k.reshape(DEC_BATCH, DEPTH, PAST_LEN, KV_WIDTH), KV_WIDTH),
                _key_buffer(cache_gqa_v.reshape(DEC_BATCH, DEPTH, PAST_LEN, KV_WIDTH), KV_WIDTH),
                _key_buffer(cache_mla_ckv, MLA_KV_RANK),
                _key_buffer(cache_mla_krope, LANE))

    mix = jnp.zeros((N_TOK, D_MIX), BF16)

    st_lru, st_k, st_v, st_ckv, st_kr = [], [], [], [], []
    for l in range(DEPTH):
        mod6 = mod.reshape(8, 6, 1, D_MODEL)
        sh1, sc1, g1, sh2, sc2, g2 = (mod6[:, s] for s in range(6))

        h = norm_mod(x, norm_mix_g[l], sc1, sh1)
        p, kr = matmul_nt(h, w_in_t, w_kr16, l, tm=1024, tn=512, n=IN_MAIN)

        mix, s_lru = lru_mixer(p, conv_w, conv_b, w_gates, lru_prm, h0_ctx, mix, l, latent=False)
        mix, _ = lru_mixer(p, conv_w, conv_b, w_gates, lru_prm, state_lru[:, l], mix, l, latent=True)

        (q_c, k_c, v_c, cq_c, ckv_c, kr_c, kf_c, ckvf_c) = prep(
            p, kr, gqa_qn[l], gqa_kn[l], mla_qn[l], mla_kvn[l], latent=False)
        q_l, k_buf, v_buf, cq_l, ckv_buf, kr_buf = prep(
            p, kr, gqa_qn[l], gqa_kn[l], mla_qn[l], mla_kvn[l], latent=True,
            tables_b=tables_b, tables_c=tables_c, layer=l, key_bufs=key_bufs)
        key_bufs = (k_buf, v_buf, ckv_buf, kr_buf)
        req0 = l * DEC_BATCH

        mix = gqa_attention(q_c, k_c.reshape(BATCH, SEQ, KV_WIDTH), v_c.reshape(BATCH, SEQ, KV_WIDTH), mix, 0,
                            n_batch=BATCH, t_q=SEQ, t_k=SEQ, tq=SEQ, groups=GQA_KV_HEADS, sub=SEQ)
        mix = gqa_attention(q_l, k_buf.reshape(-1, LAT_KEYS, KV_WIDTH), v_buf.reshape(-1, LAT_KEYS, KV_WIDTH),
                            mix, N_CTX, n_batch=DEC_BATCH, t_q=DEC_SEQ, t_k=LAT_KEYS, tq=1024, groups=1, sub=128,
                            kv_b0=req0)

        qm_c = matmul(cq_c, w_uq16, l, tm=2048, tn=1024, out_dtype=BF16)
        qm_l = matmul_rope_tail(cq_l, w_uq16, l, tables_c, tm=1024, tn=MLA_WIDTH, first_rope_tile=1)
        kv_c = matmul(ckv_c, w_ukv, l, tm=2048, tn=1024, out_dtype=BF16)
        kv_l = matmul(ckv_buf, w_ukv, l, tm=LAT_KEYS, tn=1024, out_dtype=BF16,
                      row0=req0 * LAT_KEYS, m=DEC_BATCH * LAT_KEYS)
        mix = mla_attention(qm_c, kv_c.reshape(BATCH, SEQ, -1), kr_c.reshape(BATCH, SEQ, LANE), mix, 0,
                            n_batch=BATCH, t_q=SEQ, t_k=SEQ, tq=SEQ, heads=MLA_HEADS, sub=SEQ)
        mix = mla_attention(qm_l, kv_l.reshape(DEC_BATCH, LAT_KEYS, -1), kr_buf.reshape(-1, LAT_KEYS, LANE),
                            mix, N_CTX, n_batch=DEC_BATCH, t_q=DEC_SEQ, t_k=LAT_KEYS, tq=2048, heads=1, sub=256,
                            kr_b0=req0)

        x = matmul_gated_residual(mix, w_out, l, x, g1, tm=1024, tn=512, mix_cols=True)

        h = norm_mod(x, norm_ffn_g[l], sc2, sh2)
        if l + 1 < DEPTH:
            ff, w_down16, mod = swiglu_up(h, w_gate, w_up, w_down, l, tm=2048, tn=256,
                                          mod_args=(cond8, w_mod, b_mod3))
        else:
            ff, w_down16 = swiglu_up(h, w_gate, w_up, w_down, l, tm=2048, tn=256)
        x = matmul_gated_residual(ff, w_down16, 0, x, g2, tm=512, tn=512)

        st_lru.append(s_lru)
        st_k.append(kf_c.reshape(BATCH, SEQ, GQA_KV_HEADS, HEAD_DIM))
        st_v.append(p[:N_CTX, 4096:4096 + KV_WIDTH].reshape(BATCH, SEQ, GQA_KV_HEADS, HEAD_DIM))
        st_ckv.append(ckvf_c.reshape(BATCH, SEQ, MLA_KV_RANK))
        st_kr.append(kr[:N_CTX, :MLA_ROPE].reshape(BATCH, SEQ, MLA_ROPE))

    y_ctx = final_norm(x, norm_f, 0, N_CTX)
    y_lat = final_norm(x, norm_f, N_CTX, N_LAT)
    return (y_ctx.reshape(BATCH, SEQ, D_MODEL), y_lat.reshape(DEC_BATCH, DEC_SEQ, D_MODEL),
            jnp.stack(st_lru, axis=1), jnp.stack(st_k, axis=1), jnp.stack(st_v, axis=1),
            jnp.stack(st_ckv, axis=1), jnp.stack(st_kr, axis=1))
```

```python
import functools

import jax
import jax.numpy as jnp
from jax import lax
from jax.experimental import pallas as pl
from jax.experimental.pallas import tpu as pltpu

F32 = jnp.float32
BF16 = jnp.bfloat16

D_MODEL = 4096
BATCH = 16
SEQ = 256
DEPTH = 4
DEC_BATCH = 2
DEC_SEQ = 2048
PAST_LEN = 512
GRID_W = 64
EPS = 1e-6
ROPE_THETA = 10000.0

LRU_WIDTH = 1024
LRU_BLOCKS = 16
LRU_BLOCK = 64
LRU_C = 8.0
HEAD_DIM = 128
GQA_HEADS = 12
GQA_KV_HEADS = 4
GQA_REP = GQA_HEADS // GQA_KV_HEADS
GQA_WIDTH = GQA_HEADS * HEAD_DIM
KV_WIDTH = GQA_KV_HEADS * HEAD_DIM
MLA_HEADS = 12
MLA_Q_RANK = 1024
MLA_KV_RANK = 512
MLA_NOPE = 128
MLA_ROPE = 64
MLA_V = 128
MLA_WIDTH = MLA_HEADS * MLA_V
D_MIX = LRU_WIDTH + GQA_WIDTH + MLA_WIDTH
MIX_B_COL = 0
MIX_C_COL = GQA_WIDTH
MIX_A_COL = GQA_WIDTH + MLA_WIDTH
D_FF = 11008
IN_MAIN = 6144
LANE = 128
N_CTX = BATCH * SEQ
N_LAT = DEC_BATCH * DEC_SEQ
N_TOK = N_CTX + N_LAT
LAT_KEYS = PAST_LEN + DEC_SEQ
VMEM_LIMIT = 56 * 1024 * 1024
LRU_CHUNK = 256
LRU_ROWS = 2048
ROW_TILE = 256
NORM_TILE = 512
NORM_CHUNK = 16
SUBLANE = 8

T_IN = dict(tm=1024, tn=512)
T_OUT = dict(tm=1024, tn=512)
T_UP = dict(tm=2048, tn=256)
T_DOWN = dict(tm=512, tn=512)
T_MLA = dict(tm=2048, tn=1024)
A_GQA = dict(tq=1024, sub=128)
A_MLA = dict(tq=2048, sub=256)


def _params(*sem):
    return pltpu.CompilerParams(dimension_semantics=sem, vmem_limit_bytes=VMEM_LIMIT)


def _cond_of_tile(i, tm):
    row = i * tm
    return jnp.where(row < N_CTX, 0, 1 + (row - N_CTX) // DEC_SEQ)


def _silu(x):
    return x * jax.nn.sigmoid(x)


def _dot(a, b):
    return jnp.dot(a, b, preferred_element_type=F32)


def _dot_nt(a, b):
    return lax.dot_general(a, b, (((1,), (1,)), ((), ())), preferred_element_type=F32)


N_MOD = 6 * D_MODEL
MOD_SLAB = 256


def _mod_tile(c_ref, w_ref, b_ref):
    return _dot(_silu(c_ref[...]).astype(BF16), _w16(w_ref)) + b_ref[...]


def _mod_kernel(c_ref, w_ref, b_ref, o_ref):
    o_ref[...] = _mod_tile(c_ref, w_ref, b_ref)


def modulation(cond8, w_mod, b_mod3, layer):
    tn = 512
    return pl.pallas_call(
        _mod_kernel,
        grid=(N_MOD // tn,),
        in_specs=[
            pl.BlockSpec((SUBLANE, D_MODEL), lambda j: (0, 0)),
            pl.BlockSpec((None, D_MODEL, tn), lambda j: (layer, 0, j)),
            pl.BlockSpec((None, 1, tn), lambda j: (layer, 0, j)),
        ],
        out_specs=pl.BlockSpec((SUBLANE, tn), lambda j: (0, j)),
        out_shape=jax.ShapeDtypeStruct((SUBLANE, N_MOD), F32),
        compiler_params=_params("parallel"),
        name="modulation",
    )(cond8, w_mod, b_mod3)


def _row_parts(x, tm, tn=None):
    def spec(row_fn, col_fn):
        if tn is None:
            return pl.BlockSpec((tm, D_MODEL), lambda i: (row_fn(i), 0))
        return pl.BlockSpec((tm, tn), lambda i, j: (row_fn(i), col_fn(i, j)))

    if not isinstance(x, tuple):
        return [x], [spec(lambda i: i, lambda i, j: j)]
    n_a = x[0].shape[0] // tm
    return list(x), [
        spec(lambda i: jnp.minimum(i, n_a - 1), lambda i, j: jnp.where(i < n_a, j, 0)),
        spec(lambda i: jnp.maximum(i - n_a, 0), lambda i, j: jnp.where(i < n_a, 0, j)),
    ]


def _read_rows(refs, tm, rows=slice(None)):
    if len(refs) == 1:
        return refs[0][rows, :]
    n_a = N_CTX // tm
    return jnp.where(pl.program_id(0) < n_a, refs[0][rows, :], refs[1][rows, :])


def _rmsnorm_rows(x_refs, o_ref, scale, shift):
    tm = o_ref.shape[0]

    def body(c, carry):
        rows = pl.ds(pl.multiple_of(c * NORM_CHUNK, NORM_CHUNK), NORM_CHUNK)
        x = _read_rows(x_refs, tm, rows)
        y = x * lax.rsqrt(jnp.mean(x * x, axis=-1, keepdims=True) + EPS) * scale
        if shift is not None:
            y = y + shift
        o_ref[rows, :] = y.astype(o_ref.dtype)
        return carry

    lax.fori_loop(0, tm // NORM_CHUNK, body, 0, unroll=4)


def _norm_mod_kernel(*refs):
    *x_refs, g_ref, sc_ref, sh_ref, o_ref = refs
    _rmsnorm_rows(x_refs, o_ref, g_ref[...] * (1.0 + sc_ref[...]), sh_ref[...])


def norm_mod(x, g, sc, sh):
    tm = NORM_TILE
    x_args, x_specs = _row_parts(x, tm)
    return pl.pallas_call(
        _norm_mod_kernel,
        grid=(N_TOK // tm,),
        in_specs=x_specs + [
            pl.BlockSpec((1, D_MODEL), lambda i: (0, 0)),
            pl.BlockSpec((None, 1, D_MODEL), lambda i: (_cond_of_tile(i, tm), 0, 0)),
            pl.BlockSpec((None, 1, D_MODEL), lambda i: (_cond_of_tile(i, tm), 0, 0)),
        ],
        out_specs=pl.BlockSpec((tm, D_MODEL), lambda i: (i, 0)),
        out_shape=jax.ShapeDtypeStruct((N_TOK, D_MODEL), BF16),
        compiler_params=_params("parallel"),
        name="norm_mod",
    )(*x_args, g.reshape(1, D_MODEL), sc, sh)


def _final_norm_kernel(x_ref, g_ref, o_ref):
    _rmsnorm_rows([x_ref], o_ref, g_ref[...], None)


def final_norm(x, g, row0, n_rows):
    tm = NORM_TILE
    off = row0 // tm
    return pl.pallas_call(
        _final_norm_kernel,
        grid=(n_rows // tm,),
        in_specs=[
            pl.BlockSpec((tm, D_MODEL), lambda i: (i + off, 0)),
            pl.BlockSpec((1, D_MODEL), lambda i: (0, 0)),
        ],
        out_specs=pl.BlockSpec((tm, D_MODEL), lambda i: (i, 0)),
        out_shape=jax.ShapeDtypeStruct((n_rows, D_MODEL), F32),
        compiler_params=_params("parallel"),
        name="final_norm",
    )(x, g.reshape(1, D_MODEL))


def _w16(w_ref):
    return w_ref[...].astype(BF16)


def _mm_kernel(x_ref, w_ref, o_ref):
    o_ref[...] = _dot(x_ref[...], _w16(w_ref)).astype(o_ref.dtype)


def matmul(x, w, layer, *, tm, tn, out_dtype, row0=0, m=None):
    k = x.shape[1]
    m = x.shape[0] if m is None else m
    n = w.shape[-1]
    r_off = row0 // tm
    return pl.pallas_call(
        _mm_kernel,
        grid=(m // tm, n // tn),
        in_specs=[
            pl.BlockSpec((tm, k), lambda i, j: (r_off + i, 0)),
            pl.BlockSpec((None, k, tn), lambda i, j: (layer, 0, j)),
        ],
        out_specs=pl.BlockSpec((tm, tn), lambda i, j: (i, j)),
        out_shape=jax.ShapeDtypeStruct((m, n), out_dtype),
        compiler_params=_params("parallel", "arbitrary"),
        name="matmul",
    )(x, w)


def _mm_nt_kernel(x_ref, w_ref, w2_ref, o_ref, o2_ref):
    x = x_ref[...]
    o_ref[...] = _dot_nt(x, _w16(w_ref)).astype(o_ref.dtype)

    @pl.when(pl.program_id(1) == 0)
    def _():
        o2_ref[...] = _dot(x, w2_ref[...])


def matmul_nt(x, w_t, w2, layer, *, tm, tn, n):
    m, k = x.shape
    n2 = w2.shape[-1]
    return pl.pallas_call(
        _mm_nt_kernel,
        grid=(m // tm, n // tn),
        in_specs=[
            pl.BlockSpec((tm, k), lambda i, j: (i, 0)),
            pl.BlockSpec((None, tn, k), lambda i, j: (layer, j, 0)),
            pl.BlockSpec((None, k, n2), lambda i, j: (layer, 0, 0)),
        ],
        out_specs=[
            pl.BlockSpec((tm, tn), lambda i, j: (i, j)),
            pl.BlockSpec((tm, n2), lambda i, j: (i, 0)),
        ],
        out_shape=[
            jax.ShapeDtypeStruct((m, n), F32),
            jax.ShapeDtypeStruct((m, n2), F32),
        ],
        compiler_params=_params("parallel", "arbitrary"),
        name="matmul_nt",
    )(x, w_t, w2)


def _mm_res_kernel(x_ref, w_ref, *refs, mix_cols):
    *r_refs, g_ref, o_ref = refs
    w = _w16(w_ref)
    if mix_cols:
        acc = (_dot(x_ref[:, :MIX_A_COL], w[LRU_WIDTH:]) + _dot(x_ref[:, MIX_A_COL:], w[:LRU_WIDTH]))
    else:
        acc = _dot(x_ref[...], w)
    o_ref[...] = _read_rows(r_refs, o_ref.shape[0]) + g_ref[...] * acc


def matmul_gated_residual(x, w, layer, res, gate, *, tm, tn, mix_cols=False):
    m, k = x.shape
    n = w.shape[-1]
    r_args, r_specs = _row_parts(res, tm, tn)
    return pl.pallas_call(
        functools.partial(_mm_res_kernel, mix_cols=mix_cols),
        grid=(m // tm, n // tn),
        in_specs=[
            pl.BlockSpec((tm, k), lambda i, j: (i, 0)),
            pl.BlockSpec((None, k, tn), lambda i, j: (layer, 0, j)),
        ] + r_specs + [
            pl.BlockSpec((None, 1, tn), lambda i, j: (_cond_of_tile(i, tm), 0, j)),
        ],
        out_specs=pl.BlockSpec((tm, tn), lambda i, j: (i, j)),
        out_shape=jax.ShapeDtypeStruct((m, n), F32),
        compiler_params=_params("parallel", "arbitrary"),
        name="matmul_gated_residual",
    )(x, w, *r_args, gate)


def _swiglu_kernel(x_ref, wg_ref, wu_ref, wd_ref, *refs, mod_steps):
    if mod_steps:
        c_ref, wm_ref, bm_ref, o_ref, wd16_ref, mod_ref = refs
    else:
        o_ref, wd16_ref = refs
    x = x_ref[...]
    g = _dot(x, _w16(wg_ref))
    u = _dot(x, _w16(wu_ref))
    o_ref[...] = (_silu(g) * u).astype(o_ref.dtype)
    wd16_ref[...] = wd_ref[...].astype(BF16)
    if mod_steps:
        @pl.when(pl.program_id(0) * pl.num_programs(1) + pl.program_id(1) < mod_steps)
        def _():
            mod_ref[...] = _mod_tile(c_ref, wm_ref, bm_ref)


def swiglu_up(x, w_gate, w_up, w_down, layer, *, tm, tn, mod_args=None):
    m, k = x.shape
    n = w_gate.shape[-1]
    nj = n // tn
    steps = (m // tm) * nj
    kd, nd = w_down.shape[1:]
    slab = kd // steps
    assert slab * steps == kd and slab % 16 == 0
    in_specs = [
        pl.BlockSpec((tm, k), lambda i, j: (i, 0), pipeline_mode=pl.Buffered(1)),
        pl.BlockSpec((None, k, tn), lambda i, j: (layer, 0, j)),
        pl.BlockSpec((None, k, tn), lambda i, j: (layer, 0, j)),
        pl.BlockSpec((None, slab, nd), lambda i, j: (layer, i * nj + j, 0)),
    ]
    out_specs = [
        pl.BlockSpec((tm, tn), lambda i, j: (i, j)),
        pl.BlockSpec((None, slab, nd), lambda i, j: (0, i * nj + j, 0)),
    ]
    out_shape = [
        jax.ShapeDtypeStruct((m, n), BF16),
        jax.ShapeDtypeStruct((1, kd, nd), BF16),
    ]
    args = [x, w_gate, w_up, w_down]
    mod_steps = 0
    if mod_args is not None:
        mod_steps = N_MOD // MOD_SLAB
        assert mod_steps <= steps
        slab_of = lambda i, j: jnp.minimum(i * nj + j, mod_steps - 1)
        in_specs += [
            pl.BlockSpec((SUBLANE, D_MODEL), lambda i, j: (0, 0)),
            pl.BlockSpec((None, D_MODEL, MOD_SLAB), lambda i, j: (layer + 1, 0, slab_of(i, j))),
            pl.BlockSpec((None, 1, MOD_SLAB), lambda i, j: (layer + 1, 0, slab_of(i, j))),
        ]
        out_specs.append(pl.BlockSpec((SUBLANE, MOD_SLAB), lambda i, j: (0, slab_of(i, j))))
        out_shape.append(jax.ShapeDtypeStruct((SUBLANE, N_MOD), F32))
        args += list(mod_args)
    return pl.pallas_call(
        functools.partial(_swiglu_kernel, mod_steps=mod_steps),
        grid=(m // tm, nj),
        in_specs=in_specs,
        out_specs=out_specs,
        out_shape=out_shape,
        compiler_params=_params("arbitrary", "arbitrary"),
        name="swiglu_up",
    )(*args)


def _rope(x, c, se, so):
    nxt = pltpu.roll(x, LANE - 1, 1)
    prv = pltpu.roll(x, 1, 1)
    return x * c + nxt * se + prv * so


def _mm_rope_kernel(x_ref, w_ref, c_ref, se_ref, so_ref, o_ref, *, first_rope_tile):
    acc = _dot(x_ref[...], _w16(w_ref))
    j = pl.program_id(1)

    @pl.when(j < first_rope_tile)
    def _():
        o_ref[...] = acc.astype(o_ref.dtype)

    @pl.when(j >= first_rope_tile)
    def _():
        c, se, so = c_ref[...], se_ref[...], so_ref[...]
        for h in range(acc.shape[1] // LANE):
            sl = slice(h * LANE, (h + 1) * LANE)
            o_ref[:, sl] = _rope(acc[:, sl], c, se, so).astype(o_ref.dtype)


def matmul_rope_tail(x, w, layer, tables, *, tm, tn, first_rope_tile):
    m, k = x.shape
    n = w.shape[-1]
    nt = DEC_SEQ // tm
    tab = pl.BlockSpec((tm, LANE), lambda i, j: (i % nt, 0))
    return pl.pallas_call(
        functools.partial(_mm_rope_kernel, first_rope_tile=first_rope_tile),
        grid=(m // tm, n // tn),
        in_specs=[
            pl.BlockSpec((tm, k), lambda i, j: (i, 0)),
            pl.BlockSpec((None, k, tn), lambda i, j: (layer, 0, j)),
            tab, tab, tab,
        ],
        out_specs=pl.BlockSpec((tm, tn), lambda i, j: (i, j)),
        out_shape=jax.ShapeDtypeStruct((m, n), BF16),
        compiler_params=_params("parallel", "arbitrary"),
        name="matmul_rope_tail",
    )(x, w, *tables)


def _head_rmsnorm(x, g):
    return x * lax.rsqrt(jnp.mean(x * x, axis=-1, keepdims=True) + EPS) * g


def _prep_kernel(*refs, rope):
    if rope:
        (qk_ref, v_ref, c_ref, kr_ref, qn_ref, kn_ref, mqn_ref, mkvn_ref,
         cb_ref, seb_ref, sob_ref, cc_ref, sec_ref, soc_ref,
         _k_buf, _v_buf, _ckv_buf, _kr_buf,
         q_o, k_o, v_o, cq_o, ckv_o, kr_o) = refs
    else:
        (qk_ref, v_ref, c_ref, kr_ref, qn_ref, kn_ref, mqn_ref, mkvn_ref,
         q_o, k_o, v_o, cq_o, ckv_o, kr_o, kf_o, ckvf_o) = refs
    qn, kn = qn_ref[...], kn_ref[...]
    if rope:
        cb, seb, sob = cb_ref[...], seb_ref[...], sob_ref[...]
    for h in range(GQA_HEADS + GQA_KV_HEADS):
        sl = slice(h * LANE, (h + 1) * LANE)
        is_q = h < GQA_HEADS
        y = _head_rmsnorm(qk_ref[:, sl], qn if is_q else kn)
        osl = sl if is_q else slice((h - GQA_HEADS) * LANE, (h - GQA_HEADS + 1) * LANE)
        if not rope and not is_q:
            kf_o[:, osl] = y
        if rope:
            y = _rope(y, cb, seb, sob)
        (q_o if is_q else k_o)[:, osl] = y.astype(BF16)
    v_o[...] = v_ref[...].astype(BF16)
    cq = c_ref[:, :MLA_Q_RANK]
    cq_o[...] = _head_rmsnorm(cq, mqn_ref[...]).astype(BF16)
    ckv = _head_rmsnorm(c_ref[:, MLA_Q_RANK:], mkvn_ref[...])
    ckv_o[...] = ckv.astype(BF16)
    if not rope:
        ckvf_o[...] = ckv
    kr = kr_ref[...]
    if rope:
        kr = _rope(kr, cc_ref[...], sec_ref[...], soc_ref[...])
    kr_o[...] = kr.astype(BF16)


def prep(p, kr, gqa_qn, gqa_kn, mla_qn, mla_kvn, *, latent, tables_b=None, tables_c=None,
         layer=None, key_bufs=None):
    tm = ROW_TILE
    n_rows = N_LAT if latent else N_CTX
    off = (N_CTX // tm) if latent else 0
    nt = DEC_SEQ // tm
    row = lambda c: (lambda i: (i + off, c))
    in_specs = [
        pl.BlockSpec((tm, 2048), row(1)),
        pl.BlockSpec((tm, KV_WIDTH), row(8)),
        pl.BlockSpec((tm, 1536), row(3)),
        pl.BlockSpec((tm, LANE), row(0)),
        pl.BlockSpec((1, HEAD_DIM), lambda i: (0, 0)),
        pl.BlockSpec((1, HEAD_DIM), lambda i: (0, 0)),
        pl.BlockSpec((1, MLA_Q_RANK), lambda i: (0, 0)),
        pl.BlockSpec((1, MLA_KV_RANK), lambda i: (0, 0)),
    ]
    args = [p, p, p, kr, gqa_qn.reshape(1, -1), gqa_kn.reshape(1, -1),
            mla_qn.reshape(1, -1), mla_kvn.reshape(1, -1)]
    out = lambda w, dt: (pl.BlockSpec((tm, w), lambda i: (i, 0)), jax.ShapeDtypeStruct((n_rows, w), dt))
    outs = [out(GQA_WIDTH, BF16), out(KV_WIDTH, BF16), out(KV_WIDTH, BF16),
            out(MLA_Q_RANK, BF16), out(MLA_KV_RANK, BF16), out(LANE, BF16)]
    aliases = {}
    if latent:
        tab = pl.BlockSpec((tm, LANE), lambda i: (i % nt, 0))
        in_specs += [tab] * 6
        args += list(tables_b) + list(tables_c)
        blocks_per_req = LAT_KEYS // tm

        def key_rows(i):
            return ((layer * DEC_BATCH + i // nt) * blocks_per_req + PAST_LEN // tm + i % nt, 0)

        for n_out, buf in zip((1, 2, 4, 5), key_bufs):
            aliases[len(args)] = n_out
            in_specs.append(pl.BlockSpec(memory_space=pl.ANY))
            args.append(buf)
            outs[n_out] = (pl.BlockSpec((tm, buf.shape[1]), key_rows), jax.ShapeDtypeStruct(buf.shape, buf.dtype))
    else:
        outs += [out(KV_WIDTH, F32), out(MLA_KV_RANK, F32)]
    return pl.pallas_call(
        functools.partial(_prep_kernel, rope=latent),
        grid=(n_rows // tm,),
        in_specs=in_specs,
        out_specs=[o[0] for o in outs],
        out_shape=[o[1] for o in outs],
        input_output_aliases=aliases,
        compiler_params=_params("parallel"),
        name="prep_latent" if latent else "prep_context",
    )(*args)


def _gelu_tanh(x):
    return 0.5 * x * (1.0 + jnp.tanh(0.7978845608028654 * (x + 0.044715 * (x * x * x))))


def _softplus(z):
    return jnp.maximum(z, 0.0) + jnp.log1p(jnp.exp(-jnp.abs(z)))


def _lru_kernel(xa_ref, ga_ref, cw_ref, cb_ref, w_ref, p_ref, h0_ref, mix_ref, o_ref, st_ref,
                a_scr, b_scr, hf_scr, hb_scr, *, t_sub, tiles_per_iter):
    del mix_ref
    rows, c = xa_ref.shape
    n_sub = rows // t_sub
    x = xa_ref[...]
    tpos = lax.broadcasted_iota(jnp.int32, x.shape, 0) % t_sub
    cw = cw_ref[...]
    xm2 = jnp.where(tpos >= 2, pltpu.roll(x, 2, 0), 0.0)
    xm1 = jnp.where(tpos >= 1, pltpu.roll(x, 1, 0), 0.0)
    xp1 = jnp.where(tpos < t_sub - 1, pltpu.roll(x, rows - 1, 0), 0.0)
    xc = cb_ref[...] + xm2 * cw[0:1] + xm1 * cw[1:2] + x * cw[2:3] + xp1 * cw[3:4]
    xc16 = xc.astype(BF16)

    for d in range(2):
        g = _dot(xc16, w_ref[d])
        prm = p_ref[d]
        r = jax.nn.sigmoid(g[:, :c] + prm[0:1])
        ig = jax.nn.sigmoid(g[:, c:] + prm[1:2])
        log_a = (-LRU_C) * r * _softplus(-prm[2:3])
        a = jnp.exp(log_a)
        a_scr[d] = a
        b_scr[d] = jnp.sqrt(-jnp.tanh(log_a) * (a * a + 1.0)) * (ig * xc)

    ts = SUBLANE
    shifts = tuple(1 << n for n in range(ts.bit_length() - 1))
    sub_row = lax.broadcasted_iota(jnp.int32, (ts, c), 0)
    keep_f = [sub_row >= s for s in shifts]
    keep_b = [sub_row < ts - s for s in shifts]

    def tile_scan(a, b, carry, reverse):
        for n, s in enumerate(shifts):
            keep, shift = (keep_b[n], ts - s) if reverse else (keep_f[n], s)
            a_prev = jnp.where(keep, pltpu.roll(a, shift, 0), 1.0)
            b_prev = jnp.where(keep, pltpu.roll(b, shift, 0), 0.0)
            b = a * b_prev + b
            a = a * a_prev
        h = a * carry + b
        last = h[0:1] if reverse else h[ts - 1:ts]
        return h, jnp.broadcast_to(last, h.shape)

    def body(j, hs):
        hs = list(hs)
        for u in range(tiles_per_iter):
            off = pl.multiple_of((j * tiles_per_iter + u) * ts, ts)
            for q in range(n_sub):
                idx_f = pl.ds(q * t_sub + off, ts)
                idx_b = pl.ds(q * t_sub + (t_sub - ts) - off, ts)
                hf, hs[2 * q] = tile_scan(a_scr[0, idx_f, :], b_scr[0, idx_f, :], hs[2 * q], False)
                hb, hs[2 * q + 1] = tile_scan(a_scr[1, idx_b, :], b_scr[1, idx_b, :], hs[2 * q + 1], True)
                hf_scr[idx_f, :] = hf
                hb_scr[idx_b, :] = hb
        return tuple(hs)

    init = tuple(jnp.broadcast_to(h0_ref[q, d:d + 1, :], (ts, c)) for q in range(n_sub) for d in range(2))
    hs = lax.fori_loop(0, t_sub // (ts * tiles_per_iter), body, init)
    for q in range(n_sub):
        st_ref[q, 0:1, :] = hs[2 * q][0:1]
        st_ref[q, 1:2, :] = hs[2 * q + 1][0:1]
    o_ref[...] = ((hf_scr[...] + hb_scr[...]) * _gelu_tanh(ga_ref[...])).astype(o_ref.dtype)


def lru_mixer(p, conv_w, conv_b, w_gates, lru_prm, h0, mix, layer, *, latent):
    c = LRU_CHUNK
    rows = LRU_ROWS
    t_sub = DEC_SEQ if latent else SEQ
    n_sub = rows // t_sub
    n_rows = N_LAT if latent else N_CTX
    off = (N_CTX // rows) if latent else 0
    nck = LRU_WIDTH // c
    return pl.pallas_call(
        functools.partial(_lru_kernel, t_sub=t_sub, tiles_per_iter=4 if latent else 1),
        grid=(n_rows // rows, nck),
        in_specs=[
            pl.BlockSpec((rows, c), lambda s, k: (s + off, k)),
            pl.BlockSpec((rows, c), lambda s, k: (s + off, nck + k)),
            pl.BlockSpec((None, 4, c), lambda s, k: (layer, 0, k)),
            pl.BlockSpec((None, 1, c), lambda s, k: (layer, 0, k)),
            pl.BlockSpec((None, 2, None, c, 2 * c), lambda s, k: (layer, 0, k, 0, 0)),
            pl.BlockSpec((None, 2, 3, c), lambda s, k: (layer, 0, 0, k)),
            pl.BlockSpec((n_sub, 2, c), lambda s, k: (s, 0, k)),
            pl.BlockSpec(memory_space=pl.ANY),
        ],
        out_specs=[
            pl.BlockSpec((rows, c), lambda s, k: (s + off, MIX_A_COL // c + k)),
            pl.BlockSpec((n_sub, 2, c), lambda s, k: (s, 0, k)),
        ],
        out_shape=[
            jax.ShapeDtypeStruct(mix.shape, mix.dtype),
            jax.ShapeDtypeStruct((n_rows // t_sub, 2, LRU_WIDTH), F32),
        ],
        input_output_aliases={7: 0},
        scratch_shapes=[pltpu.VMEM((2, rows, c), F32)] * 2 + [pltpu.VMEM((rows, c), F32)] * 2,
        compiler_params=_params("parallel", "parallel"),
        name="lru_latent" if latent else "lru_context",
    )(p, p, conv_w, conv_b.reshape(DEPTH, 1, LRU_WIDTH), w_gates, lru_prm, h0, mix)


LOG2E = 1.4426950408889634


def _softmax_pv(s, v, scale):
    t = s * (scale * LOG2E)
    m = jnp.max(t, axis=-1, keepdims=True)
    p = jnp.exp2(t - m).astype(BF16)
    oa = _dot(p, jnp.concatenate([v, jnp.ones_like(v)], axis=1))
    return oa[:, :LANE] / oa[:, LANE:]


def _gqa_kernel(q_ref, k_ref, v_ref, mix_ref, o_ref, *, scale, groups, sub):
    del mix_ref
    tq = q_ref.shape[0]
    for g in range(groups):
        kv_sl = slice(g * LANE, (g + 1) * LANE)
        k, v = k_ref[:, kv_sl], v_ref[:, kv_sl]
        for h in range(g * GQA_REP, (g + 1) * GQA_REP):
            sl = slice(h * LANE, (h + 1) * LANE)
            for r0 in range(0, tq, sub):
                s = _dot_nt(q_ref[r0:r0 + sub, sl], k)
                o_ref[r0:r0 + sub, sl] = _softmax_pv(s, v, scale).astype(o_ref.dtype)


def gqa_attention(q, k, v, mix, row0, *, n_batch, t_q, t_k, tq, groups, sub, kv_b0=0):
    nq = t_q // tq
    qw = groups * GQA_REP * HEAD_DIM
    kw = groups * HEAD_DIM
    r_off, c_off = row0 // tq, MIX_B_COL // qw
    return pl.pallas_call(
        functools.partial(_gqa_kernel, scale=HEAD_DIM ** -0.5, groups=groups, sub=sub),
        grid=(n_batch, GQA_KV_HEADS // groups, nq),
        in_specs=[
            pl.BlockSpec((tq, qw), lambda b, g, i: (b * nq + i, g)),
            pl.BlockSpec((None, t_k, kw), lambda b, g, i: (kv_b0 + b, 0, g)),
            pl.BlockSpec((None, t_k, kw), lambda b, g, i: (kv_b0 + b, 0, g)),
            pl.BlockSpec(memory_space=pl.ANY),
        ],
        out_specs=pl.BlockSpec((tq, qw), lambda b, g, i: (r_off + b * nq + i, c_off + g)),
        out_shape=jax.ShapeDtypeStruct(mix.shape, mix.dtype),
        input_output_aliases={3: 0},
        compiler_params=_params("parallel", "parallel", "arbitrary"),
        name="gqa_attention",
    )(q, k, v, mix)


def _mla_kernel(qn_ref, qr_ref, kv_ref, kr_ref, mix_ref, o_ref, *, scale, heads, sub):
    del mix_ref
    tq = qn_ref.shape[0]
    kr = kr_ref[...]
    for h in range(heads):
        sl = slice(h * LANE, (h + 1) * LANE)
        k0 = h * (MLA_NOPE + MLA_V)
        k = jnp.concatenate([kv_ref[:, k0:k0 + MLA_NOPE], kr], axis=1)
        v = kv_ref[:, k0 + MLA_NOPE:k0 + MLA_NOPE + MLA_V]
        for r0 in range(0, tq, sub):
            q = jnp.concatenate([qn_ref[r0:r0 + sub, sl], qr_ref[r0:r0 + sub, sl]], axis=1)
            o_ref[r0:r0 + sub, sl] = _softmax_pv(_dot_nt(q, k), v, scale).astype(o_ref.dtype)


def mla_attention(q, kv, kr, mix, row0, *, n_batch, t_q, t_k, tq, heads, sub, kr_b0=0):
    nq = t_q // tq
    nh = MLA_HEADS // heads
    ow = heads * MLA_V
    r_off, c_off = row0 // tq, MIX_C_COL // ow
    return pl.pallas_call(
        functools.partial(_mla_kernel, scale=(MLA_NOPE + MLA_ROPE) ** -0.5, heads=heads, sub=sub),
        grid=(n_batch, nh, nq),
        in_specs=[
            pl.BlockSpec((tq, heads * MLA_NOPE), lambda b, h, i: (b * nq + i, h)),
            pl.BlockSpec((tq, heads * LANE), lambda b, h, i: (b * nq + i, nh + h)),
            pl.BlockSpec((None, t_k, heads * (MLA_NOPE + MLA_V)), lambda b, h, i: (b, 0, h)),
            pl.BlockSpec((None, t_k, LANE), lambda b, h, i: (kr_b0 + b, 0, 0)),
            pl.BlockSpec(memory_space=pl.ANY),
        ],
        out_specs=pl.BlockSpec((tq, ow), lambda b, h, i: (r_off + b * nq + i, c_off + h)),
        out_shape=jax.ShapeDtypeStruct(mix.shape, mix.dtype),
        input_output_aliases={4: 0},
        compiler_params=_params("parallel", "parallel", "arbitrary"),
        name="mla_attention",
    )(q, q, kv, kr, mix)


def _rope_tables(rot_dim):
    rows = DEC_SEQ // GRID_W
    row = jnp.repeat(jnp.arange(rows, dtype=F32), GRID_W)
    col = jnp.tile(jnp.arange(GRID_W, dtype=F32), rows)
    quarter = rot_dim // 4
    freqs = ROPE_THETA ** (-jnp.arange(quarter, dtype=F32) / quarter)
    ang = jnp.concatenate([row[:, None] * freqs, col[:, None] * freqs], axis=-1)
    cos, sin = jnp.cos(ang), jnp.sin(ang)
    zero = jnp.zeros_like(sin)
    c = jnp.repeat(cos, 2, axis=-1)
    se = jnp.stack([-sin, zero], axis=-1).reshape(DEC_SEQ, rot_dim)
    so = jnp.stack([zero, sin], axis=-1).reshape(DEC_SEQ, rot_dim)
    pad = ((0, 0), (0, LANE - rot_dim))
    return tuple(jnp.pad(t, pad) for t in (c, se, so))


def _block_diag_gates(lru_wr, lru_wi):
    per = LRU_CHUNK // LRU_BLOCK
    nck = LRU_WIDTH // LRU_CHUNK
    eye = jnp.eye(per, dtype=F32)

    def bd(w):
        w = w.reshape(DEPTH, 2, nck, per, LRU_BLOCK, LRU_BLOCK)
        w = jnp.einsum("dzcakj,ab->dzcakbj", w, eye)
        return w.reshape(DEPTH, 2, nck, LRU_CHUNK, LRU_CHUNK)

    return jnp.concatenate([bd(lru_wr), bd(lru_wi)], axis=-1).astype(BF16)


def _key_buffer(cache, width):
    c = jnp.swapaxes(cache, 0, 1).astype(BF16)
    c = jnp.pad(c, ((0, 0), (0, 0), (0, DEC_SEQ), (0, width - cache.shape[-1])))
    return c.reshape(DEPTH * DEC_BATCH * LAT_KEYS, width)


def _permute_w_uq(w_uq):
    w = w_uq.reshape(DEPTH, MLA_Q_RANK, MLA_HEADS, MLA_NOPE + MLA_ROPE)
    nope = w[..., :MLA_NOPE].reshape(DEPTH, MLA_Q_RANK, MLA_HEADS * MLA_NOPE)
    rope = jnp.pad(w[..., MLA_NOPE:], ((0, 0), (0, 0), (0, 0), (0, LANE - MLA_ROPE)))
    rope = rope.reshape(DEPTH, MLA_Q_RANK, MLA_HEADS * LANE)
    return jnp.concatenate([nope, rope], axis=-1).astype(BF16)


def kernel(x_prompt, x_sample, state_lru, cache_gqa_k, cache_gqa_v, cache_mla_ckv, cache_mla_krope, c,
           c_ctx, w_mod, b_mod, norm_mix_g, norm_ffn_g, w_in, conv_w, conv_b, lru_wr, lru_br, lru_wi, lru_bi,
           lru_lam, gqa_qn, gqa_kn, mla_qn, mla_kvn, w_uq, w_ukv, w_out, w_gate, w_up, w_down, norm_f):
    x = (x_prompt.reshape(N_CTX, D_MODEL), x_sample.reshape(N_LAT, D_MODEL))

    cond8 = jnp.zeros((SUBLANE, D_MODEL), F32).at[0].set(c_ctx).at[1:1 + DEC_BATCH].set(c)
    b_mod3 = b_mod.reshape(DEPTH, 1, N_MOD)
    mod = modulation(cond8, w_mod, b_mod3, 0)

    w_kr16 = jnp.pad(w_in[:, :, IN_MAIN:], ((0, 0), (0, 0), (0, LANE - MLA_ROPE))).astype(BF16)
    w_in_t = jnp.swapaxes(w_in, 1, 2)
    w_uq16 = _permute_w_uq(w_uq)
    w_gates = _block_diag_gates(lru_wr, lru_wi)
    lru_prm = jnp.stack([lru_br, lru_bi, lru_lam], axis=2)
    tables_b = _rope_tables(HEAD_DIM)
    tables_c = _rope_tables(MLA_ROPE)
    h0_ctx = jnp.zeros((BATCH, 2, LRU_WIDTH), F32)
    key_bufs = (_key_buffer(cache_gqa_k.reshape(DEC_BATCH, DEPTH, PAST_LEN, KV_WIDTH), KV_WIDTH),
                _key_buffer(cache_gqa_v.reshape(DEC_BATCH, DEPTH, PAST_LEN, KV_WIDTH), KV_WIDTH),
                _key_buffer(cache_mla_ckv, MLA_KV_RANK),
                _key_buffer(cache_mla_krope, LANE))

    mix = jnp.zeros((N_TOK, D_MIX), BF16)

    st_lru, st_k, st_v, st_ckv, st_kr = [], [], [], [], []
    for l in range(DEPTH):
        mod6 = mod.reshape(SUBLANE, 6, 1, D_MODEL)
        sh1, sc1, g1, sh2, sc2, g2 = (mod6[:, s] for s in range(6))

        h = norm_mod(x, norm_mix_g[l], sc1, sh1)
        p, kr = matmul_nt(h, w_in_t, w_kr16, l, n=IN_MAIN, **T_IN)

        mix, s_lru = lru_mixer(p, conv_w, conv_b, w_gates, lru_prm, h0_ctx, mix, l, latent=False)
        mix, _ = lru_mixer(p, conv_w, conv_b, w_gates, lru_prm, state_lru[:, l], mix, l, latent=True)

        (q_c, k_c, v_c, cq_c, ckv_c, kr_c, kf_c, ckvf_c) = prep(
            p, kr, gqa_qn[l], gqa_kn[l], mla_qn[l], mla_kvn[l], latent=False)
        q_l, k_buf, v_buf, cq_l, ckv_buf, kr_buf = prep(
            p, kr, gqa_qn[l], gqa_kn[l], mla_qn[l], mla_kvn[l], latent=True,
            tables_b=tables_b, tables_c=tables_c, layer=l, key_bufs=key_bufs)
        key_bufs = (k_buf, v_buf, ckv_buf, kr_buf)
        req0 = l * DEC_BATCH

        mix = gqa_attention(q_c, k_c.reshape(BATCH, SEQ, KV_WIDTH), v_c.reshape(BATCH, SEQ, KV_WIDTH), mix, 0,
                            n_batch=BATCH, t_q=SEQ, t_k=SEQ, tq=SEQ, groups=GQA_KV_HEADS, sub=SEQ)
        mix = gqa_attention(q_l, k_buf.reshape(-1, LAT_KEYS, KV_WIDTH), v_buf.reshape(-1, LAT_KEYS, KV_WIDTH),
                            mix, N_CTX, n_batch=DEC_BATCH, t_q=DEC_SEQ, t_k=LAT_KEYS, groups=1, **A_GQA,
                            kv_b0=req0)

        qm_c = matmul(cq_c, w_uq16, l, out_dtype=BF16, **T_MLA)
        qm_l = matmul_rope_tail(cq_l, w_uq16, l, tables_c, tm=1024, tn=MLA_WIDTH, first_rope_tile=1)
        kv_c = matmul(ckv_c, w_ukv, l, out_dtype=BF16, **T_MLA)
        kv_l = matmul(ckv_buf, w_ukv, l, tm=LAT_KEYS, tn=T_MLA["tn"], out_dtype=BF16,
                      row0=req0 * LAT_KEYS, m=DEC_BATCH * LAT_KEYS)
        mix = mla_attention(qm_c, kv_c.reshape(BATCH, SEQ, -1), kr_c.reshape(BATCH, SEQ, LANE), mix, 0,
                            n_batch=BATCH, t_q=SEQ, t_k=SEQ, tq=SEQ, heads=MLA_HEADS, sub=SEQ)
        mix = mla_attention(qm_l, kv_l.reshape(DEC_BATCH, LAT_KEYS, -1), kr_buf.reshape(-1, LAT_KEYS, LANE),
                            mix, N_CTX, n_batch=DEC_BATCH, t_q=DEC_SEQ, t_k=LAT_KEYS, heads=1, **A_MLA,
                            kr_b0=req0)

        x = matmul_gated_residual(mix, w_out, l, x, g1, mix_cols=True, **T_OUT)

        h = norm_mod(x, norm_ffn_g[l], sc2, sh2)
        if l + 1 < DEPTH:
            ff, w_down16, mod = swiglu_up(h, w_gate, w_up, w_down, l, mod_args=(cond8, w_mod, b_mod3), **T_UP)
        else:
            ff, w_down16 = swiglu_up(h, w_gate, w_up, w_down, l, **T_UP)
        x = matmul_gated_residual(ff, w_down16, 0, x, g2, **T_DOWN)

        st_lru.append(s_lru)
        st_k.append(kf_c.reshape(BATCH, SEQ, GQA_KV_HEADS, HEAD_DIM))
        st_v.append(p[:N_CTX, 4096:4096 + KV_WIDTH].reshape(BATCH, SEQ, GQA_KV_HEADS, HEAD_DIM))
        st_ckv.append(ckvf_c.reshape(BATCH, SEQ, MLA_KV_RANK))
        st_kr.append(kr[:N_CTX, :MLA_ROPE].reshape(BATCH, SEQ, MLA_ROPE))

    y_ctx = final_norm(x, norm_f, 0, N_CTX)
    y_lat = final_norm(x, norm_f, N_CTX, N_LAT)
    return (y_ctx.reshape(BATCH, SEQ, D_MODEL), y_lat.reshape(DEC_BATCH, DEC_SEQ, D_MODEL),
            jnp.stack(st_lru, axis=1), jnp.stack(st_k, axis=1), jnp.stack(st_v, axis=1),
            jnp.stack(st_ckv, axis=1), jnp.stack(st_kr, axis=1))
```

```python
import functools

import jax
import jax.numpy as jnp
from jax import lax
from jax.experimental import pallas as pl
from jax.experimental.pallas import tpu as pltpu

F32 = jnp.float32
BF16 = jnp.bfloat16

D_MODEL = 4096
BATCH = 16
SEQ = 256
DEPTH = 4
DEC_BATCH = 2
DEC_SEQ = 2048
PAST_LEN = 512
GRID_W = 64
EPS = 1e-6
ROPE_THETA = 10000.0

LRU_WIDTH = 1024
LRU_BLOCKS = 16
LRU_BLOCK = 64
LRU_C = 8.0
HEAD_DIM = 128
GQA_HEADS = 12
GQA_KV_HEADS = 4
GQA_REP = GQA_HEADS // GQA_KV_HEADS
GQA_WIDTH = GQA_HEADS * HEAD_DIM
KV_WIDTH = GQA_KV_HEADS * HEAD_DIM
MLA_HEADS = 12
MLA_Q_RANK = 1024
MLA_KV_RANK = 512
MLA_NOPE = 128
MLA_ROPE = 64
MLA_V = 128
MLA_WIDTH = MLA_HEADS * MLA_V
D_MIX = LRU_WIDTH + GQA_WIDTH + MLA_WIDTH
MIX_B_COL = 0
MIX_C_COL = GQA_WIDTH
MIX_A_COL = GQA_WIDTH + MLA_WIDTH
D_FF = 11008
IN_MAIN = 6144
LANE = 128
N_CTX = BATCH * SEQ
N_LAT = DEC_BATCH * DEC_SEQ
N_TOK = N_CTX + N_LAT
LAT_KEYS = PAST_LEN + DEC_SEQ
VMEM_LIMIT = 56 * 1024 * 1024
LRU_CHUNK = 256
LRU_ROWS = 2048
ROW_TILE = 256
NORM_TILE = 512
NORM_CHUNK = 16
SUBLANE = 8

T_IN = dict(tm=1024, tn=512)
T_OUT = dict(tm=1024, tn=512)
T_IN16 = dict(tm=1024, tn=1024)
T_OUT16 = dict(tm=1024, tn=512)
T_UP = dict(tm=2048, tn=256)
T_DOWN = dict(tm=512, tn=512)
T_MLA = dict(tm=2048, tn=1024)
A_GQA = dict(tq=1024, sub=128)
A_MLA = dict(tq=2048, sub=256)


def _params(*sem):
    return pltpu.CompilerParams(dimension_semantics=sem, vmem_limit_bytes=VMEM_LIMIT)


def _cond_of_tile(i, tm):
    row = i * tm
    return jnp.where(row < N_CTX, 0, 1 + (row - N_CTX) // DEC_SEQ)


def _silu(x):
    return x * jax.nn.sigmoid(x)


def _dot(a, b):
    return jnp.dot(a, b, preferred_element_type=F32)


def _dot_nt(a, b):
    return lax.dot_general(a, b, (((1,), (1,)), ((), ())), preferred_element_type=F32)


N_MOD = 6 * D_MODEL
MOD_SLAB = 256
NEXT_CAST_STEPS = 128


def _mod_tile(c_ref, w_ref, b_ref):
    return _dot(_silu(c_ref[...]).astype(BF16), _w16(w_ref)) + b_ref[...]


def _mod_kernel(c_ref, w_ref, b_ref, o_ref):
    o_ref[...] = _mod_tile(c_ref, w_ref, b_ref)


def modulation(cond8, w_mod, b_mod3, layer):
    tn = 512
    return pl.pallas_call(
        _mod_kernel,
        grid=(N_MOD // tn,),
        in_specs=[
            pl.BlockSpec((SUBLANE, D_MODEL), lambda j: (0, 0)),
            pl.BlockSpec((None, D_MODEL, tn), lambda j: (layer, 0, j)),
            pl.BlockSpec((None, 1, tn), lambda j: (layer, 0, j)),
        ],
        out_specs=pl.BlockSpec((SUBLANE, tn), lambda j: (0, j)),
        out_shape=jax.ShapeDtypeStruct((SUBLANE, N_MOD), F32),
        compiler_params=_params("parallel"),
        name="modulation",
    )(cond8, w_mod, b_mod3)


def _row_parts(x, tm, tn=None):
    def spec(row_fn, col_fn):
        if tn is None:
            return pl.BlockSpec((tm, D_MODEL), lambda i: (row_fn(i), 0))
        return pl.BlockSpec((tm, tn), lambda i, j: (row_fn(i), col_fn(i, j)))

    if not isinstance(x, tuple):
        return [x], [spec(lambda i: i, lambda i, j: j)]
    n_a = x[0].shape[0] // tm
    return list(x), [
        spec(lambda i: jnp.minimum(i, n_a - 1), lambda i, j: jnp.where(i < n_a, j, 0)),
        spec(lambda i: jnp.maximum(i - n_a, 0), lambda i, j: jnp.where(i < n_a, 0, j)),
    ]


def _read_rows(refs, tm, rows=slice(None)):
    if len(refs) == 1:
        return refs[0][rows, :]
    n_a = N_CTX // tm
    return jnp.where(pl.program_id(0) < n_a, refs[0][rows, :], refs[1][rows, :])


def _rmsnorm_rows(x_refs, o_ref, scale, shift):
    tm = o_ref.shape[0]

    def body(c, carry):
        rows = pl.ds(pl.multiple_of(c * NORM_CHUNK, NORM_CHUNK), NORM_CHUNK)
        x = _read_rows(x_refs, tm, rows)
        y = x * lax.rsqrt(jnp.mean(x * x, axis=-1, keepdims=True) + EPS) * scale
        if shift is not None:
            y = y + shift
        o_ref[rows, :] = y.astype(o_ref.dtype)
        return carry

    lax.fori_loop(0, tm // NORM_CHUNK, body, 0, unroll=4)


def _norm_mod_kernel(*refs):
    *x_refs, g_ref, sc_ref, sh_ref, o_ref = refs
    _rmsnorm_rows(x_refs, o_ref, g_ref[...] * (1.0 + sc_ref[...]), sh_ref[...])


def norm_mod(x, g, sc, sh):
    tm = NORM_TILE
    x_args, x_specs = _row_parts(x, tm)
    return pl.pallas_call(
        _norm_mod_kernel,
        grid=(N_TOK // tm,),
        in_specs=x_specs + [
            pl.BlockSpec((1, D_MODEL), lambda i: (0, 0)),
            pl.BlockSpec((None, 1, D_MODEL), lambda i: (_cond_of_tile(i, tm), 0, 0)),
            pl.BlockSpec((None, 1, D_MODEL), lambda i: (_cond_of_tile(i, tm), 0, 0)),
        ],
        out_specs=pl.BlockSpec((tm, D_MODEL), lambda i: (i, 0)),
        out_shape=jax.ShapeDtypeStruct((N_TOK, D_MODEL), BF16),
        compiler_params=_params("parallel"),
        name="norm_mod",
    )(*x_args, g.reshape(1, D_MODEL), sc, sh)


def _final_norm_kernel(x_ref, g_ref, o_ref):
    _rmsnorm_rows([x_ref], o_ref, g_ref[...], None)


def final_norm(x, g, row0, n_rows):
    tm = NORM_TILE
    off = row0 // tm
    return pl.pallas_call(
        _final_norm_kernel,
        grid=(n_rows // tm,),
        in_specs=[
            pl.BlockSpec((tm, D_MODEL), lambda i: (i + off, 0)),
            pl.BlockSpec((1, D_MODEL), lambda i: (0, 0)),
        ],
        out_specs=pl.BlockSpec((tm, D_MODEL), lambda i: (i, 0)),
        out_shape=jax.ShapeDtypeStruct((n_rows, D_MODEL), F32),
        compiler_params=_params("parallel"),
        name="final_norm",
    )(x, g.reshape(1, D_MODEL))


def _w16(w_ref):
    return w_ref[...].astype(BF16)


def _mm_kernel(x_ref, w_ref, o_ref):
    o_ref[...] = _dot(x_ref[...], _w16(w_ref)).astype(o_ref.dtype)


def matmul(x, w, layer, *, tm, tn, out_dtype, row0=0, m=None):
    k = x.shape[1]
    m = x.shape[0] if m is None else m
    n = w.shape[-1]
    r_off = row0 // tm
    return pl.pallas_call(
        _mm_kernel,
        grid=(m // tm, n // tn),
        in_specs=[
            pl.BlockSpec((tm, k), lambda i, j: (r_off + i, 0)),
            pl.BlockSpec((None, k, tn), lambda i, j: (layer, 0, j)),
        ],
        out_specs=pl.BlockSpec((tm, tn), lambda i, j: (i, j)),
        out_shape=jax.ShapeDtypeStruct((m, n), out_dtype),
        compiler_params=_params("parallel", "arbitrary"),
        name="matmul",
    )(x, w)


def _mm_nt_kernel(x_ref, w_ref, w2_ref, o_ref, o2_ref):
    x = x_ref[...]
    o_ref[...] = _dot_nt(x, _w16(w_ref)).astype(o_ref.dtype)

    @pl.when(pl.program_id(1) == 0)
    def _():
        o2_ref[...] = _dot(x, w2_ref[...])


def matmul_nt(x, w_t, w2, layer, *, tm, tn, n, w2_layer=None):
    m, k = x.shape
    n2 = w2.shape[-1]
    w2_layer = layer if w2_layer is None else w2_layer
    return pl.pallas_call(
        _mm_nt_kernel,
        grid=(m // tm, n // tn),
        in_specs=[
            pl.BlockSpec((tm, k), lambda i, j: (i, 0)),
            pl.BlockSpec((None, tn, k), lambda i, j: (layer, j, 0)),
            pl.BlockSpec((None, k, n2), lambda i, j: (w2_layer, 0, 0)),
        ],
        out_specs=[
            pl.BlockSpec((tm, tn), lambda i, j: (i, j)),
            pl.BlockSpec((tm, n2), lambda i, j: (i, 0)),
        ],
        out_shape=[
            jax.ShapeDtypeStruct((m, n), F32),
            jax.ShapeDtypeStruct((m, n2), F32),
        ],
        compiler_params=_params("parallel", "arbitrary"),
        name="matmul_nt",
    )(x, w_t, w2)


def _mm_res_kernel(x_ref, w_ref, *refs, mix_cols):
    *r_refs, g_ref, o_ref = refs
    w = _w16(w_ref)
    if mix_cols:
        acc = (_dot(x_ref[:, :MIX_A_COL], w[LRU_WIDTH:]) + _dot(x_ref[:, MIX_A_COL:], w[:LRU_WIDTH]))
    else:
        acc = _dot(x_ref[...], w)
    o_ref[...] = _read_rows(r_refs, o_ref.shape[0]) + g_ref[...] * acc


def matmul_gated_residual(x, w, layer, res, gate, *, tm, tn, mix_cols=False):
    m, k = x.shape
    n = w.shape[-1]
    r_args, r_specs = _row_parts(res, tm, tn)
    return pl.pallas_call(
        functools.partial(_mm_res_kernel, mix_cols=mix_cols),
        grid=(m // tm, n // tn),
        in_specs=[
            pl.BlockSpec((tm, k), lambda i, j: (i, 0)),
            pl.BlockSpec((None, k, tn), lambda i, j: (layer, 0, j)),
        ] + r_specs + [
            pl.BlockSpec((None, 1, tn), lambda i, j: (_cond_of_tile(i, tm), 0, j)),
        ],
        out_specs=pl.BlockSpec((tm, tn), lambda i, j: (i, j)),
        out_shape=jax.ShapeDtypeStruct((m, n), F32),
        compiler_params=_params("parallel", "arbitrary"),
        name="matmul_gated_residual",
    )(x, w, *r_args, gate)


def _swiglu_kernel(x_ref, wg_ref, wu_ref, wd_ref, *refs, next_layer):
    if next_layer:
        c_ref, wm_ref, bm_ref, win_ref, wout_ref, o_ref, wd16_ref, mod_ref, win16_ref, wout16_ref = refs
    else:
        o_ref, wd16_ref = refs
    x = x_ref[...]
    g = _dot(x, _w16(wg_ref))
    u = _dot(x, _w16(wu_ref))
    o_ref[...] = (_silu(g) * u).astype(o_ref.dtype)
    wd16_ref[...] = wd_ref[...].astype(BF16)
    if next_layer:
        step = pl.program_id(0) * pl.num_programs(1) + pl.program_id(1)

        @pl.when(step < N_MOD // MOD_SLAB)
        def _():
            mod_ref[...] = _mod_tile(c_ref, wm_ref, bm_ref)

        @pl.when(step < NEXT_CAST_STEPS)
        def _():
            win16_ref[...] = win_ref[...].astype(BF16)
            wout16_ref[...] = wout_ref[...].astype(BF16)


def swiglu_up(x, w_gate, w_up, w_down, layer, *, tm, tn, next_args=None):
    m, k = x.shape
    n = w_gate.shape[-1]
    nj = n // tn
    steps = (m // tm) * nj
    kd, nd = w_down.shape[1:]
    slab = kd // steps
    assert slab * steps == kd and slab % 16 == 0
    in_specs = [
        pl.BlockSpec((tm, k), lambda i, j: (i, 0), pipeline_mode=pl.Buffered(1)),
        pl.BlockSpec((None, k, tn), lambda i, j: (layer, 0, j)),
        pl.BlockSpec((None, k, tn), lambda i, j: (layer, 0, j)),
        pl.BlockSpec((None, slab, nd), lambda i, j: (layer, i * nj + j, 0)),
    ]
    out_specs = [
        pl.BlockSpec((tm, tn), lambda i, j: (i, j)),
        pl.BlockSpec((None, slab, nd), lambda i, j: (0, i * nj + j, 0)),
    ]
    out_shape = [
        jax.ShapeDtypeStruct((m, n), BF16),
        jax.ShapeDtypeStruct((1, kd, nd), BF16),
    ]
    args = [x, w_gate, w_up, w_down]
    if next_args is not None:
        mod_steps = N_MOD // MOD_SLAB
        assert max(mod_steps, NEXT_CAST_STEPS) <= steps
        in_rows, out_rows = IN_MAIN // NEXT_CAST_STEPS, D_MIX // NEXT_CAST_STEPS
        assert in_rows % 16 == 0 and out_rows % 16 == 0
        mod_slab = lambda i, j: jnp.minimum(i * nj + j, mod_steps - 1)
        cast_slab = lambda i, j: jnp.minimum(i * nj + j, NEXT_CAST_STEPS - 1)
        in_specs += [
            pl.BlockSpec((SUBLANE, D_MODEL), lambda i, j: (0, 0)),
            pl.BlockSpec((None, D_MODEL, MOD_SLAB), lambda i, j: (layer + 1, 0, mod_slab(i, j))),
            pl.BlockSpec((None, 1, MOD_SLAB), lambda i, j: (layer + 1, 0, mod_slab(i, j))),
            pl.BlockSpec((None, in_rows, D_MODEL), lambda i, j: (layer + 1, cast_slab(i, j), 0)),
            pl.BlockSpec((None, out_rows, D_MODEL), lambda i, j: (layer + 1, cast_slab(i, j), 0)),
        ]
        out_specs += [
            pl.BlockSpec((SUBLANE, MOD_SLAB), lambda i, j: (0, mod_slab(i, j))),
            pl.BlockSpec((None, in_rows, D_MODEL), lambda i, j: (0, cast_slab(i, j), 0)),
            pl.BlockSpec((None, out_rows, D_MODEL), lambda i, j: (0, cast_slab(i, j), 0)),
        ]
        out_shape += [
            jax.ShapeDtypeStruct((SUBLANE, N_MOD), F32),
            jax.ShapeDtypeStruct((1, IN_MAIN, D_MODEL), BF16),
            jax.ShapeDtypeStruct((1, D_MIX, D_MODEL), BF16),
        ]
        args += list(next_args)
    return pl.pallas_call(
        functools.partial(_swiglu_kernel, next_layer=next_args is not None),
        grid=(m // tm, nj),
        in_specs=in_specs,
        out_specs=out_specs,
        out_shape=out_shape,
        compiler_params=_params("arbitrary", "arbitrary"),
        name="swiglu_up",
    )(*args)


def _rope(x, c, se, so):
    nxt = pltpu.roll(x, LANE - 1, 1)
    prv = pltpu.roll(x, 1, 1)
    return x * c + nxt * se + prv * so


def _mm_rope_kernel(x_ref, w_ref, c_ref, se_ref, so_ref, o_ref, *, first_rope_tile):
    acc = _dot(x_ref[...], _w16(w_ref))
    j = pl.program_id(1)

    @pl.when(j < first_rope_tile)
    def _():
        o_ref[...] = acc.astype(o_ref.dtype)

    @pl.when(j >= first_rope_tile)
    def _():
        c, se, so = c_ref[...], se_ref[...], so_ref[...]
        for h in range(acc.shape[1] // LANE):
            sl = slice(h * LANE, (h + 1) * LANE)
            o_ref[:, sl] = _rope(acc[:, sl], c, se, so).astype(o_ref.dtype)


def matmul_rope_tail(x, w, layer, tables, *, tm, tn, first_rope_tile):
    m, k = x.shape
    n = w.shape[-1]
    nt = DEC_SEQ // tm
    tab = pl.BlockSpec((tm, LANE), lambda i, j: (i % nt, 0))
    return pl.pallas_call(
        functools.partial(_mm_rope_kernel, first_rope_tile=first_rope_tile),
        grid=(m // tm, n // tn),
        in_specs=[
            pl.BlockSpec((tm, k), lambda i, j: (i, 0)),
            pl.BlockSpec((None, k, tn), lambda i, j: (layer, 0, j)),
            tab, tab, tab,
        ],
        out_specs=pl.BlockSpec((tm, tn), lambda i, j: (i, j)),
        out_shape=jax.ShapeDtypeStruct((m, n), BF16),
        compiler_params=_params("parallel", "arbitrary"),
        name="matmul_rope_tail",
    )(x, w, *tables)


def _head_rmsnorm(x, g):
    return x * lax.rsqrt(jnp.mean(x * x, axis=-1, keepdims=True) + EPS) * g


def _prep_kernel(*refs, rope):
    if rope:
        (qk_ref, v_ref, c_ref, kr_ref, qn_ref, kn_ref, mqn_ref, mkvn_ref,
         cb_ref, seb_ref, sob_ref, cc_ref, sec_ref, soc_ref,
         _k_buf, _v_buf, _ckv_buf, _kr_buf,
         q_o, k_o, v_o, cq_o, ckv_o, kr_o) = refs
    else:
        (qk_ref, v_ref, c_ref, kr_ref, qn_ref, kn_ref, mqn_ref, mkvn_ref,
         q_o, k_o, v_o, cq_o, ckv_o, kr_o, kf_o, ckvf_o) = refs
    qn, kn = qn_ref[...], kn_ref[...]
    if rope:
        cb, seb, sob = cb_ref[...], seb_ref[...], sob_ref[...]
    for h in range(GQA_HEADS + GQA_KV_HEADS):
        sl = slice(h * LANE, (h + 1) * LANE)
        is_q = h < GQA_HEADS
        y = _head_rmsnorm(qk_ref[:, sl], qn if is_q else kn)
        osl = sl if is_q else slice((h - GQA_HEADS) * LANE, (h - GQA_HEADS + 1) * LANE)
        if not rope and not is_q:
            kf_o[:, osl] = y
        if rope:
            y = _rope(y, cb, seb, sob)
        (q_o if is_q else k_o)[:, osl] = y.astype(BF16)
    v_o[...] = v_ref[...].astype(BF16)
    cq = c_ref[:, :MLA_Q_RANK]
    cq_o[...] = _head_rmsnorm(cq, mqn_ref[...]).astype(BF16)
    ckv = _head_rmsnorm(c_ref[:, MLA_Q_RANK:], mkvn_ref[...])
    ckv_o[...] = ckv.astype(BF16)
    if not rope:
        ckvf_o[...] = ckv
    kr = kr_ref[...]
    if rope:
        kr = _rope(kr, cc_ref[...], sec_ref[...], soc_ref[...])
    kr_o[...] = kr.astype(BF16)


def prep(p, kr, gqa_qn, gqa_kn, mla_qn, mla_kvn, *, latent, tables_b=None, tables_c=None,
         layer=None, key_bufs=None):
    tm = ROW_TILE
    n_rows = N_LAT if latent else N_CTX
    off = (N_CTX // tm) if latent else 0
    nt = DEC_SEQ // tm
    row = lambda c: (lambda i: (i + off, c))
    in_specs = [
        pl.BlockSpec((tm, 2048), row(1)),
        pl.BlockSpec((tm, KV_WIDTH), row(8)),
        pl.BlockSpec((tm, 1536), row(3)),
        pl.BlockSpec((tm, LANE), row(0)),
        pl.BlockSpec((1, HEAD_DIM), lambda i: (0, 0)),
        pl.BlockSpec((1, HEAD_DIM), lambda i: (0, 0)),
        pl.BlockSpec((1, MLA_Q_RANK), lambda i: (0, 0)),
        pl.BlockSpec((1, MLA_KV_RANK), lambda i: (0, 0)),
    ]
    args = [p, p, p, kr, gqa_qn.reshape(1, -1), gqa_kn.reshape(1, -1),
            mla_qn.reshape(1, -1), mla_kvn.reshape(1, -1)]
    out = lambda w, dt: (pl.BlockSpec((tm, w), lambda i: (i, 0)), jax.ShapeDtypeStruct((n_rows, w), dt))
    outs = [out(GQA_WIDTH, BF16), out(KV_WIDTH, BF16), out(KV_WIDTH, BF16),
            out(MLA_Q_RANK, BF16), out(MLA_KV_RANK, BF16), out(LANE, BF16)]
    aliases = {}
    if latent:
        tab = pl.BlockSpec((tm, LANE), lambda i: (i % nt, 0))
        in_specs += [tab] * 6
        args += list(tables_b) + list(tables_c)
        blocks_per_req = LAT_KEYS // tm

        def key_rows(i):
            return ((layer * DEC_BATCH + i // nt) * blocks_per_req + PAST_LEN // tm + i % nt, 0)

        for n_out, buf in zip((1, 2, 4, 5), key_bufs):
            aliases[len(args)] = n_out
            in_specs.append(pl.BlockSpec(memory_space=pl.ANY))
            args.append(buf)
            outs[n_out] = (pl.BlockSpec((tm, buf.shape[1]), key_rows), jax.ShapeDtypeStruct(buf.shape, buf.dtype))
    else:
        outs += [out(KV_WIDTH, F32), out(MLA_KV_RANK, F32)]
    return pl.pallas_call(
        functools.partial(_prep_kernel, rope=latent),
        grid=(n_rows // tm,),
        in_specs=in_specs,
        out_specs=[o[0] for o in outs],
        out_shape=[o[1] for o in outs],
        input_output_aliases=aliases,
        compiler_params=_params("parallel"),
        name="prep_latent" if latent else "prep_context",
    )(*args)


def _gelu_tanh(x):
    return 0.5 * x * (1.0 + jnp.tanh(0.7978845608028654 * (x + 0.044715 * (x * x * x))))


def _softplus(z):
    return jnp.maximum(z, 0.0) + jnp.log1p(jnp.exp(-jnp.abs(z)))


def _lru_kernel(xa_ref, ga_ref, cw_ref, cb_ref, w_ref, p_ref, h0_ref, mix_ref, o_ref, st_ref,
                a_scr, b_scr, hf_scr, hb_scr, *, t_sub, tiles_per_iter):
    del mix_ref
    rows, c = xa_ref.shape
    n_sub = rows // t_sub
    x = xa_ref[...]
    tpos = lax.broadcasted_iota(jnp.int32, x.shape, 0) % t_sub
    cw = cw_ref[...]
    xm2 = jnp.where(tpos >= 2, pltpu.roll(x, 2, 0), 0.0)
    xm1 = jnp.where(tpos >= 1, pltpu.roll(x, 1, 0), 0.0)
    xp1 = jnp.where(tpos < t_sub - 1, pltpu.roll(x, rows - 1, 0), 0.0)
    xc = cb_ref[...] + xm2 * cw[0:1] + xm1 * cw[1:2] + x * cw[2:3] + xp1 * cw[3:4]
    xc16 = xc.astype(BF16)

    for d in range(2):
        g = _dot(xc16, w_ref[d])
        prm = p_ref[d]
        r = jax.nn.sigmoid(g[:, :c] + prm[0:1])
        ig = jax.nn.sigmoid(g[:, c:] + prm[1:2])
        log_a = (-LRU_C) * r * _softplus(-prm[2:3])
        a = jnp.exp(log_a)
        a_scr[d] = a
        b_scr[d] = jnp.sqrt(-jnp.tanh(log_a) * (a * a + 1.0)) * (ig * xc)

    ts = SUBLANE
    shifts = tuple(1 << n for n in range(ts.bit_length() - 1))
    sub_row = lax.broadcasted_iota(jnp.int32, (ts, c), 0)
    keep_f = [sub_row >= s for s in shifts]
    keep_b = [sub_row < ts - s for s in shifts]

    def tile_scan(a, b, carry, reverse):
        for n, s in enumerate(shifts):
            keep, shift = (keep_b[n], ts - s) if reverse else (keep_f[n], s)
            a_prev = jnp.where(keep, pltpu.roll(a, shift, 0), 1.0)
            b_prev = jnp.where(keep, pltpu.roll(b, shift, 0), 0.0)
            b = a * b_prev + b
            a = a * a_prev
        h = a * carry + b
        last = h[0:1] if reverse else h[ts - 1:ts]
        return h, jnp.broadcast_to(last, h.shape)

    def body(j, hs):
        hs = list(hs)
        for u in range(tiles_per_iter):
            off = pl.multiple_of((j * tiles_per_iter + u) * ts, ts)
            for q in range(n_sub):
                idx_f = pl.ds(q * t_sub + off, ts)
                idx_b = pl.ds(q * t_sub + (t_sub - ts) - off, ts)
                hf, hs[2 * q] = tile_scan(a_scr[0, idx_f, :], b_scr[0, idx_f, :], hs[2 * q], False)
                hb, hs[2 * q + 1] = tile_scan(a_scr[1, idx_b, :], b_scr[1, idx_b, :], hs[2 * q + 1], True)
                hf_scr[idx_f, :] = hf
                hb_scr[idx_b, :] = hb
        return tuple(hs)

    init = tuple(jnp.broadcast_to(h0_ref[q, d:d + 1, :], (ts, c)) for q in range(n_sub) for d in range(2))
    hs = lax.fori_loop(0, t_sub // (ts * tiles_per_iter), body, init)
    for q in range(n_sub):
        st_ref[q, 0:1, :] = hs[2 * q][0:1]
        st_ref[q, 1:2, :] = hs[2 * q + 1][0:1]
    o_ref[...] = ((hf_scr[...] + hb_scr[...]) * _gelu_tanh(ga_ref[...])).astype(o_ref.dtype)


def lru_mixer(p, conv_w, conv_b, w_gates, lru_prm, h0, mix, layer, *, latent):
    c = LRU_CHUNK
    rows = LRU_ROWS
    t_sub = DEC_SEQ if latent else SEQ
    n_sub = rows // t_sub
    n_rows = N_LAT if latent else N_CTX
    off = (N_CTX // rows) if latent else 0
    nck = LRU_WIDTH // c
    return pl.pallas_call(
        functools.partial(_lru_kernel, t_sub=t_sub, tiles_per_iter=4 if latent else 1),
        grid=(n_rows // rows, nck),
        in_specs=[
            pl.BlockSpec((rows, c), lambda s, k: (s + off, k)),
            pl.BlockSpec((rows, c), lambda s, k: (s + off, nck + k)),
            pl.BlockSpec((None, 4, c), lambda s, k: (layer, 0, k)),
            pl.BlockSpec((None, 1, c), lambda s, k: (layer, 0, k)),
            pl.BlockSpec((None, 2, None, c, 2 * c), lambda s, k: (layer, 0, k, 0, 0)),
            pl.BlockSpec((None, 2, 3, c), lambda s, k: (layer, 0, 0, k)),
            pl.BlockSpec((n_sub, 2, c), lambda s, k: (s, 0, k)),
            pl.BlockSpec(memory_space=pl.ANY),
        ],
        out_specs=[
            pl.BlockSpec((rows, c), lambda s, k: (s + off, MIX_A_COL // c + k)),
            pl.BlockSpec((n_sub, 2, c), lambda s, k: (s, 0, k)),
        ],
        out_shape=[
            jax.ShapeDtypeStruct(mix.shape, mix.dtype),
            jax.ShapeDtypeStruct((n_rows // t_sub, 2, LRU_WIDTH), F32),
        ],
        input_output_aliases={7: 0},
        scratch_shapes=[pltpu.VMEM((2, rows, c), F32)] * 2 + [pltpu.VMEM((rows, c), F32)] * 2,
        compiler_params=_params("parallel", "parallel"),
        name="lru_latent" if latent else "lru_context",
    )(p, p, conv_w, conv_b.reshape(DEPTH, 1, LRU_WIDTH), w_gates, lru_prm, h0, mix)


LOG2E = 1.4426950408889634


def _softmax_pv(s, v, scale):
    t = s * (scale * LOG2E)
    m = jnp.max(t, axis=-1, keepdims=True)
    p = jnp.exp2(t - m).astype(BF16)
    oa = _dot(p, jnp.concatenate([v, jnp.ones_like(v)], axis=1))
    return oa[:, :LANE] / oa[:, LANE:]


def _gqa_kernel(q_ref, k_ref, v_ref, mix_ref, o_ref, *, scale, groups, sub):
    del mix_ref
    tq = q_ref.shape[0]
    for g in range(groups):
        kv_sl = slice(g * LANE, (g + 1) * LANE)
        k, v = k_ref[:, kv_sl], v_ref[:, kv_sl]
        for h in range(g * GQA_REP, (g + 1) * GQA_REP):
            sl = slice(h * LANE, (h + 1) * LANE)
            for r0 in range(0, tq, sub):
                s = _dot_nt(q_ref[r0:r0 + sub, sl], k)
                o_ref[r0:r0 + sub, sl] = _softmax_pv(s, v, scale).astype(o_ref.dtype)


def gqa_attention(q, k, v, mix, row0, *, n_batch, t_q, t_k, tq, groups, sub, kv_b0=0):
    nq = t_q // tq
    qw = groups * GQA_REP * HEAD_DIM
    kw = groups * HEAD_DIM
    r_off, c_off = row0 // tq, MIX_B_COL // qw
    return pl.pallas_call(
        functools.partial(_gqa_kernel, scale=HEAD_DIM ** -0.5, groups=groups, sub=sub),
        grid=(n_batch, GQA_KV_HEADS // groups, nq),
        in_specs=[
            pl.BlockSpec((tq, qw), lambda b, g, i: (b * nq + i, g)),
            pl.BlockSpec((None, t_k, kw), lambda b, g, i: (kv_b0 + b, 0, g)),
            pl.BlockSpec((None, t_k, kw), lambda b, g, i: (kv_b0 + b, 0, g)),
            pl.BlockSpec(memory_space=pl.ANY),
        ],
        out_specs=pl.BlockSpec((tq, qw), lambda b, g, i: (r_off + b * nq + i, c_off + g)),
        out_shape=jax.ShapeDtypeStruct(mix.shape, mix.dtype),
        input_output_aliases={3: 0},
        compiler_params=_params("parallel", "parallel", "arbitrary"),
        name="gqa_attention",
    )(q, k, v, mix)


def _mla_kernel(qn_ref, qr_ref, kv_ref, kr_ref, mix_ref, o_ref, *, scale, heads, sub):
    del mix_ref
    tq = qn_ref.shape[0]
    kr = kr_ref[...]
    for h in range(heads):
        sl = slice(h * LANE, (h + 1) * LANE)
        k0 = h * (MLA_NOPE + MLA_V)
        k = jnp.concatenate([kv_ref[:, k0:k0 + MLA_NOPE], kr], axis=1)
        v = kv_ref[:, k0 + MLA_NOPE:k0 + MLA_NOPE + MLA_V]
        for r0 in range(0, tq, sub):
            q = jnp.concatenate([qn_ref[r0:r0 + sub, sl], qr_ref[r0:r0 + sub, sl]], axis=1)
            o_ref[r0:r0 + sub, sl] = _softmax_pv(_dot_nt(q, k), v, scale).astype(o_ref.dtype)


def mla_attention(q, kv, kr, mix, row0, *, n_batch, t_q, t_k, tq, heads, sub, kr_b0=0):
    nq = t_q // tq
    nh = MLA_HEADS // heads
    ow = heads * MLA_V
    r_off, c_off = row0 // tq, MIX_C_COL // ow
    return pl.pallas_call(
        functools.partial(_mla_kernel, scale=(MLA_NOPE + MLA_ROPE) ** -0.5, heads=heads, sub=sub),
        grid=(n_batch, nh, nq),
        in_specs=[
            pl.BlockSpec((tq, heads * MLA_NOPE), lambda b, h, i: (b * nq + i, h)),
            pl.BlockSpec((tq, heads * LANE), lambda b, h, i: (b * nq + i, nh + h)),
            pl.BlockSpec((None, t_k, heads * (MLA_NOPE + MLA_V)), lambda b, h, i: (b, 0, h)),
            pl.BlockSpec((None, t_k, LANE), lambda b, h, i: (kr_b0 + b, 0, 0)),
            pl.BlockSpec(memory_space=pl.ANY),
        ],
        out_specs=pl.BlockSpec((tq, ow), lambda b, h, i: (r_off + b * nq + i, c_off + h)),
        out_shape=jax.ShapeDtypeStruct(mix.shape, mix.dtype),
        input_output_aliases={4: 0},
        compiler_params=_params("parallel", "parallel", "arbitrary"),
        name="mla_attention",
    )(q, q, kv, kr, mix)


def _rope_tables(rot_dim):
    rows = DEC_SEQ // GRID_W
    row = jnp.repeat(jnp.arange(rows, dtype=F32), GRID_W)
    col = jnp.tile(jnp.arange(GRID_W, dtype=F32), rows)
    quarter = rot_dim // 4
    freqs = ROPE_THETA ** (-jnp.arange(quarter, dtype=F32) / quarter)
    ang = jnp.concatenate([row[:, None] * freqs, col[:, None] * freqs], axis=-1)
    cos, sin = jnp.cos(ang), jnp.sin(ang)
    zero = jnp.zeros_like(sin)
    c = jnp.repeat(cos, 2, axis=-1)
    se = jnp.stack([-sin, zero], axis=-1).reshape(DEC_SEQ, rot_dim)
    so = jnp.stack([zero, sin], axis=-1).reshape(DEC_SEQ, rot_dim)
    pad = ((0, 0), (0, LANE - rot_dim))
    return tuple(jnp.pad(t, pad) for t in (c, se, so))


def _block_diag_gates(lru_wr, lru_wi):
    per = LRU_CHUNK // LRU_BLOCK
    nck = LRU_WIDTH // LRU_CHUNK
    eye = jnp.eye(per, dtype=F32)

    def bd(w):
        w = w.reshape(DEPTH, 2, nck, per, LRU_BLOCK, LRU_BLOCK)
        w = jnp.einsum("dzcakj,ab->dzcakbj", w, eye)
        return w.reshape(DEPTH, 2, nck, LRU_CHUNK, LRU_CHUNK)

    return jnp.concatenate([bd(lru_wr), bd(lru_wi)], axis=-1).astype(BF16)


def _key_buffer(cache, width):
    c = jnp.swapaxes(cache, 0, 1).astype(BF16)
    c = jnp.pad(c, ((0, 0), (0, 0), (0, DEC_SEQ), (0, width - cache.shape[-1])))
    return c.reshape(DEPTH * DEC_BATCH * LAT_KEYS, width)


def _permute_w_uq(w_uq):
    w = w_uq.reshape(DEPTH, MLA_Q_RANK, MLA_HEADS, MLA_NOPE + MLA_ROPE)
    nope = w[..., :MLA_NOPE].reshape(DEPTH, MLA_Q_RANK, MLA_HEADS * MLA_NOPE)
    rope = jnp.pad(w[..., MLA_NOPE:], ((0, 0), (0, 0), (0, 0), (0, LANE - MLA_ROPE)))
    rope = rope.reshape(DEPTH, MLA_Q_RANK, MLA_HEADS * LANE)
    return jnp.concatenate([nope, rope], axis=-1).astype(BF16)


def kernel(x_prompt, x_sample, state_lru, cache_gqa_k, cache_gqa_v, cache_mla_ckv, cache_mla_krope, c,
           c_ctx, w_mod, b_mod, norm_mix_g, norm_ffn_g, w_in, conv_w, conv_b, lru_wr, lru_br, lru_wi, lru_bi,
           lru_lam, gqa_qn, gqa_kn, mla_qn, mla_kvn, w_uq, w_ukv, w_out, w_gate, w_up, w_down, norm_f):
    x = (x_prompt.reshape(N_CTX, D_MODEL), x_sample.reshape(N_LAT, D_MODEL))

    cond8 = jnp.zeros((SUBLANE, D_MODEL), F32).at[0].set(c_ctx).at[1:1 + DEC_BATCH].set(c)
    b_mod3 = b_mod.reshape(DEPTH, 1, N_MOD)
    mod = modulation(cond8, w_mod, b_mod3, 0)

    w_kr16 = jnp.pad(w_in[:, :, IN_MAIN:], ((0, 0), (0, 0), (0, LANE - MLA_ROPE))).astype(BF16)
    w_in_t = jnp.swapaxes(w_in, 1, 2)
    w_uq16 = _permute_w_uq(w_uq)
    w_gates = _block_diag_gates(lru_wr, lru_wi)
    lru_prm = jnp.stack([lru_br, lru_bi, lru_lam], axis=2)
    tables_b = _rope_tables(HEAD_DIM)
    tables_c = _rope_tables(MLA_ROPE)
    h0_ctx = jnp.zeros((BATCH, 2, LRU_WIDTH), F32)
    key_bufs = (_key_buffer(cache_gqa_k.reshape(DEC_BATCH, DEPTH, PAST_LEN, KV_WIDTH), KV_WIDTH),
                _key_buffer(cache_gqa_v.reshape(DEC_BATCH, DEPTH, PAST_LEN, KV_WIDTH), KV_WIDTH),
                _key_buffer(cache_mla_ckv, MLA_KV_RANK),
                _key_buffer(cache_mla_krope, LANE))

    mix = jnp.zeros((N_TOK, D_MIX), BF16)

    st_lru, st_k, st_v, st_ckv, st_kr = [], [], [], [], []
    for l in range(DEPTH):
        mod6 = mod.reshape(SUBLANE, 6, 1, D_MODEL)
        sh1, sc1, g1, sh2, sc2, g2 = (mod6[:, s] for s in range(6))

        h = norm_mod(x, norm_mix_g[l], sc1, sh1)
        if l == 0:
            p, kr = matmul_nt(h, w_in_t, w_kr16, l, n=IN_MAIN, **T_IN)
        else:
            p, kr = matmul_nt(h, w_in16, w_kr16, 0, n=IN_MAIN, w2_layer=l, **T_IN16)

        mix, s_lru = lru_mixer(p, conv_w, conv_b, w_gates, lru_prm, h0_ctx, mix, l, latent=False)
        mix, _ = lru_mixer(p, conv_w, conv_b, w_gates, lru_prm, state_lru[:, l], mix, l, latent=True)

        (q_c, k_c, v_c, cq_c, ckv_c, kr_c, kf_c, ckvf_c) = prep(
            p, kr, gqa_qn[l], gqa_kn[l], mla_qn[l], mla_kvn[l], latent=False)
        q_l, k_buf, v_buf, cq_l, ckv_buf, kr_buf = prep(
            p, kr, gqa_qn[l], gqa_kn[l], mla_qn[l], mla_kvn[l], latent=True,
            tables_b=tables_b, tables_c=tables_c, layer=l, key_bufs=key_bufs)
        key_bufs = (k_buf, v_buf, ckv_buf, kr_buf)
        req0 = l * DEC_BATCH

        mix = gqa_attention(q_c, k_c.reshape(BATCH, SEQ, KV_WIDTH), v_c.reshape(BATCH, SEQ, KV_WIDTH), mix, 0,
                            n_batch=BATCH, t_q=SEQ, t_k=SEQ, tq=SEQ, groups=GQA_KV_HEADS, sub=SEQ)
        mix = gqa_attention(q_l, k_buf.reshape(-1, LAT_KEYS, KV_WIDTH), v_buf.reshape(-1, LAT_KEYS, KV_WIDTH),
                            mix, N_CTX, n_batch=DEC_BATCH, t_q=DEC_SEQ, t_k=LAT_KEYS, groups=1, **A_GQA,
                            kv_b0=req0)

        qm_c = matmul(cq_c, w_uq16, l, out_dtype=BF16, **T_MLA)
        qm_l = matmul_rope_tail(cq_l, w_uq16, l, tables_c, tm=1024, tn=MLA_WIDTH, first_rope_tile=1)
        kv_c = matmul(ckv_c, w_ukv, l, out_dtype=BF16, **T_MLA)
        kv_l = matmul(ckv_buf, w_ukv, l, tm=LAT_KEYS, tn=T_MLA["tn"], out_dtype=BF16,
                      row0=req0 * LAT_KEYS, m=DEC_BATCH * LAT_KEYS)
        mix = mla_attention(qm_c, kv_c.reshape(BATCH, SEQ, -1), kr_c.reshape(BATCH, SEQ, LANE), mix, 0,
                            n_batch=BATCH, t_q=SEQ, t_k=SEQ, tq=SEQ, heads=MLA_HEADS, sub=SEQ)
        mix = mla_attention(qm_l, kv_l.reshape(DEC_BATCH, LAT_KEYS, -1), kr_buf.reshape(-1, LAT_KEYS, LANE),
                            mix, N_CTX, n_batch=DEC_BATCH, t_q=DEC_SEQ, t_k=LAT_KEYS, heads=1, **A_MLA,
                            kr_b0=req0)

        if l == 0:
            x = matmul_gated_residual(mix, w_out, l, x, g1, mix_cols=True, **T_OUT)
        else:
            x = matmul_gated_residual(mix, w_out16, 0, x, g1, mix_cols=True, **T_OUT16)

        h = norm_mod(x, norm_ffn_g[l], sc2, sh2)
        if l + 1 < DEPTH:
            ff, w_down16, mod, w_in16, w_out16 = swiglu_up(
                h, w_gate, w_up, w_down, l, next_args=(cond8, w_mod, b_mod3, w_in_t, w_out), **T_UP)
        else:
            ff, w_down16 = swiglu_up(h, w_gate, w_up, w_down, l, **T_UP)
        x = matmul_gated_residual(ff, w_down16, 0, x, g2, **T_DOWN)

        st_lru.append(s_lru)
        st_k.append(kf_c.reshape(BATCH, SEQ, GQA_KV_HEADS, HEAD_DIM))
        st_v.append(p[:N_CTX, 4096:4096 + KV_WIDTH].reshape(BATCH, SEQ, GQA_KV_HEADS, HEAD_DIM))
        st_ckv.append(ckvf_c.reshape(BATCH, SEQ, MLA_KV_RANK))
        st_kr.append(kr[:N_CTX, :MLA_ROPE].reshape(BATCH, SEQ, MLA_ROPE))

    y_ctx = final_norm(x, norm_f, 0, N_CTX)
    y_lat = final_norm(x, norm_f, N_CTX, N_LAT)
    return (y_ctx.reshape(BATCH, SEQ, D_MODEL), y_lat.reshape(DEC_BATCH, DEC_SEQ, D_MODEL),
            jnp.stack(st_lru, axis=1), jnp.stack(st_k, axis=1), jnp.stack(st_v, axis=1),
            jnp.stack(st_ckv, axis=1), jnp.stack(st_kr, axis=1))
```

```python
import functools

import jax
import jax.numpy as jnp
from jax import lax
from jax.experimental import pallas as pl
from jax.experimental.pallas import tpu as pltpu

F32 = jnp.float32
BF16 = jnp.bfloat16

D_MODEL = 4096
BATCH = 16
SEQ = 256
DEPTH = 4
DEC_BATCH = 2
DEC_SEQ = 2048
PAST_LEN = 512
GRID_W = 64
EPS = 1e-6
ROPE_THETA = 10000.0

LRU_WIDTH = 1024
LRU_BLOCKS = 16
LRU_BLOCK = 64
LRU_C = 8.0
HEAD_DIM = 128
GQA_HEADS = 12
GQA_KV_HEADS = 4
GQA_REP = GQA_HEADS // GQA_KV_HEADS
GQA_WIDTH = GQA_HEADS * HEAD_DIM
KV_WIDTH = GQA_KV_HEADS * HEAD_DIM
MLA_HEADS = 12
MLA_Q_RANK = 1024
MLA_KV_RANK = 512
MLA_NOPE = 128
MLA_ROPE = 64
MLA_V = 128
MLA_WIDTH = MLA_HEADS * MLA_V
D_MIX = LRU_WIDTH + GQA_WIDTH + MLA_WIDTH
MIX_B_COL = 0
MIX_C_COL = GQA_WIDTH
MIX_A_COL = GQA_WIDTH + MLA_WIDTH
D_FF = 11008
IN_MAIN = 6144
LANE = 128
N_CTX = BATCH * SEQ
N_LAT = DEC_BATCH * DEC_SEQ
N_TOK = N_CTX + N_LAT
LAT_KEYS = PAST_LEN + DEC_SEQ
VMEM_LIMIT = 56 * 1024 * 1024
LRU_CHUNK = 256
LRU_ROWS = 2048
ROW_TILE = 256
NORM_TILE = 512
NORM_CHUNK = 16
SUBLANE = 8

T_IN = dict(tm=1024, tn=512)
T_OUT = dict(tm=1024, tn=512)
T_IN16 = dict(tm=1024, tn=1024)
T_OUT16 = dict(tm=1024, tn=512)
T_UP = dict(tm=2048, tn=256)
T_DOWN = dict(tm=512, tn=512)
T_MLA = dict(tm=2048, tn=1024)
A_GQA = dict(tq=1024, sub=128)
A_MLA = dict(tq=2048, sub=256)


def _params(*sem):
    return pltpu.CompilerParams(dimension_semantics=sem, vmem_limit_bytes=VMEM_LIMIT)


def _cond_of_tile(i, tm):
    row = i * tm
    return jnp.where(row < N_CTX, 0, 1 + (row - N_CTX) // DEC_SEQ)


def _silu(x):
    return x * jax.nn.sigmoid(x)


def _dot(a, b):
    return jnp.dot(a, b, preferred_element_type=F32)


def _dot_nt(a, b):
    return lax.dot_general(a, b, (((1,), (1,)), ((), ())), preferred_element_type=F32)


N_MOD = 6 * D_MODEL
MOD_SLAB = 256
NEXT_CAST_STEPS = 128


def _mod_tile(c_ref, w_ref, b_ref):
    return _dot(_silu(c_ref[...]).astype(BF16), _w16(w_ref)) + b_ref[...]


def _mod_kernel(c_ref, w_ref, b_ref, o_ref):
    o_ref[...] = _mod_tile(c_ref, w_ref, b_ref)


def modulation(cond8, w_mod, b_mod3, layer):
    tn = 512
    return pl.pallas_call(
        _mod_kernel,
        grid=(N_MOD // tn,),
        in_specs=[
            pl.BlockSpec((SUBLANE, D_MODEL), lambda j: (0, 0)),
            pl.BlockSpec((None, D_MODEL, tn), lambda j: (layer, 0, j)),
            pl.BlockSpec((None, 1, tn), lambda j: (layer, 0, j)),
        ],
        out_specs=pl.BlockSpec((SUBLANE, tn), lambda j: (0, j)),
        out_shape=jax.ShapeDtypeStruct((SUBLANE, N_MOD), F32),
        compiler_params=_params("parallel"),
        name="modulation",
    )(cond8, w_mod, b_mod3)


def _row_parts(x, tm, tn=None):
    def spec(row_fn, col_fn):
        if tn is None:
            return pl.BlockSpec((tm, D_MODEL), lambda i: (row_fn(i), 0))
        return pl.BlockSpec((tm, tn), lambda i, j: (row_fn(i), col_fn(i, j)))

    if not isinstance(x, tuple):
        return [x], [spec(lambda i: i, lambda i, j: j)]
    n_a = x[0].shape[0] // tm
    return list(x), [
        spec(lambda i: jnp.minimum(i, n_a - 1), lambda i, j: jnp.where(i < n_a, j, 0)),
        spec(lambda i: jnp.maximum(i - n_a, 0), lambda i, j: jnp.where(i < n_a, 0, j)),
    ]


def _read_rows(refs, tm, rows=slice(None)):
    if len(refs) == 1:
        return refs[0][rows, :]
    n_a = N_CTX // tm
    return jnp.where(pl.program_id(0) < n_a, refs[0][rows, :], refs[1][rows, :])


def _rmsnorm_rows(x_refs, o_ref, scale, shift):
    tm = o_ref.shape[0]

    def body(c, carry):
        rows = pl.ds(pl.multiple_of(c * NORM_CHUNK, NORM_CHUNK), NORM_CHUNK)
        x = _read_rows(x_refs, tm, rows)
        y = x * lax.rsqrt(jnp.mean(x * x, axis=-1, keepdims=True) + EPS) * scale
        if shift is not None:
            y = y + shift
        o_ref[rows, :] = y.astype(o_ref.dtype)
        return carry

    lax.fori_loop(0, tm // NORM_CHUNK, body, 0, unroll=4)


def _norm_mod_kernel(*refs):
    *x_refs, g_ref, sc_ref, sh_ref, o_ref = refs
    _rmsnorm_rows(x_refs, o_ref, g_ref[...] * (1.0 + sc_ref[...]), sh_ref[...])


def norm_mod(x, g, sc, sh):
    tm = NORM_TILE
    x_args, x_specs = _row_parts(x, tm)
    return pl.pallas_call(
        _norm_mod_kernel,
        grid=(N_TOK // tm,),
        in_specs=x_specs + [
            pl.BlockSpec((1, D_MODEL), lambda i: (0, 0)),
            pl.BlockSpec((None, 1, D_MODEL), lambda i: (_cond_of_tile(i, tm), 0, 0)),
            pl.BlockSpec((None, 1, D_MODEL), lambda i: (_cond_of_tile(i, tm), 0, 0)),
        ],
        out_specs=pl.BlockSpec((tm, D_MODEL), lambda i: (i, 0)),
        out_shape=jax.ShapeDtypeStruct((N_TOK, D_MODEL), BF16),
        compiler_params=_params("parallel"),
        name="norm_mod",
    )(*x_args, g.reshape(1, D_MODEL), sc, sh)


def _final_norm_kernel(x_ref, g_ref, o_ref):
    _rmsnorm_rows([x_ref], o_ref, g_ref[...], None)


def final_norm(x, g, row0, n_rows):
    tm = NORM_TILE
    off = row0 // tm
    return pl.pallas_call(
        _final_norm_kernel,
        grid=(n_rows // tm,),
        in_specs=[
            pl.BlockSpec((tm, D_MODEL), lambda i: (i + off, 0)),
            pl.BlockSpec((1, D_MODEL), lambda i: (0, 0)),
        ],
        out_specs=pl.BlockSpec((tm, D_MODEL), lambda i: (i, 0)),
        out_shape=jax.ShapeDtypeStruct((n_rows, D_MODEL), F32),
        compiler_params=_params("parallel"),
        name="final_norm",
    )(x, g.reshape(1, D_MODEL))


def _w16(w_ref):
    return w_ref[...].astype(BF16)


def _mm_kernel(x_ref, w_ref, o_ref):
    o_ref[...] = _dot(x_ref[...], _w16(w_ref)).astype(o_ref.dtype)


def matmul(x, w, layer, *, tm, tn, out_dtype, row0=0, m=None):
    k = x.shape[1]
    m = x.shape[0] if m is None else m
    n = w.shape[-1]
    r_off = row0 // tm
    return pl.pallas_call(
        _mm_kernel,
        grid=(m // tm, n // tn),
        in_specs=[
            pl.BlockSpec((tm, k), lambda i, j: (r_off + i, 0)),
            pl.BlockSpec((None, k, tn), lambda i, j: (layer, 0, j)),
        ],
        out_specs=pl.BlockSpec((tm, tn), lambda i, j: (i, j)),
        out_shape=jax.ShapeDtypeStruct((m, n), out_dtype),
        compiler_params=_params("parallel", "arbitrary"),
        name="matmul",
    )(x, w)


def _mm_nt_kernel(x_ref, w_ref, w2_ref, o_ref, o2_ref):
    x = x_ref[...]
    o_ref[...] = _dot_nt(x, _w16(w_ref)).astype(o_ref.dtype)

    @pl.when(pl.program_id(1) == 0)
    def _():
        o2_ref[...] = _dot(x, w2_ref[...])


def matmul_nt(x, w_t, w2, layer, *, tm, tn, n, w2_layer=None):
    m, k = x.shape
    n2 = w2.shape[-1]
    w2_layer = layer if w2_layer is None else w2_layer
    return pl.pallas_call(
        _mm_nt_kernel,
        grid=(m // tm, n // tn),
        in_specs=[
            pl.BlockSpec((tm, k), lambda i, j: (i, 0)),
            pl.BlockSpec((None, tn, k), lambda i, j: (layer, j, 0)),
            pl.BlockSpec((None, k, n2), lambda i, j: (w2_layer, 0, 0)),
        ],
        out_specs=[
            pl.BlockSpec((tm, tn), lambda i, j: (i, j)),
            pl.BlockSpec((tm, n2), lambda i, j: (i, 0)),
        ],
        out_shape=[
            jax.ShapeDtypeStruct((m, n), F32),
            jax.ShapeDtypeStruct((m, n2), F32),
        ],
        compiler_params=_params("parallel", "arbitrary"),
        name="matmul_nt",
    )(x, w_t, w2)


def _mm_res_kernel(x_ref, w_ref, *refs, mix_cols):
    *r_refs, g_ref, o_ref = refs
    w = _w16(w_ref)
    if mix_cols:
        acc = (_dot(x_ref[:, :MIX_A_COL], w[LRU_WIDTH:]) + _dot(x_ref[:, MIX_A_COL:], w[:LRU_WIDTH]))
    else:
        acc = _dot(x_ref[...], w)
    o_ref[...] = _read_rows(r_refs, o_ref.shape[0]) + g_ref[...] * acc


def matmul_gated_residual(x, w, layer, res, gate, *, tm, tn, mix_cols=False):
    m, k = x.shape
    n = w.shape[-1]
    r_args, r_specs = _row_parts(res, tm, tn)
    return pl.pallas_call(
        functools.partial(_mm_res_kernel, mix_cols=mix_cols),
        grid=(m // tm, n // tn),
        in_specs=[
            pl.BlockSpec((tm, k), lambda i, j: (i, 0)),
            pl.BlockSpec((None, k, tn), lambda i, j: (layer, 0, j)),
        ] + r_specs + [
            pl.BlockSpec((None, 1, tn), lambda i, j: (_cond_of_tile(i, tm), 0, j)),
        ],
        out_specs=pl.BlockSpec((tm, tn), lambda i, j: (i, j)),
        out_shape=jax.ShapeDtypeStruct((m, n), F32),
        compiler_params=_params("parallel", "arbitrary"),
        name="matmul_gated_residual",
    )(x, w, *r_args, gate)


def _swiglu_kernel(x_ref, wg_ref, wu_ref, wd_ref, *refs, next_layer):
    if next_layer:
        c_ref, wm_ref, bm_ref, win_ref, wout_ref, o_ref, wd16_ref, mod_ref, win16_ref, wout16_ref = refs
    else:
        o_ref, wd16_ref = refs
    x = x_ref[...]
    g = _dot(x, _w16(wg_ref))
    u = _dot(x, _w16(wu_ref))
    o_ref[...] = (_silu(g) * u).astype(o_ref.dtype)
    wd16_ref[...] = wd_ref[...].astype(BF16)
    if next_layer:
        step = pl.program_id(0) * pl.num_programs(1) + pl.program_id(1)

        @pl.when(step < N_MOD // MOD_SLAB)
        def _():
            mod_ref[...] = _mod_tile(c_ref, wm_ref, bm_ref)

        @pl.when(step < NEXT_CAST_STEPS)
        def _():
            win16_ref[...] = win_ref[...].astype(BF16)
            wout16_ref[...] = wout_ref[...].astype(BF16)


def swiglu_up(x, w_gate, w_up, w_down, layer, *, tm, tn, next_args=None):
    m, k = x.shape
    n = w_gate.shape[-1]
    nj = n // tn
    steps = (m // tm) * nj
    kd, nd = w_down.shape[1:]
    slab = kd // steps
    assert slab * steps == kd and slab % 16 == 0
    in_specs = [
        pl.BlockSpec((tm, k), lambda i, j: (i, 0), pipeline_mode=pl.Buffered(1)),
        pl.BlockSpec((None, k, tn), lambda i, j: (layer, 0, j)),
        pl.BlockSpec((None, k, tn), lambda i, j: (layer, 0, j)),
        pl.BlockSpec((None, slab, nd), lambda i, j: (layer, i * nj + j, 0)),
    ]
    out_specs = [
        pl.BlockSpec((tm, tn), lambda i, j: (i, j)),
        pl.BlockSpec((None, slab, nd), lambda i, j: (0, i * nj + j, 0)),
    ]
    out_shape = [
        jax.ShapeDtypeStruct((m, n), BF16),
        jax.ShapeDtypeStruct((1, kd, nd), BF16),
    ]
    args = [x, w_gate, w_up, w_down]
    if next_args is not None:
        mod_steps = N_MOD // MOD_SLAB
        assert max(mod_steps, NEXT_CAST_STEPS) <= steps
        in_rows, out_rows = IN_MAIN // NEXT_CAST_STEPS, D_MIX // NEXT_CAST_STEPS
        assert in_rows % 16 == 0 and out_rows % 16 == 0
        mod_slab = lambda i, j: jnp.minimum(i * nj + j, mod_steps - 1)
        cast_slab = lambda i, j: jnp.minimum(i * nj + j, NEXT_CAST_STEPS - 1)
        in_specs += [
            pl.BlockSpec((SUBLANE, D_MODEL), lambda i, j: (0, 0)),
            pl.BlockSpec((None, D_MODEL, MOD_SLAB), lambda i, j: (layer + 1, 0, mod_slab(i, j))),
            pl.BlockSpec((None, 1, MOD_SLAB), lambda i, j: (layer + 1, 0, mod_slab(i, j))),
            pl.BlockSpec((None, in_rows, D_MODEL), lambda i, j: (layer + 1, cast_slab(i, j), 0)),
            pl.BlockSpec((None, out_rows, D_MODEL), lambda i, j: (layer + 1, cast_slab(i, j), 0)),
        ]
        out_specs += [
            pl.BlockSpec((SUBLANE, MOD_SLAB), lambda i, j: (0, mod_slab(i, j))),
            pl.BlockSpec((None, in_rows, D_MODEL), lambda i, j: (0, cast_slab(i, j), 0)),
            pl.BlockSpec((None, out_rows, D_MODEL), lambda i, j: (0, cast_slab(i, j), 0)),
        ]
        out_shape += [
            jax.ShapeDtypeStruct((SUBLANE, N_MOD), F32),
            jax.ShapeDtypeStruct((1, IN_MAIN, D_MODEL), BF16),
            jax.ShapeDtypeStruct((1, D_MIX, D_MODEL), BF16),
        ]
        args += list(next_args)
    return pl.pallas_call(
        functools.partial(_swiglu_kernel, next_layer=next_args is not None),
        grid=(m // tm, nj),
        in_specs=in_specs,
        out_specs=out_specs,
        out_shape=out_shape,
        compiler_params=_params("arbitrary", "arbitrary"),
        name="swiglu_up",
    )(*args)


def _rope(x, c, se, so):
    nxt = pltpu.roll(x, LANE - 1, 1)
    prv = pltpu.roll(x, 1, 1)
    return x * c + nxt * se + prv * so


def _mm_rope_kernel(x_ref, w_ref, c_ref, se_ref, so_ref, o_ref, *, first_rope_tile):
    acc = _dot(x_ref[...], _w16(w_ref))
    j = pl.program_id(1)

    @pl.when(j < first_rope_tile)
    def _():
        o_ref[...] = acc.astype(o_ref.dtype)

    @pl.when(j >= first_rope_tile)
    def _():
        c, se, so = c_ref[...], se_ref[...], so_ref[...]
        for h in range(acc.shape[1] // LANE):
            sl = slice(h * LANE, (h + 1) * LANE)
            o_ref[:, sl] = _rope(acc[:, sl], c, se, so).astype(o_ref.dtype)


def matmul_rope_tail(x, w, layer, tables, *, tm, tn, first_rope_tile):
    m, k = x.shape
    n = w.shape[-1]
    nt = DEC_SEQ // tm
    tab = pl.BlockSpec((tm, LANE), lambda i, j: (i % nt, 0))
    return pl.pallas_call(
        functools.partial(_mm_rope_kernel, first_rope_tile=first_rope_tile),
        grid=(m // tm, n // tn),
        in_specs=[
            pl.BlockSpec((tm, k), lambda i, j: (i, 0)),
            pl.BlockSpec((None, k, tn), lambda i, j: (layer, 0, j)),
            tab, tab, tab,
        ],
        out_specs=pl.BlockSpec((tm, tn), lambda i, j: (i, j)),
        out_shape=jax.ShapeDtypeStruct((m, n), BF16),
        compiler_params=_params("parallel", "arbitrary"),
        name="matmul_rope_tail",
    )(x, w, *tables)


def _head_rmsnorm(x, g):
    return x * lax.rsqrt(jnp.mean(x * x, axis=-1, keepdims=True) + EPS) * g


def _prep_kernel(*refs, rope):
    if rope:
        (qk_ref, v_ref, c_ref, kr_ref, qn_ref, kn_ref, mqn_ref, mkvn_ref,
         cb_ref, seb_ref, sob_ref, cc_ref, sec_ref, soc_ref,
         _k_buf, _v_buf, _ckv_buf, _kr_buf,
         q_o, k_o, v_o, cq_o, ckv_o, kr_o) = refs
    else:
        (qk_ref, v_ref, c_ref, kr_ref, qn_ref, kn_ref, mqn_ref, mkvn_ref,
         q_o, k_o, v_o, cq_o, ckv_o, kr_o, kf_o, ckvf_o) = refs
    qn, kn = qn_ref[...], kn_ref[...]
    if rope:
        cb, seb, sob = cb_ref[...], seb_ref[...], sob_ref[...]
    for h in range(GQA_HEADS + GQA_KV_HEADS):
        sl = slice(h * LANE, (h + 1) * LANE)
        is_q = h < GQA_HEADS
        y = _head_rmsnorm(qk_ref[:, sl], qn if is_q else kn)
        osl = sl if is_q else slice((h - GQA_HEADS) * LANE, (h - GQA_HEADS + 1) * LANE)
        if not rope and not is_q:
            kf_o[:, osl] = y
        if rope:
            y = _rope(y, cb, seb, sob)
        (q_o if is_q else k_o)[:, osl] = y.astype(BF16)
    v_o[...] = v_ref[...].astype(BF16)
    cq = c_ref[:, :MLA_Q_RANK]
    cq_o[...] = _head_rmsnorm(cq, mqn_ref[...]).astype(BF16)
    ckv = _head_rmsnorm(c_ref[:, MLA_Q_RANK:], mkvn_ref[...])
    ckv_o[...] = ckv.astype(BF16)
    if not rope:
        ckvf_o[...] = ckv
    kr = kr_ref[...]
    if rope:
        kr = _rope(kr, cc_ref[...], sec_ref[...], soc_ref[...])
    kr_o[...] = kr.astype(BF16)


def prep(p, kr, gqa_qn, gqa_kn, mla_qn, mla_kvn, *, latent, tables_b=None, tables_c=None,
         layer=None, key_bufs=None):
    tm = ROW_TILE
    n_rows = N_LAT if latent else N_CTX
    off = (N_CTX // tm) if latent else 0
    nt = DEC_SEQ // tm
    row = lambda c: (lambda i: (i + off, c))
    in_specs = [
        pl.BlockSpec((tm, 2048), row(1)),
        pl.BlockSpec((tm, KV_WIDTH), row(8)),
        pl.BlockSpec((tm, 1536), row(3)),
        pl.BlockSpec((tm, LANE), row(0)),
        pl.BlockSpec((1, HEAD_DIM), lambda i: (0, 0)),
        pl.BlockSpec((1, HEAD_DIM), lambda i: (0, 0)),
        pl.BlockSpec((1, MLA_Q_RANK), lambda i: (0, 0)),
        pl.BlockSpec((1, MLA_KV_RANK), lambda i: (0, 0)),
    ]
    args = [p, p, p, kr, gqa_qn.reshape(1, -1), gqa_kn.reshape(1, -1),
            mla_qn.reshape(1, -1), mla_kvn.reshape(1, -1)]
    out = lambda w, dt: (pl.BlockSpec((tm, w), lambda i: (i, 0)), jax.ShapeDtypeStruct((n_rows, w), dt))
    outs = [out(GQA_WIDTH, BF16), out(KV_WIDTH, BF16), out(KV_WIDTH, BF16),
            out(MLA_Q_RANK, BF16), out(MLA_KV_RANK, BF16), out(LANE, BF16)]
    aliases = {}
    if latent:
        tab = pl.BlockSpec((tm, LANE), lambda i: (i % nt, 0))
        in_specs += [tab] * 6
        args += list(tables_b) + list(tables_c)
        blocks_per_req = LAT_KEYS // tm

        def key_rows(i):
            return ((layer * DEC_BATCH + i // nt) * blocks_per_req + PAST_LEN // tm + i % nt, 0)

        for n_out, buf in zip((1, 2, 4, 5), key_bufs):
            aliases[len(args)] = n_out
            in_specs.append(pl.BlockSpec(memory_space=pl.ANY))
            args.append(buf)
            outs[n_out] = (pl.BlockSpec((tm, buf.shape[1]), key_rows), jax.ShapeDtypeStruct(buf.shape, buf.dtype))
    else:
        outs += [out(KV_WIDTH, F32), out(MLA_KV_RANK, F32)]
    return pl.pallas_call(
        functools.partial(_prep_kernel, rope=latent),
        grid=(n_rows // tm,),
        in_specs=in_specs,
        out_specs=[o[0] for o in outs],
        out_shape=[o[1] for o in outs],
        input_output_aliases=aliases,
        compiler_params=_params("parallel"),
        name="prep_latent" if latent else "prep_context",
    )(*args)


def _gelu_tanh(x):
    return 0.5 * x * (1.0 + jnp.tanh(0.7978845608028654 * (x + 0.044715 * (x * x * x))))


def _softplus(z):
    return jnp.maximum(z, 0.0) + jnp.log1p(jnp.exp(-jnp.abs(z)))


def _lru_kernel(xa_ref, ga_ref, cw_ref, cb_ref, w_ref, p_ref, h0_ref, mix_ref, o_ref, st_ref,
                a_scr, b_scr, hf_scr, hb_scr, *, t_sub, tiles_per_iter):
    del mix_ref
    rows, c = xa_ref.shape
    n_sub = rows // t_sub
    x = xa_ref[...]
    tpos = lax.broadcasted_iota(jnp.int32, x.shape, 0) % t_sub
    cw = cw_ref[...]
    xm2 = jnp.where(tpos >= 2, pltpu.roll(x, 2, 0), 0.0)
    xm1 = jnp.where(tpos >= 1, pltpu.roll(x, 1, 0), 0.0)
    xp1 = jnp.where(tpos < t_sub - 1, pltpu.roll(x, rows - 1, 0), 0.0)
    xc = cb_ref[...] + xm2 * cw[0:1] + xm1 * cw[1:2] + x * cw[2:3] + xp1 * cw[3:4]
    xc16 = xc.astype(BF16)

    for d in range(2):
        g = _dot(xc16, w_ref[d])
        prm = p_ref[d]
        r = jax.nn.sigmoid(g[:, :c] + prm[0:1])
        ig = jax.nn.sigmoid(g[:, c:] + prm[1:2])
        log_a = (-LRU_C) * r * _softplus(-prm[2:3])
        a = jnp.exp(log_a)
        a_scr[d] = a
        b_scr[d] = jnp.sqrt(-jnp.tanh(log_a) * (a * a + 1.0)) * (ig * xc)

    ts = SUBLANE
    shifts = tuple(1 << n for n in range(ts.bit_length() - 1))
    sub_row = lax.broadcasted_iota(jnp.int32, (ts, c), 0)
    keep_f = [sub_row >= s for s in shifts]
    keep_b = [sub_row < ts - s for s in shifts]

    def tile_scan(a, b, carry, reverse):
        for n, s in enumerate(shifts):
            keep, shift = (keep_b[n], ts - s) if reverse else (keep_f[n], s)
            a_prev = jnp.where(keep, pltpu.roll(a, shift, 0), 1.0)
            b_prev = jnp.where(keep, pltpu.roll(b, shift, 0), 0.0)
            b = a * b_prev + b
            a = a * a_prev
        h = a * carry + b
        last = h[0:1] if reverse else h[ts - 1:ts]
        return h, jnp.broadcast_to(last, h.shape)

    def body(j, hs):
        hs = list(hs)
        for u in range(tiles_per_iter):
            off = pl.multiple_of((j * tiles_per_iter + u) * ts, ts)
            for q in range(n_sub):
                idx_f = pl.ds(q * t_sub + off, ts)
                idx_b = pl.ds(q * t_sub + (t_sub - ts) - off, ts)
                hf, hs[2 * q] = tile_scan(a_scr[0, idx_f, :], b_scr[0, idx_f, :], hs[2 * q], False)
                hb, hs[2 * q + 1] = tile_scan(a_scr[1, idx_b, :], b_scr[1, idx_b, :], hs[2 * q + 1], True)
                hf_scr[idx_f, :] = hf
                hb_scr[idx_b, :] = hb
        return tuple(hs)

    init = tuple(jnp.broadcast_to(h0_ref[q, d:d + 1, :], (ts, c)) for q in range(n_sub) for d in range(2))
    hs = lax.fori_loop(0, t_sub // (ts * tiles_per_iter), body, init)
    for q in range(n_sub):
        st_ref[q, 0:1, :] = hs[2 * q][0:1]
        st_ref[q, 1:2, :] = hs[2 * q + 1][0:1]
    o_ref[...] = ((hf_scr[...] + hb_scr[...]) * _gelu_tanh(ga_ref[...])).astype(o_ref.dtype)


def lru_mixer(p, conv_w, conv_b, w_gates, lru_prm, h0, mix, layer, *, latent):
    c = LRU_CHUNK
    rows = LRU_ROWS
    t_sub = DEC_SEQ if latent else SEQ
    n_sub = rows // t_sub
    n_rows = N_LAT if latent else N_CTX
    off = (N_CTX // rows) if latent else 0
    nck = LRU_WIDTH // c
    return pl.pallas_call(
        functools.partial(_lru_kernel, t_sub=t_sub, tiles_per_iter=4 if latent else 1),
        grid=(n_rows // rows, nck),
        in_specs=[
            pl.BlockSpec((rows, c), lambda s, k: (s + off, k)),
            pl.BlockSpec((rows, c), lambda s, k: (s + off, nck + k)),
            pl.BlockSpec((None, 4, c), lambda s, k: (layer, 0, k)),
            pl.BlockSpec((None, 1, c), lambda s, k: (layer, 0, k)),
            pl.BlockSpec((None, 2, None, c, 2 * c), lambda s, k: (layer, 0, k, 0, 0)),
            pl.BlockSpec((None, 2, 3, c), lambda s, k: (layer, 0, 0, k)),
            pl.BlockSpec((n_sub, 2, c), lambda s, k: (s, 0, k)),
            pl.BlockSpec(memory_space=pl.ANY),
        ],
        out_specs=[
            pl.BlockSpec((rows, c), lambda s, k: (s + off, MIX_A_COL // c + k)),
            pl.BlockSpec((n_sub, 2, c), lambda s, k: (s, 0, k)),
        ],
        out_shape=[
            jax.ShapeDtypeStruct(mix.shape, mix.dtype),
            jax.ShapeDtypeStruct((n_rows // t_sub, 2, LRU_WIDTH), F32),
        ],
        input_output_aliases={7: 0},
        scratch_shapes=[pltpu.VMEM((2, rows, c), F32)] * 2 + [pltpu.VMEM((rows, c), F32)] * 2,
        compiler_params=_params("parallel", "parallel"),
        name="lru_latent" if latent else "lru_context",
    )(p, p, conv_w, conv_b.reshape(DEPTH, 1, LRU_WIDTH), w_gates, lru_prm, h0, mix)


LOG2E = 1.4426950408889634


def _softmax_pv(s, v, scale):
    t = s * (scale * LOG2E)
    m = jnp.max(t, axis=-1, keepdims=True)
    p = jnp.exp2(t - m).astype(BF16)
    oa = _dot(p, jnp.concatenate([v, jnp.ones_like(v)], axis=1))
    return oa[:, :LANE] / oa[:, LANE:]


def _gqa_kernel(q_ref, k_ref, v_ref, mix_ref, o_ref, *, scale, groups, sub):
    del mix_ref
    tq = q_ref.shape[0]
    for g in range(groups):
        kv_sl = slice(g * LANE, (g + 1) * LANE)
        k, v = k_ref[:, kv_sl], v_ref[:, kv_sl]
        for h in range(g * GQA_REP, (g + 1) * GQA_REP):
            sl = slice(h * LANE, (h + 1) * LANE)
            for r0 in range(0, tq, sub):
                s = _dot_nt(q_ref[r0:r0 + sub, sl], k)
                o_ref[r0:r0 + sub, sl] = _softmax_pv(s, v, scale).astype(o_ref.dtype)


def gqa_attention(q, k, v, mix, row0, *, n_batch, t_q, t_k, tq, groups, sub, kv_b0=0):
    nq = t_q // tq
    qw = groups * GQA_REP * HEAD_DIM
    kw = groups * HEAD_DIM
    r_off, c_off = row0 // tq, MIX_B_COL // qw
    return pl.pallas_call(
        functools.partial(_gqa_kernel, scale=HEAD_DIM ** -0.5, groups=groups, sub=sub),
        grid=(n_batch, GQA_KV_HEADS // groups, nq),
        in_specs=[
            pl.BlockSpec((tq, qw), lambda b, g, i: (b * nq + i, g)),
            pl.BlockSpec((None, t_k, kw), lambda b, g, i: (kv_b0 + b, 0, g)),
            pl.BlockSpec((None, t_k, kw), lambda b, g, i: (kv_b0 + b, 0, g)),
            pl.BlockSpec(memory_space=pl.ANY),
        ],
        out_specs=pl.BlockSpec((tq, qw), lambda b, g, i: (r_off + b * nq + i, c_off + g)),
        out_shape=jax.ShapeDtypeStruct(mix.shape, mix.dtype),
        input_output_aliases={3: 0},
        compiler_params=_params("parallel", "parallel", "arbitrary"),
        name="gqa_attention",
    )(q, k, v, mix)


def _mla_kernel(qn_ref, qr_ref, kv_ref, kr_ref, mix_ref, o_ref, *, scale, heads, sub):
    del mix_ref
    tq = qn_ref.shape[0]
    kr = kr_ref[...]
    for h in range(heads):
        sl = slice(h * LANE, (h + 1) * LANE)
        k0 = h * (MLA_NOPE + MLA_V)
        k = jnp.concatenate([kv_ref[:, k0:k0 + MLA_NOPE], kr], axis=1)
        v = kv_ref[:, k0 + MLA_NOPE:k0 + MLA_NOPE + MLA_V]
        for r0 in range(0, tq, sub):
            q = jnp.concatenate([qn_ref[r0:r0 + sub, sl], qr_ref[r0:r0 + sub, sl]], axis=1)
            o_ref[r0:r0 + sub, sl] = _softmax_pv(_dot_nt(q, k), v, scale).astype(o_ref.dtype)


def mla_attention(q, kv, kr, mix, row0, *, n_batch, t_q, t_k, tq, heads, sub, kr_b0=0):
    nq = t_q // tq
    nh = MLA_HEADS // heads
    ow = heads * MLA_V
    r_off, c_off = row0 // tq, MIX_C_COL // ow
    return pl.pallas_call(
        functools.partial(_mla_kernel, scale=(MLA_NOPE + MLA_ROPE) ** -0.5, heads=heads, sub=sub),
        grid=(n_batch, nh, nq),
        in_specs=[
            pl.BlockSpec((tq, heads * MLA_NOPE), lambda b, h, i: (b * nq + i, h)),
            pl.BlockSpec((tq, heads * LANE), lambda b, h, i: (b * nq + i, nh + h)),
            pl.BlockSpec((None, t_k, heads * (MLA_NOPE + MLA_V)), lambda b, h, i: (b, 0, h)),
            pl.BlockSpec((None, t_k, LANE), lambda b, h, i: (kr_b0 + b, 0, 0)),
            pl.BlockSpec(memory_space=pl.ANY),
        ],
        out_specs=pl.BlockSpec((tq, ow), lambda b, h, i: (r_off + b * nq + i, c_off + h)),
        out_shape=jax.ShapeDtypeStruct(mix.shape, mix.dtype),
        input_output_aliases={4: 0},
        compiler_params=_params("parallel", "parallel", "arbitrary"),
        name="mla_attention",
    )(q, q, kv, kr, mix)


def _rope_tables(rot_dim):
    rows = DEC_SEQ // GRID_W
    row = jnp.repeat(jnp.arange(rows, dtype=F32), GRID_W)
    col = jnp.tile(jnp.arange(GRID_W, dtype=F32), rows)
    quarter = rot_dim // 4
    freqs = ROPE_THETA ** (-jnp.arange(quarter, dtype=F32) / quarter)
    ang = jnp.concatenate([row[:, None] * freqs, col[:, None] * freqs], axis=-1)
    cos, sin = jnp.cos(ang), jnp.sin(ang)
    zero = jnp.zeros_like(sin)
    c = jnp.repeat(cos, 2, axis=-1)
    se = jnp.stack([-sin, zero], axis=-1).reshape(DEC_SEQ, rot_dim)
    so = jnp.stack([zero, sin], axis=-1).reshape(DEC_SEQ, rot_dim)
    pad = ((0, 0), (0, LANE - rot_dim))
    return tuple(jnp.pad(t, pad) for t in (c, se, so))


def _block_diag_gates(lru_wr, lru_wi):
    per = LRU_CHUNK // LRU_BLOCK
    nck = LRU_WIDTH // LRU_CHUNK
    eye = jnp.eye(per, dtype=F32)

    def bd(w):
        w = w.reshape(DEPTH, 2, nck, per, LRU_BLOCK, LRU_BLOCK)
        w = jnp.einsum("dzcakj,ab->dzcakbj", w, eye)
        return w.reshape(DEPTH, 2, nck, LRU_CHUNK, LRU_CHUNK)

    return jnp.concatenate([bd(lru_wr), bd(lru_wi)], axis=-1).astype(BF16)


def _key_buffer(cache, width):
    c = jnp.swapaxes(cache, 0, 1).astype(BF16)
    c = jnp.pad(c, ((0, 0), (0, 0), (0, DEC_SEQ), (0, width - cache.shape[-1])))
    return c.reshape(DEPTH * DEC_BATCH * LAT_KEYS, width)


def _permute_w_uq(w_uq):
    w = w_uq.reshape(DEPTH, MLA_Q_RANK, MLA_HEADS, MLA_NOPE + MLA_ROPE)
    nope = w[..., :MLA_NOPE].reshape(DEPTH, MLA_Q_RANK, MLA_HEADS * MLA_NOPE)
    rope = jnp.pad(w[..., MLA_NOPE:], ((0, 0), (0, 0), (0, 0), (0, LANE - MLA_ROPE)))
    rope = rope.reshape(DEPTH, MLA_Q_RANK, MLA_HEADS * LANE)
    return jnp.concatenate([nope, rope], axis=-1).astype(BF16)


def kernel(x_prompt, x_sample, state_lru, cache_gqa_k, cache_gqa_v, cache_mla_ckv, cache_mla_krope, c,
           c_ctx, w_mod, b_mod, norm_mix_g, norm_ffn_g, w_in, conv_w, conv_b, lru_wr, lru_br, lru_wi, lru_bi,
           lru_lam, gqa_qn, gqa_kn, mla_qn, mla_kvn, w_uq, w_ukv, w_out, w_gate, w_up, w_down, norm_f):
    x = (x_prompt.reshape(N_CTX, D_MODEL), x_sample.reshape(N_LAT, D_MODEL))

    cond8 = jnp.zeros((SUBLANE, D_MODEL), F32).at[0].set(c_ctx).at[1:1 + DEC_BATCH].set(c)
    b_mod3 = b_mod.reshape(DEPTH, 1, N_MOD)
    mod = modulation(cond8, w_mod, b_mod3, 0)

    w_kr16 = jnp.pad(w_in[:, :, IN_MAIN:], ((0, 0), (0, 0), (0, LANE - MLA_ROPE))).astype(BF16)
    w_in_t = jnp.swapaxes(w_in, 1, 2)
    w_uq16 = _permute_w_uq(w_uq)
    w_gates = _block_diag_gates(lru_wr, lru_wi)
    lru_prm = jnp.stack([lru_br, lru_bi, lru_lam], axis=2)
    tables_b = _rope_tables(HEAD_DIM)
    tables_c = _rope_tables(MLA_ROPE)
    h0_ctx = jnp.zeros((BATCH, 2, LRU_WIDTH), F32)
    key_bufs = (_key_buffer(cache_gqa_k.reshape(DEC_BATCH, DEPTH, PAST_LEN, KV_WIDTH), KV_WIDTH),
                _key_buffer(cache_gqa_v.reshape(DEC_BATCH, DEPTH, PAST_LEN, KV_WIDTH), KV_WIDTH),
                _key_buffer(cache_mla_ckv, MLA_KV_RANK),
                _key_buffer(cache_mla_krope, LANE))

    mix = jnp.zeros((N_TOK, D_MIX), BF16)

    st_lru, st_k, st_v, st_ckv, st_kr = [], [], [], [], []
    for l in range(DEPTH):
        mod6 = mod.reshape(SUBLANE, 6, 1, D_MODEL)
        sh1, sc1, g1, sh2, sc2, g2 = (mod6[:, s] for s in range(6))

        h = norm_mod(x, norm_mix_g[l], sc1, sh1)
        if l == 0:
            p, kr = matmul_nt(h, w_in_t, w_kr16, l, n=IN_MAIN, **T_IN)
        else:
            p, kr = matmul_nt(h, w_in16, w_kr16, 0, n=IN_MAIN, w2_layer=l, **T_IN16)

        mix, s_lru = lru_mixer(p, conv_w, conv_b, w_gates, lru_prm, h0_ctx, mix, l, latent=False)
        mix, _ = lru_mixer(p, conv_w, conv_b, w_gates, lru_prm, state_lru[:, l], mix, l, latent=True)

        (q_c, k_c, v_c, cq_c, ckv_c, kr_c, kf_c, ckvf_c) = prep(
            p, kr, gqa_qn[l], gqa_kn[l], mla_qn[l], mla_kvn[l], latent=False)
        q_l, k_buf, v_buf, cq_l, ckv_buf, kr_buf = prep(
            p, kr, gqa_qn[l], gqa_kn[l], mla_qn[l], mla_kvn[l], latent=True,
            tables_b=tables_b, tables_c=tables_c, layer=l, key_bufs=key_bufs)
        key_bufs = (k_buf, v_buf, ckv_buf, kr_buf)
        req0 = l * DEC_BATCH

        mix = gqa_attention(q_c, k_c.reshape(BATCH, SEQ, KV_WIDTH), v_c.reshape(BATCH, SEQ, KV_WIDTH), mix, 0,
                            n_batch=BATCH, t_q=SEQ, t_k=SEQ, tq=SEQ, groups=GQA_KV_HEADS, sub=SEQ)
        mix = gqa_attention(q_l, k_buf.reshape(-1, LAT_KEYS, KV_WIDTH), v_buf.reshape(-1, LAT_KEYS, KV_WIDTH),
                            mix, N_CTX, n_batch=DEC_BATCH, t_q=DEC_SEQ, t_k=LAT_KEYS, groups=1, **A_GQA,
                            kv_b0=req0)

        qm_c = matmul(cq_c, w_uq16, l, out_dtype=BF16, **T_MLA)
        qm_l = matmul_rope_tail(cq_l, w_uq16, l, tables_c, tm=1024, tn=MLA_WIDTH, first_rope_tile=1)
        kv_c = matmul(ckv_c, w_ukv, l, out_dtype=BF16, **T_MLA)
        kv_l = matmul(ckv_buf, w_ukv, l, tm=LAT_KEYS, tn=T_MLA["tn"], out_dtype=BF16,
                      row0=req0 * LAT_KEYS, m=DEC_BATCH * LAT_KEYS)
        mix = mla_attention(qm_c, kv_c.reshape(BATCH, SEQ, -1), kr_c.reshape(BATCH, SEQ, LANE), mix, 0,
                            n_batch=BATCH, t_q=SEQ, t_k=SEQ, tq=SEQ, heads=MLA_HEADS, sub=SEQ)
        mix = mla_attention(qm_l, kv_l.reshape(DEC_BATCH, LAT_KEYS, -1), kr_buf.reshape(-1, LAT_KEYS, LANE),
                            mix, N_CTX, n_batch=DEC_BATCH, t_q=DEC_SEQ, t_k=LAT_KEYS, heads=2, **A_MLA,
                            kr_b0=req0)

        if l == 0:
            x = matmul_gated_residual(mix, w_out, l, x, g1, mix_cols=True, **T_OUT)
        else:
            x = matmul_gated_residual(mix, w_out16, 0, x, g1, mix_cols=True, **T_OUT16)

        h = norm_mod(x, norm_ffn_g[l], sc2, sh2)
        if l + 1 < DEPTH:
            ff, w_down16, mod, w_in16, w_out16 = swiglu_up(
                h, w_gate, w_up, w_down, l, next_args=(cond8, w_mod, b_mod3, w_in_t, w_out), **T_UP)
        else:
            ff, w_down16 = swiglu_up(h, w_gate, w_up, w_down, l, **T_UP)
        x = matmul_gated_residual(ff, w_down16, 0, x, g2, **T_DOWN)

        st_lru.append(s_lru)
        st_k.append(kf_c.reshape(BATCH, SEQ, GQA_KV_HEADS, HEAD_DIM))
        st_v.append(p[:N_CTX, 4096:4096 + KV_WIDTH].reshape(BATCH, SEQ, GQA_KV_HEADS, HEAD_DIM))
        st_ckv.append(ckvf_c.reshape(BATCH, SEQ, MLA_KV_RANK))
        st_kr.append(kr[:N_CTX, :MLA_ROPE].reshape(BATCH, SEQ, MLA_ROPE))

    y_ctx = final_norm(x, norm_f, 0, N_CTX)
    y_lat = final_norm(x, norm_f, N_CTX, N_LAT)
    return (y_ctx.reshape(BATCH, SEQ, D_MODEL), y_lat.reshape(DEC_BATCH, DEC_SEQ, D_MODEL),
            jnp.stack(st_lru, axis=1), jnp.stack(st_k, axis=1), jnp.stack(st_v, axis=1),
            jnp.stack(st_ckv, axis=1), jnp.stack(st_kr, axis=1))
```
